```python
import jax, jax.numpy as jnp
from jax import lax
import numpy as np

D_MODEL = 1024
BATCH = 16
SEQ = 256
DEPTH = 4
DEC_BATCH = 2
DEC_SEQ = 2048
PAST_LEN = 256

GRID_W = 64
N_MIXERS = 3
N_GLA = (DEPTH + 2) // 3
N_CONF = (DEPTH + 1) // 3
N_SCONV = DEPTH // 3
GLA_HEADS = 4
GLA_KEY_WIDTH = D_MODEL // 2
GLA_DK = GLA_KEY_WIDTH // GLA_HEADS
GLA_DV = D_MODEL // GLA_HEADS
GLA_RANK = 16
GLA_GATE_NORM = 16.0
GLA_CHUNK = 64
CONF_WIDTH = 31
SCONV_WIDTH = 3
D_FF = 4 * D_MODEL
N_MOD = 6
LN_EPS = 1e-5
RMS_EPS = 1e-6
ALPHA = (2 * DEPTH) ** 0.25
BETA = (8 * DEPTH) ** -0.25

kernel_name = 'hybrid_gla_conformer_shortconv_diffusion_step'


def layer_norm(x, g, b):
    xf = x.astype(jnp.float32)
    mu = jnp.mean(xf, axis=-1, keepdims=True)
    var = jnp.mean(jnp.square(xf - mu), axis=-1, keepdims=True)
    return ((xf - mu) * lax.rsqrt(var + LN_EPS)).astype(x.dtype) * g + b


def adaln(cvec, w, b):
    m = jax.nn.silu(cvec) @ w + b
    m = m.reshape(m.shape[:-1] + (1, N_MOD, D_MODEL))
    return [m[..., i, :] for i in range(N_MOD)]


def modulate(x, shift, scale):
    return x * (1 + scale) + shift


def dw_conv(x, w, dilation=1):
    pad = (w.shape[0] // 2) * dilation
    return lax.conv_general_dilated(x, w[:, None, :], window_strides=(1,), padding=[(pad, pad)],
                                    rhs_dilation=(dilation,), dimension_numbers=('NWC', 'WIO', 'NWC'),
                                    feature_group_count=x.shape[-1])


def gla_scan(q, k, v, g, s0):
    B, L, H, DK = q.shape
    n = L // GLA_CHUNK
    rs = lambda t: t.reshape(B, n, GLA_CHUNK, H, t.shape[-1])
    q, k, v, g = rs(q), rs(k), rs(v), rs(g)
    bcum = jnp.cumsum(g, axis=2)
    blast = bcum[:, :, -1:]
    q_dec = q * jnp.exp(bcum)
    k_dec = k * jnp.exp(-bcum)
    k_end = k * jnp.exp(blast - bcum)
    causal_in_chunk = jnp.tril(jnp.ones((GLA_CHUNK, GLA_CHUNK), dtype=bool))
    scores = jnp.einsum('bnihd,bnjhd->bnhij', q_dec, k_dec)
    scores = jnp.where(causal_in_chunk, scores, 0.0)
    o_intra = jnp.einsum('bnhij,bnjhe->bnihe', scores, v)
    chunk_kv = jnp.einsum('bnjhd,bnjhe->bnhde', k_end, v)
    chunk_decay = jnp.exp(blast[:, :, 0])

    def step(s, inp):
        dec, kv = inp
        return dec[..., None] * s + kv, s

    s_final, s_starts = lax.scan(step, s0, (jnp.moveaxis(chunk_decay, 1, 0), jnp.moveaxis(chunk_kv, 1, 0)))
    s_starts = jnp.moveaxis(s_starts, 0, 1)
    o_inter = jnp.einsum('bnihd,bnhde->bnihe', q_dec, s_starts)
    return (o_intra + o_inter).reshape(B, L, H, v.shape[-1]), s_final


def gla_mixer(h, w_in, w_ga, w_gb, b_g, gn_g, w_o, s0):
    B, L, _ = h.shape
    f32 = jnp.float32
    q, k, v, r = jnp.split(h @ w_in, [GLA_KEY_WIDTH, 2 * GLA_KEY_WIDTH, 2 * GLA_KEY_WIDTH + D_MODEL], axis=-1)
    q = q.astype(f32).reshape(B, L, GLA_HEADS, GLA_DK) * (GLA_DK ** -0.5)
    k = k.astype(f32).reshape(B, L, GLA_HEADS, GLA_DK)
    v = v.astype(f32).reshape(B, L, GLA_HEADS, GLA_DV)
    z = jnp.einsum('bld,zdr->zblr', h, w_ga)
    z = jnp.einsum('zblr,zrk->zblk', z, w_gb) + b_g[:, None, None, :]
    g = (jax.nn.log_sigmoid(z.astype(f32)) / GLA_GATE_NORM).reshape(2, B, L, GLA_HEADS, GLA_DK)
    s0 = s0.astype(f32)
    o_f, s_f = gla_scan(q, k, v, g[0], s0[:, 0])
    fl = lambda t: jnp.flip(t, axis=1)
    o_b, s_b = gla_scan(fl(q), fl(k), fl(v), fl(g[1]), s0[:, 1])
    o = o_f + fl(o_b)
    o = o * lax.rsqrt(jnp.mean(jnp.square(o), axis=-1, keepdims=True) + RMS_EPS)
    o = o.reshape(B, L, D_MODEL).astype(h.dtype) * gn_g
    return (o * jax.nn.silu(r)) @ w_o, jnp.stack([s_f, s_b], axis=1).astype(h.dtype)


def conformer_conv(h, w_pw1, b_pw1, w_dw, b_dw, ln_g, ln_b, w_pw2, b_pw2, grid):
    B, L, D = h.shape
    a, gt = jnp.split(h @ w_pw1 + b_pw1, 2, axis=-1)
    u = a * jax.nn.sigmoid(gt)
    if grid:
        rows = L // GRID_W
        u = dw_conv(u.reshape(B * rows, GRID_W, D), w_dw).reshape(B, L, D)
    else:
        u = dw_conv(u, w_dw)
    u = jax.nn.silu(layer_norm(u + b_dw, ln_g, ln_b))
    return u @ w_pw2 + b_pw2


def short_conv(h, w_in, w_conv, w_out, grid):
    bg, cg, u = jnp.split(h @ w_in, 3, axis=-1)
    u = dw_conv(cg * u, w_conv, GRID_W if grid else 1)
    return (bg * u) @ w_out


def sqrelu_mlp(h, w1, w2):
    return jnp.square(jax.nn.relu(h @ w1)) @ w2


def setup_inputs(seed: int = 0) -> dict:
    key = jax.random.key(seed)
    ks = jax.random.split(key, 32)
    nrm = lambda k, shape, s: jax.random.normal(k, shape, jnp.float32) * s
    D = D_MODEL
    gla_in_width = 2 * GLA_KEY_WIDTH + 2 * D
    return {
        'x_prompt': nrm(ks[0], (BATCH, SEQ, D), 1.0),
        'x_sample': nrm(ks[1], (DEC_BATCH, DEC_SEQ, D), 1.0),
        'c': nrm(ks[2], (DEC_BATCH, D), 1.0),
        'state_gla': nrm(ks[3], (DEC_BATCH, N_GLA, 2, GLA_HEADS, GLA_DK, GLA_DV), 1.0),
        'c_ctx': nrm(ks[4], (D,), 1.0),
        'mod_w': nrm(ks[5], (DEPTH, D, N_MOD * D), D ** -0.5),
        'mod_b': nrm(ks[6], (DEPTH, N_MOD * D), 0.02),
        'ln_g': 1.0 + nrm(ks[7], (DEPTH, 2, D), 0.02),
        'ln_b': nrm(ks[8], (DEPTH, 2, D), 0.02),
        'ff_w1': nrm(ks[9], (DEPTH, D, D_FF), D ** -0.5),
        'ff_w2': nrm(ks[10], (DEPTH, D_FF, D), D_FF ** -0.5 * BETA),
        'gla_w_in': nrm(ks[11], (N_GLA, D, gla_in_width), D ** -0.5),
        'gla_w_ga': nrm(ks[12], (N_GLA, 2, D, GLA_RANK), D ** -0.5),
        'gla_w_gb': nrm(ks[13], (N_GLA, 2, GLA_RANK, GLA_KEY_WIDTH), GLA_RANK ** -0.5),
        'gla_b_g': nrm(ks[14], (N_GLA, 2, GLA_KEY_WIDTH), 0.02),
        'gla_gn_g': 1.0 + nrm(ks[15], (N_GLA, D), 0.02),
        'gla_w_o': nrm(ks[16], (N_GLA, D, D), D ** -0.5 * BETA),
        'conf_w_pw1': nrm(ks[17], (N_CONF, D, 2 * D), D ** -0.5),
        'conf_b_pw1': nrm(ks[18], (N_CONF, 2 * D), 0.02),
        'conf_w_dw': nrm(ks[19], (N_CONF, CONF_WIDTH, D), CONF_WIDTH ** -0.5),
        'conf_b_dw': nrm(ks[20], (N_CONF, D), 0.02),
        'conf_ln_g': 1.0 + nrm(ks[21], (N_CONF, D), 0.02),
        'conf_ln_b': nrm(ks[22], (N_CONF, D), 0.02),
        'conf_w_pw2': nrm(ks[23], (N_CONF, D, D), D ** -0.5 * BETA),
        'conf_b_pw2': nrm(ks[24], (N_CONF, D), 0.02),
        'sc_w_in': nrm(ks[25], (N_SCONV, D, 3 * D), D ** -0.5),
        'sc_w_conv': nrm(ks[26], (N_SCONV, SCONV_WIDTH, D), SCONV_WIDTH ** -0.5),
        'sc_w_out': nrm(ks[27], (N_SCONV, D, D), D ** -0.5 * BETA),
    }


def reference(x_prompt, x_sample, c, state_gla, c_ctx, mod_w, mod_b, ln_g, ln_b, ff_w1, ff_w2,
              gla_w_in, gla_w_ga, gla_w_gb, gla_b_g, gla_gn_g, gla_w_o,
              conf_w_pw1, conf_b_pw1, conf_w_dw, conf_b_dw, conf_ln_g, conf_ln_b, conf_w_pw2, conf_b_pw2,
              sc_w_in, sc_w_conv, sc_w_out):
    xp, xs = x_prompt, x_sample
    new_states = []
    for l in range(DEPTH):
        kind, j = l % N_MIXERS, l // N_MIXERS
        mp = adaln(c_ctx, mod_w[l], mod_b[l])
        ms = adaln(c, mod_w[l], mod_b[l])
        hp = modulate(xp, mp[0], mp[1])
        hs = modulate(xs, ms[0], ms[1])
        if kind == 0:
            s_zero = jnp.zeros((xp.shape[0], 2, GLA_HEADS, GLA_DK, GLA_DV), xp.dtype)
            op, st = gla_mixer(hp, gla_w_in[j], gla_w_ga[j], gla_w_gb[j], gla_b_g[j], gla_gn_g[j], gla_w_o[j], s_zero)
            os_, _ = gla_mixer(hs, gla_w_in[j], gla_w_ga[j], gla_w_gb[j], gla_b_g[j], gla_gn_g[j], gla_w_o[j], state_gla[:, j])
            new_states.append(st)
        elif kind == 1:
            op = conformer_conv(hp, conf_w_pw1[j], conf_b_pw1[j], conf_w_dw[j], conf_b_dw[j], conf_ln_g[j], conf_ln_b[j], conf_w_pw2[j], conf_b_pw2[j], False)
            os_ = conformer_conv(hs, conf_w_pw1[j], conf_b_pw1[j], conf_w_dw[j], conf_b_dw[j], conf_ln_g[j], conf_ln_b[j], conf_w_pw2[j], conf_b_pw2[j], True)
        else:
            op = short_conv(hp, sc_w_in[j], sc_w_conv[j], sc_w_out[j], False)
            os_ = short_conv(hs, sc_w_in[j], sc_w_conv[j], sc_w_out[j], True)
        xp = layer_norm(ALPHA * xp + mp[2] * op, ln_g[l, 0], ln_b[l, 0])
        xs = layer_norm(ALPHA * xs + ms[2] * os_, ln_g[l, 0], ln_b[l, 0])
        fp = sqrelu_mlp(modulate(xp, mp[3], mp[4]), ff_w1[l], ff_w2[l])
        fs = sqrelu_mlp(modulate(xs, ms[3], ms[4]), ff_w1[l], ff_w2[l])
        xp = layer_norm(ALPHA * xp + mp[5] * fp, ln_g[l, 1], ln_b[l, 1])
        xs = layer_norm(ALPHA * xs + ms[5] * fs, ln_g[l, 1], ln_b[l, 1])
    new_state_gla = jnp.stack(new_states, axis=1)
    return (xp, xs, new_state_gla)
```

```python
import functools

import jax
import jax.numpy as jnp
from jax import lax
from jax.experimental import pallas as pl
from jax.experimental.pallas import tpu as pltpu

F32 = jnp.float32
BF16 = jnp.bfloat16

D = 1024
DEPTH = 4
BATCH, SEQ = 16, 256
DEC_BATCH, DEC_SEQ = 2, 2048
GRID_W = 64
N_MOD = 6
N_CVEC = 1 + DEC_BATCH
NCTX = BATCH * SEQ
NTOK = NCTX + DEC_BATCH * DEC_SEQ
H, DK, DV = 4, 128, 256
KW = H * DK
RANK = 16
CHUNK = 64
GATE_NORM = 16.0
CONF_W = 31
D_FF = 4 * D
LN_EPS = 1e-5
RMS_EPS = 1e-6
ALPHA = (2 * DEPTH) ** 0.25

TM = 256
NTILE = NTOK // TM
CTX_TILES = NCTX // TM
TILES_PER_LAT = DEC_SEQ // TM
N_SEQ = BATCH + DEC_BATCH
TM_MLP = 1024
TF_MLP = 512
TN_MOD = 2048
SUBLANE = 8
VMEM_LIMIT = 56 * 1024 * 1024


def _cparams(*sem):
    return pltpu.CompilerParams(dimension_semantics=sem, vmem_limit_bytes=VMEM_LIMIT)


def _mod_index(row0):
    return jnp.where(row0 < NCTX, 0, 1 + (row0 - NCTX) // DEC_SEQ)


def _resident(shape):
    nd = len(shape)
    return pl.BlockSpec(shape, lambda *_: (0,) * nd, pipeline_mode=pl.Buffered(1))


def _layer_norm(y, g, b):
    mu = jnp.mean(y, axis=-1, keepdims=True)
    yc = y - mu
    var = jnp.mean(yc * yc, axis=-1, keepdims=True)
    return yc * lax.rsqrt(var + LN_EPS) * g + b


def _silu(x):
    return x * jax.nn.sigmoid(x)


def _cast_rows(src_ref, dst_ref, step):
    n = src_ref.shape[0] // step

    def body(r, c):
        rows = pl.ds(pl.multiple_of(r * step, step), step)
        dst_ref[rows, :] = src_ref[rows, :].astype(BF16)
        return c

    lax.fori_loop(0, n, body, 0)


def _adaln_kernel(c_ref, w_ref, b_ref, o_ref):
    s = _silu(c_ref[...]).astype(BF16)
    o_ref[...] = jnp.dot(s, w_ref[...].astype(BF16), preferred_element_type=F32) + b_ref[...]


def _adaln(c8, mod_w, mod_b):
    return pl.pallas_call(
        _adaln_kernel,
        grid=(DEPTH, N_MOD * D // TN_MOD),
        in_specs=[
            pl.BlockSpec((SUBLANE, D), lambda l, n: (0, 0)),
            pl.BlockSpec((None, D, TN_MOD), lambda l, n: (l, 0, n)),
            pl.BlockSpec((None, 1, TN_MOD), lambda l, n: (l, 0, n)),
        ],
        out_specs=pl.BlockSpec((None, SUBLANE, TN_MOD), lambda l, n: (l, 0, n)),
        out_shape=jax.ShapeDtypeStruct((DEPTH, SUBLANE, N_MOD * D), F32),
        compiler_params=_cparams("parallel", "parallel"),
        name="adaln",
    )(c8, mod_w, mod_b.reshape(DEPTH, 1, N_MOD * D))


def _mlp_kernel(x_ref, mod_ref, w1_ref, w2_ref, g_ref, b_ref, o_ref, h_scr, acc_scr):
    j = pl.program_id(1)

    @pl.when(j == 0)
    def _():
        h = x_ref[...] * (1.0 + mod_ref[4:5, :]) + mod_ref[3:4, :]
        h_scr[...] = h.astype(BF16)
        acc_scr[...] = jnp.zeros_like(acc_scr)

    a = jnp.dot(h_scr[...], w1_ref[...].astype(BF16), preferred_element_type=F32)
    a = jnp.square(jnp.maximum(a, 0.0)).astype(BF16)
    acc_scr[...] += jnp.dot(a, w2_ref[...].astype(BF16), preferred_element_type=F32)

    @pl.when(j == pl.num_programs(1) - 1)
    def _():
        y = ALPHA * x_ref[...] + mod_ref[5:6, :] * acc_scr[...]
        o_ref[...] = _layer_norm(y, g_ref[...], b_ref[...])


def _mlp(x, mods_l, w1, w2, ln_g, ln_b):
    return pl.pallas_call(
        _mlp_kernel,
        grid=(NTOK // TM_MLP, D_FF // TF_MLP),
        in_specs=[
            pl.BlockSpec((TM_MLP, D), lambda i, j: (i, 0)),
            pl.BlockSpec((None, SUBLANE, D), lambda i, j: (_mod_index(i * TM_MLP), 0, 0)),
            pl.BlockSpec((D, TF_MLP), lambda i, j: (0, j)),
            pl.BlockSpec((TF_MLP, D), lambda i, j: (j, 0)),
            pl.BlockSpec((1, D), lambda i, j: (0, 0)),
            pl.BlockSpec((1, D), lambda i, j: (0, 0)),
        ],
        out_specs=pl.BlockSpec((TM_MLP, D), lambda i, j: (i, 0)),
        out_shape=jax.ShapeDtypeStruct((NTOK, D), F32),
        scratch_shapes=[pltpu.VMEM((TM_MLP, D), BF16), pltpu.VMEM((TM_MLP, D), F32)],
        compiler_params=_cparams("parallel", "arbitrary"),
        name="mlp",
    )(x, mods_l, w1, w2, ln_g.reshape(1, D), ln_b.reshape(1, D))


CONF_PAD = 16
CONF_RB = 64
CONF_LB = 256


def _conf_conv(upad_ref, wdw_ref, cbuf_ref, u, seg):
    nseg = TM // seg
    stride = seg + 2 * CONF_PAD
    zpad = jnp.zeros((CONF_PAD, D), F32)
    for s in range(nseg):
        base = s * stride
        upad_ref[base:base + CONF_PAD, :] = zpad
        upad_ref[base + CONF_PAD:base + CONF_PAD + seg, :] = u[s * seg:(s + 1) * seg, :]
        upad_ref[base + CONF_PAD + seg:base + stride, :] = zpad

    def lane_block(lb, c):
        lanes = pl.ds(pl.multiple_of(lb * CONF_LB, CONF_LB), CONF_LB)
        for s in range(nseg):
            for rb in range(seg // CONF_RB):
                r0 = s * stride + CONF_PAD + rb * CONF_RB - CONF_W // 2
                acc = jnp.zeros((CONF_RB, CONF_LB), F32)
                for k in range(CONF_W):
                    acc = acc + wdw_ref[k:k + 1, lanes] * upad_ref[r0 + k:r0 + k + CONF_RB, lanes]
                o0 = s * seg + rb * CONF_RB
                cbuf_ref[o0:o0 + CONF_RB, lanes] = acc
        return c

    lax.fori_loop(0, D // CONF_LB, lane_block, 0)


def _conf_kernel(x_ref, mod_ref, w1_ref, b1_ref, wdw_ref, bdw_ref, cg_ref, cb_ref, w2_ref, b2_ref,
                 g_ref, b_ref, o_ref, w1s, w2s, upad, cbuf):
    i = pl.program_id(0)

    @pl.when(i == 0)
    def _():
        _cast_rows(w1_ref, w1s, 128)
        _cast_rows(w2_ref, w2s, 128)

    x = x_ref[...]
    h = (x * (1.0 + mod_ref[1:2, :]) + mod_ref[0:1, :]).astype(BF16)
    ag = jnp.dot(h, w1s[...], preferred_element_type=F32) + b1_ref[...]
    u = ag[:, :D] * jax.nn.sigmoid(ag[:, D:])

    @pl.when(i < CTX_TILES)
    def _():
        _conf_conv(upad, wdw_ref, cbuf, u, SEQ)

    @pl.when(i >= CTX_TILES)
    def _():
        _conf_conv(upad, wdw_ref, cbuf, u, GRID_W)

    uc = _silu(_layer_norm(cbuf[...] + bdw_ref[...], cg_ref[...], cb_ref[...]))
    y = jnp.dot(uc.astype(BF16), w2s[...], preferred_element_type=F32) + b2_ref[...]
    o_ref[...] = _layer_norm(ALPHA * x + mod_ref[2:3, :] * y, g_ref[...], b_ref[...])


def _conformer(x, mods_l, w1, b1, wdw, bdw, cg, cb, w2, b2, ln_g, ln_b):
    wdw_p = jnp.concatenate([wdw, jnp.zeros((1, D), F32)], axis=0)
    row = lambda v: v.reshape(1, -1)
    vec = lambda n: pl.BlockSpec((1, n), lambda i: (0, 0))
    upad_rows = (TM // GRID_W) * (GRID_W + 2 * CONF_PAD)
    return pl.pallas_call(
        _conf_kernel,
        grid=(NTILE,),
        in_specs=[
            pl.BlockSpec((TM, D), lambda i: (i, 0)),
            pl.BlockSpec((None, SUBLANE, D), lambda i: (_mod_index(i * TM), 0, 0)),
            _resident((D, 2 * D)), vec(2 * D),
            pl.BlockSpec((CONF_W + 1, D), lambda i: (0, 0)), vec(D), vec(D), vec(D),
            _resident((D, D)), vec(D), vec(D), vec(D),
        ],
        out_specs=pl.BlockSpec((TM, D), lambda i: (i, 0)),
        out_shape=jax.ShapeDtypeStruct((NTOK, D), F32),
        scratch_shapes=[pltpu.VMEM((D, 2 * D), BF16), pltpu.VMEM((D, D), BF16),
                        pltpu.VMEM((upad_rows, D), F32), pltpu.VMEM((TM, D), F32)],
        compiler_params=_cparams("arbitrary"),
        name="conformer",
    )(x, mods_l, w1, row(b1), wdw_p, row(bdw), row(cg), row(cb), w2, row(b2), row(ln_g), row(ln_b))


def _sconv_kernel(x_ref, xp_ref, xn_ref, mod_ref, win_ref, wc_ref, wout_ref, g_ref, b_ref, o_ref,
                  wins, wouts, cpad, ybuf):
    i = pl.program_id(0)

    @pl.when(i == 0)
    def _():
        _cast_rows(win_ref, wins, 128)
        _cast_rows(wout_ref, wouts, 128)

    scale = 1.0 + mod_ref[1:2, :]
    shift = mod_ref[0:1, :]
    x = x_ref[...]
    h = (x * scale + shift).astype(BF16)
    bg = jnp.dot(h, wins[:, :D], preferred_element_type=F32)
    w0, w1, w2 = wc_ref[0:1, :], wc_ref[1:2, :], wc_ref[2:3, :]

    @pl.when(i < CTX_TILES)
    def _():
        cu = jnp.dot(h, wins[:, D:], preferred_element_type=F32)
        cu = cu[:, :D] * cu[:, D:]
        zrow = jnp.zeros((SUBLANE, D), F32)
        cpad[0:SUBLANE, :] = zrow
        cpad[SUBLANE:SUBLANE + TM, :] = cu
        cpad[SUBLANE + TM:2 * SUBLANE + TM, :] = zrow
        ybuf[...] = (w0 * cpad[SUBLANE - 1:SUBLANE - 1 + TM, :] + w1 * cu
                     + w2 * cpad[SUBLANE + 1:SUBLANE + 1 + TM, :])

    @pl.when(i >= CTX_TILES)
    def _():
        r = (i - CTX_TILES) % TILES_PER_LAT
        hp = (xp_ref[...] * scale + shift).astype(BF16)
        hn = (xn_ref[...] * scale + shift).astype(BF16)
        hcat = jnp.concatenate([hp, h, hn], axis=0)
        cu = jnp.dot(hcat, wins[:, D:], preferred_element_type=F32)
        cu = cu[:, :D] * cu[:, D:]
        halo_up = jnp.where(r > 0, cu[0:GRID_W, :], 0.0)
        halo_dn = jnp.where(r < TILES_PER_LAT - 1, cu[GRID_W + TM:, :], 0.0)
        up = jnp.concatenate([halo_up, cu[GRID_W:TM, :]], axis=0)
        dn = jnp.concatenate([cu[2 * GRID_W:GRID_W + TM, :], halo_dn], axis=0)
        ybuf[...] = w0 * up + w1 * cu[GRID_W:GRID_W + TM, :] + w2 * dn

    y = jnp.dot((bg * ybuf[...]).astype(BF16), wouts[...], preferred_element_type=F32)
    o_ref[...] = _layer_norm(ALPHA * x + mod_ref[2:3, :] * y, g_ref[...], b_ref[...])


def _short_conv(x, mods_l, w_in, w_conv, w_out, ln_g, ln_b):
    wc_p = jnp.concatenate([w_conv, jnp.zeros((SUBLANE - 3, D), F32)], axis=0)
    halo_per_tile = TM // GRID_W
    n_halo = NTOK // GRID_W
    vec = pl.BlockSpec((1, D), lambda i: (0, 0))
    return pl.pallas_call(
        _sconv_kernel,
        grid=(NTILE,),
        in_specs=[
            pl.BlockSpec((TM, D), lambda i: (i, 0)),
            pl.BlockSpec((GRID_W, D), lambda i: (jnp.maximum(i * halo_per_tile - 1, 0), 0)),
            pl.BlockSpec((GRID_W, D), lambda i: (jnp.minimum((i + 1) * halo_per_tile, n_halo - 1), 0)),
            pl.BlockSpec((None, SUBLANE, D), lambda i: (_mod_index(i * TM), 0, 0)),
            _resident((D, 3 * D)),
            pl.BlockSpec((SUBLANE, D), lambda i: (0, 0)),
            _resident((D, D)), vec, vec,
        ],
        out_specs=pl.BlockSpec((TM, D), lambda i: (i, 0)),
        out_shape=jax.ShapeDtypeStruct((NTOK, D), F32),
        scratch_shapes=[pltpu.VMEM((D, 3 * D), BF16), pltpu.VMEM((D, D), BF16),
                        pltpu.VMEM((TM + 2 * SUBLANE, D), F32), pltpu.VMEM((TM, D), F32)],
        compiler_params=_cparams("arbitrary"),
        name="short_conv",
    )(x, x, x, mods_l, w_in, wc_p, w_out, ln_g.reshape(1, D), ln_b.reshape(1, D))


N_CH = TM // CHUNK


def _gla_proj_kernel(x_ref, mod_ref, win_ref, wga_ref, wgb_ref, bg_ref,
                     qd_ref, kd_ref, ket_ref, v_ref, r_ref, dec_ref, wins):
    i = pl.program_id(0)

    @pl.when(i == 0)
    def _():
        _cast_rows(win_ref, wins, 128)

    h = (x_ref[...] * (1.0 + mod_ref[1:2, :]) + mod_ref[0:1, :]).astype(BF16)
    proj = jnp.dot(h, wins[...], preferred_element_type=F32)
    q = proj[:, :KW] * (DK ** -0.5)
    k = proj[:, KW:2 * KW]
    v_ref[...] = proj[:, 2 * KW:2 * KW + D].astype(BF16)
    r_ref[...] = proj[:, 2 * KW + D:]

    za = jnp.dot(h, wga_ref[...].astype(BF16), preferred_element_type=F32)
    z = jnp.dot(za.astype(BF16), wgb_ref[...].astype(BF16), preferred_element_type=F32) + bg_ref[...]
    g = (jnp.minimum(z, 0.0) - jnp.log1p(jnp.exp(-jnp.abs(z)))) * (1.0 / GATE_NORM)

    g_hi = g.astype(BF16)
    g_lo = (g - g_hi.astype(F32)).astype(BF16)
    row = lax.broadcasted_iota(jnp.int32, (TM, TM), 0)
    col = lax.broadcasted_iota(jnp.int32, (TM, TM), 1)
    same = (row // CHUNK) == (col // CHUNK)
    tri_f = jnp.where(same & (col <= row), 1.0, 0.0).astype(BF16)
    tri_b = jnp.where(same & (col >= row), 1.0, 0.0).astype(BF16)
    bcum = (
        jnp.dot(tri_f, g_hi[:, :KW], preferred_element_type=F32)
        + jnp.dot(tri_f, g_lo[:, :KW], preferred_element_type=F32),
        jnp.dot(tri_b, g_hi[:, KW:], preferred_element_type=F32)
        + jnp.dot(tri_b, g_lo[:, KW:], preferred_element_type=F32),
    )
    for d in range(2):
        k_end, dec = [], []
        for c in range(N_CH):
            rows = slice(c * CHUNK, (c + 1) * CHUNK)
            b = bcum[d][rows, :]
            last = b[CHUNK - 1:CHUNK, :] if d == 0 else b[0:1, :]
            qd_ref[d, rows, :] = (q[rows, :] * jnp.exp(b)).astype(BF16)
            kd_ref[d, rows, :] = (k[rows, :] * jnp.exp(-b)).astype(BF16)
            k_end.append(k[rows, :] * jnp.exp(last - b))
            dec.append(jnp.exp(last))
        ket_ref[d] = jnp.concatenate(k_end, axis=0).T.astype(BF16)
        dec_ref[d] = jnp.concatenate(dec + [jnp.zeros((SUBLANE - N_CH, KW), F32)], axis=0)


def _gla_proj(x, mods_l, w_in, w_ga, w_gb, b_g):
    wga = jnp.transpose(w_ga, (1, 0, 2)).reshape(D, 2 * RANK)
    zero = jnp.zeros((RANK, KW), F32)
    wgb = jnp.concatenate([jnp.concatenate([w_gb[0], zero], axis=1),
                           jnp.concatenate([zero, w_gb[1]], axis=1)], axis=0)
    return pl.pallas_call(
        _gla_proj_kernel,
        grid=(NTILE,),
        in_specs=[
            pl.BlockSpec((TM, D), lambda i: (i, 0)),
            pl.BlockSpec((None, SUBLANE, D), lambda i: (_mod_index(i * TM), 0, 0)),
            _resident((D, 2 * KW + 2 * D)),
            pl.BlockSpec((D, 2 * RANK), lambda i: (0, 0)),
            pl.BlockSpec((2 * RANK, 2 * KW), lambda i: (0, 0)),
            pl.BlockSpec((1, 2 * KW), lambda i: (0, 0)),
        ],
        out_specs=[
            pl.BlockSpec((2, TM, KW), lambda i: (0, i, 0)),
            pl.BlockSpec((2, TM, KW), lambda i: (0, i, 0)),
            pl.BlockSpec((2, KW, TM), lambda i: (0, 0, i)),
            pl.BlockSpec((TM, D), lambda i: (i, 0)),
            pl.BlockSpec((TM, D), lambda i: (i, 0)),
            pl.BlockSpec((2, None, SUBLANE, KW), lambda i: (0, i, 0, 0)),
        ],
        out_shape=[
            jax.ShapeDtypeStruct((2, NTOK, KW), BF16),
            jax.ShapeDtypeStruct((2, NTOK, KW), BF16),
            jax.ShapeDtypeStruct((2, KW, NTOK), BF16),
            jax.ShapeDtypeStruct((NTOK, D), BF16),
            jax.ShapeDtypeStruct((NTOK, D), F32),
            jax.ShapeDtypeStruct((2, NTILE, SUBLANE, KW), F32),
        ],
        scratch_shapes=[pltpu.VMEM((D, 2 * KW + 2 * D), BF16)],
        compiler_params=_cparams("arbitrary"),
        name="gla_proj",
    )(x, mods_l, w_in, wga, wgb, b_g.reshape(1, 2 * KW))


def _scan_tile(i, rev):
    return NTILE - 1 - i if rev else i


def _scan_seq(t):
    return jnp.where(t < CTX_TILES, t, CTX_TILES + (t - CTX_TILES) // TILES_PER_LAT)


def _gla_scan_kernel(qd_ref, kd_ref, ket_ref, v_ref, dec_ref, s0_ref, o_ref, st_ref, s_scr, *, rev):
    t = _scan_tile(pl.program_id(0), rev)
    is_ctx = t < CTX_TILES
    r = (t - CTX_TILES) % TILES_PER_LAT
    first_r, last_r = (TILES_PER_LAT - 1, 0) if rev else (0, TILES_PER_LAT - 1)

    @pl.when(is_ctx)
    def _():
        s_scr[...] = jnp.zeros_like(s_scr)

    @pl.when(jnp.logical_and(jnp.logical_not(is_ctx), r == first_r))
    def _():
        s_scr[...] = s0_ref[...]

    row = lax.broadcasted_iota(jnp.int32, (TM, TM), 0)
    col = lax.broadcasted_iota(jnp.int32, (TM, TM), 1)
    same = (row // CHUNK) == (col // CHUNK)
    mask = same & ((col >= row) if rev else (col <= row))
    kcol = lax.broadcasted_iota(jnp.int32, (DK, TM), 1) // CHUNK
    chunks = range(N_CH - 1, -1, -1) if rev else range(N_CH)

    for hd in range(H):
        qh = qd_ref[:, hd * DK:(hd + 1) * DK]
        kh = kd_ref[:, hd * DK:(hd + 1) * DK]
        vh = v_ref[:, hd * DV:(hd + 1) * DV]
        keth = ket_ref[hd * DK:(hd + 1) * DK, :]
        sc = lax.dot_general(qh, kh, (((1,), (1,)), ((), ())), preferred_element_type=F32)
        p = jnp.where(mask, sc, 0.0).astype(BF16)
        o_intra = jnp.dot(p, vh, preferred_element_type=F32)
        s = s_scr[hd]
        for c in chunks:
            rows = slice(c * CHUNK, (c + 1) * CHUNK)
            o_inter = jnp.dot(qh[rows, :], s.astype(BF16), preferred_element_type=F32)
            o_ref[rows, hd * DV:(hd + 1) * DV] = o_intra[rows, :] + o_inter
            kv = jnp.dot(jnp.where(kcol == c, keth, jnp.zeros_like(keth)), vh,
                         preferred_element_type=F32)
            dec_row = dec_ref[c:c + 1, hd * DK:(hd + 1) * DK]
            dec_col = jnp.broadcast_to(dec_row, (DK, DK)).T
            s = s * jnp.concatenate([dec_col, dec_col], axis=1) + kv
        s_scr[hd] = s

    @pl.when(jnp.logical_or(is_ctx, r == last_r))
    def _():
        st_ref[...] = s_scr[...]


def _gla_scan(qd, kd, ket, v, dec, state_gla, j, rev):
    d = 1 if rev else 0
    tile = lambda i: _scan_tile(i, rev)
    lat = lambda i: jnp.clip((tile(i) - CTX_TILES) // TILES_PER_LAT, 0, DEC_BATCH - 1)
    return pl.pallas_call(
        functools.partial(_gla_scan_kernel, rev=rev),
        grid=(NTILE,),
        in_specs=[
            pl.BlockSpec((None, TM, KW), lambda i: (d, tile(i), 0)),
            pl.BlockSpec((None, TM, KW), lambda i: (d, tile(i), 0)),
            pl.BlockSpec((None, KW, TM), lambda i: (d, 0, tile(i))),
            pl.BlockSpec((TM, D), lambda i: (tile(i), 0)),
            pl.BlockSpec((None, None, SUBLANE, KW), lambda i: (d, tile(i), 0, 0)),
            pl.BlockSpec((None, None, None, H, DK, DV), lambda i: (lat(i), j, d, 0, 0, 0)),
        ],
        out_specs=[
            pl.BlockSpec((TM, D), lambda i: (tile(i), 0)),
            pl.BlockSpec((None, H, DK, DV), lambda i: (_scan_seq(tile(i)), 0, 0, 0)),
        ],
        out_shape=[
            jax.ShapeDtypeStruct((NTOK, D), F32),
            jax.ShapeDtypeStruct((N_SEQ, H, DK, DV), F32),
        ],
        scratch_shapes=[pltpu.VMEM((H, DK, DV), F32)],
        compiler_params=_cparams("arbitrary"),
        name="gla_scan_bwd" if rev else "gla_scan_fwd",
    )(qd, kd, ket, v, dec, state_gla)


def _gla_post_kernel(of_ref, ob_ref, r_ref, x_ref, mod_ref, gn_ref, wo_ref, g_ref, b_ref, o_ref, wos):
    @pl.when(pl.program_id(0) == 0)
    def _():
        _cast_rows(wo_ref, wos, 128)

    o = of_ref[...] + ob_ref[...]
    parts = []
    for hd in range(H):
        oh = o[:, hd * DV:(hd + 1) * DV]
        ms = jnp.mean(oh * oh, axis=-1, keepdims=True)
        parts.append(oh * lax.rsqrt(ms + RMS_EPS))
    on = jnp.concatenate(parts, axis=1) * gn_ref[...]
    y = jnp.dot((on * _silu(r_ref[...])).astype(BF16), wos[...], preferred_element_type=F32)
    o_ref[...] = _layer_norm(ALPHA * x_ref[...] + mod_ref[2:3, :] * y, g_ref[...], b_ref[...])


def _gla_post(o_f, o_b, r, x, mods_l, gn_g, w_o, ln_g, ln_b):
    tile = pl.BlockSpec((TM, D), lambda i: (i, 0))
    vec = pl.BlockSpec((1, D), lambda i: (0, 0))
    return pl.pallas_call(
        _gla_post_kernel,
        grid=(NTILE,),
        in_specs=[tile, tile, tile, tile,
                  pl.BlockSpec((None, SUBLANE, D), lambda i: (_mod_index(i * TM), 0, 0)),
                  vec, _resident((D, D)), vec, vec],
        out_specs=tile,
        out_shape=jax.ShapeDtypeStruct((NTOK, D), F32),
        scratch_shapes=[pltpu.VMEM((D, D), BF16)],
        compiler_params=_cparams("arbitrary"),
        name="gla_post",
    )(o_f, o_b, r, x, mods_l, gn_g.reshape(1, D), w_o, ln_g.reshape(1, D), ln_b.reshape(1, D))


def kernel(x_prompt, x_sample, c, state_gla, c_ctx, mod_w, mod_b, ln_g, ln_b, ff_w1, ff_w2, gla_w_in, gla_w_ga, gla_w_gb, gla_b_g, gla_gn_g, gla_w_o, conf_w_pw1, conf_b_pw1, conf_w_dw, conf_b_dw, conf_ln_g, conf_ln_b, conf_w_pw2, conf_b_pw2, sc_w_in, sc_w_conv, sc_w_out):
    assert x_prompt.shape == (BATCH, SEQ, D) and x_sample.shape == (DEC_BATCH, DEC_SEQ, D)
    x = jnp.concatenate([x_prompt.reshape(NCTX, D), x_sample.reshape(NTOK - NCTX, D)], axis=0)

    c8 = jnp.concatenate([c_ctx[None, :], c, jnp.zeros((SUBLANE - N_CVEC, D), F32)], axis=0)
    mods = _adaln(c8, mod_w, mod_b)[:, :N_CVEC, :].reshape(DEPTH, N_CVEC, N_MOD, D)
    mods = jnp.pad(mods, ((0, 0), (0, 0), (0, SUBLANE - N_MOD), (0, 0)))

    states = []
    for l in range(DEPTH):
        kind, j = l % 3, l // 3
        m = mods[l]
        if kind == 0:
            qd, kd, ket, v, r, dec = _gla_proj(x, m, gla_w_in[j], gla_w_ga[j], gla_w_gb[j], gla_b_g[j])
            o_f, st_f = _gla_scan(qd, kd, ket, v, dec, state_gla, j, rev=False)
            o_b, st_b = _gla_scan(qd, kd, ket, v, dec, state_gla, j, rev=True)
            states.append(jnp.stack([st_f[:BATCH], st_b[:BATCH]], axis=1))
            x = _gla_post(o_f, o_b, r, x, m, gla_gn_g[j], gla_w_o[j], ln_g[l, 0], ln_b[l, 0])
        elif kind == 1:
            x = _conformer(x, m, conf_w_pw1[j], conf_b_pw1[j], conf_w_dw[j], conf_b_dw[j], conf_ln_g[j],
                           conf_ln_b[j], conf_w_pw2[j], conf_b_pw2[j], ln_g[l, 0], ln_b[l, 0])
        else:
            x = _short_conv(x, m, sc_w_in[j], sc_w_conv[j], sc_w_out[j], ln_g[l, 0], ln_b[l, 0])
        x = _mlp(x, m, ff_w1[l], ff_w2[l], ln_g[l, 1], ln_b[l, 1])

    y_prompt = x[:NCTX].reshape(BATCH, SEQ, D)
    y_sample = x[NCTX:].reshape(DEC_BATCH, DEC_SEQ, D)
    return (y_prompt, y_sample, jnp.stack(states, axis=1))
```

```python
import functools

import jax
import jax.numpy as jnp
from jax import lax
from jax.experimental import pallas as pl
from jax.experimental.pallas import tpu as pltpu

F32 = jnp.float32
BF16 = jnp.bfloat16

D = 1024
DEPTH = 4
BATCH, SEQ = 16, 256
DEC_BATCH, DEC_SEQ = 2, 2048
GRID_W = 64
N_MOD = 6
N_CVEC = 1 + DEC_BATCH
NCTX = BATCH * SEQ
NLAT = DEC_BATCH * DEC_SEQ
NTOK = NCTX + NLAT
H, DK, DV = 4, 128, 256
KW = H * DK
RANK = 16
CHUNK = 64
GATE_NORM = 16.0
CONF_W = 31
D_FF = 4 * D
LN_EPS = 1e-5
RMS_EPS = 1e-6
ALPHA = (2 * DEPTH) ** 0.25

TM = 256
NTILE = NTOK // TM
CTX_TILES = NCTX // TM
TILES_PER_LAT = DEC_SEQ // TM
N_SEQ = BATCH + DEC_BATCH
TM_MLP = 512
TF_MLP = 1024
NJ_MLP = D_FF // TF_MLP
TN_MOD = 2048
SUBLANE = 8
VMEM_LIMIT = 58 * 1024 * 1024


def _cparams(*sem):
    return pltpu.CompilerParams(dimension_semantics=sem, vmem_limit_bytes=VMEM_LIMIT)


def _mod_rows(mod_ref, row0, ks):
    m = jnp.where(row0 < NCTX, 0, 1 + (row0 - NCTX) // DEC_SEQ)
    return [mod_ref[pl.ds(m, 1), k * D:(k + 1) * D] for k in ks]


def _mod_spec(l):
    return pl.BlockSpec((None, SUBLANE, N_MOD * D), lambda *_: (l, 0, 0))


def _stacked(block, j, single_buffer=False):
    nd = len(block)
    mode = dict(pipeline_mode=pl.Buffered(1)) if single_buffer else {}
    return pl.BlockSpec((None,) + block, lambda *_: (j,) + (0,) * nd, **mode)


def _x_specs(split, tm):
    if not split:
        return [pl.BlockSpec((tm, D), lambda i, *_: (i, 0))]
    nc = NCTX // tm
    return [pl.BlockSpec((tm, D), lambda i, *_: (jnp.minimum(i, nc - 1), 0)),
            pl.BlockSpec((tm, D), lambda i, *_: (jnp.maximum(i - nc, 0), 0))]


def _x_args(x):
    return list(x) if isinstance(x, tuple) else [x]


def _read_x(x_refs, tm):
    if len(x_refs) == 1:
        return x_refs[0][...]
    return jnp.where(pl.program_id(0) < NCTX // tm, x_refs[0][...], x_refs[1][...])


def _layer_norm(y, g, b):
    mu = jnp.mean(y, axis=-1, keepdims=True)
    yc = y - mu
    var = jnp.mean(yc * yc, axis=-1, keepdims=True)
    return yc * lax.rsqrt(var + LN_EPS) * g + b


def _silu(x):
    return x * jax.nn.sigmoid(x)


def _cast_rows(src_ref, dst_ref, step):
    n = src_ref.shape[0] // step

    def body(r, c):
        rows = pl.ds(pl.multiple_of(r * step, step), step)
        dst_ref[rows, :] = src_ref[rows, :].astype(BF16)
        return c

    lax.fori_loop(0, n, body, 0)


def _vec3(a):
    return a.reshape(-1, 1, a.shape[-1])


def _adaln_kernel(c_ref, w_ref, b_ref, o_ref):
    s = _silu(c_ref[...]).astype(BF16)
    o_ref[...] = jnp.dot(s, w_ref[...].astype(BF16), preferred_element_type=F32) + b_ref[...]


def _adaln(c8, mod_w, mod_b):
    return pl.pallas_call(
        _adaln_kernel,
        grid=(DEPTH, N_MOD * D // TN_MOD),
        in_specs=[
            pl.BlockSpec((SUBLANE, D), lambda l, n: (0, 0)),
            pl.BlockSpec((None, D, TN_MOD), lambda l, n: (l, 0, n)),
            pl.BlockSpec((None, 1, TN_MOD), lambda l, n: (l, 0, n)),
        ],
        out_specs=pl.BlockSpec((None, SUBLANE, TN_MOD), lambda l, n: (l, 0, n)),
        out_shape=jax.ShapeDtypeStruct((DEPTH, SUBLANE, N_MOD * D), F32),
        compiler_params=_cparams("parallel", "parallel"),
        name="adaln",
    )(c8, mod_w, _vec3(mod_b))


def _mlp_kernel(x_ref, mod_ref, w1_ref, w2_ref, g_ref, b_ref, *rest, split_out):
    o_refs, (w1s, w2s, h_scr, acc_scr) = rest[:-4], rest[-4:]
    i, j = pl.program_id(0), pl.program_id(1)
    shift, scale, gate = _mod_rows(mod_ref, i * TM_MLP, (3, 4, 5))

    @pl.when(i == 0)
    def _():
        w1s[j] = w1_ref[...].astype(BF16)
        w2s[j] = w2_ref[...].astype(BF16)

    @pl.when(j == 0)
    def _():
        h_scr[...] = (x_ref[...] * (1.0 + scale) + shift).astype(BF16)
        acc_scr[...] = jnp.zeros_like(acc_scr)

    a = jnp.dot(h_scr[...], w1s[j], preferred_element_type=F32)
    a = jnp.square(jnp.maximum(a, 0.0)).astype(BF16)
    acc_scr[...] += jnp.dot(a, w2s[j], preferred_element_type=F32)

    @pl.when(j == NJ_MLP - 1)
    def _():
        y = _layer_norm(ALPHA * x_ref[...] + gate * acc_scr[...], g_ref[...], b_ref[...])
        if not split_out:
            o_refs[0][...] = y
        else:
            @pl.when(i < NCTX // TM_MLP)
            def _():
                o_refs[0][...] = y

            @pl.when(i >= NCTX // TM_MLP)
            def _():
                o_refs[1][...] = y


def _mlp(x, mods, l, w1, w2, ln_g, ln_b, split_out):
    nc = NCTX // TM_MLP
    chunk = lambda i, j: jnp.where(i == 0, j, NJ_MLP - 1)
    if split_out:
        out_specs = [pl.BlockSpec((TM_MLP, D), lambda i, j: (jnp.minimum(i, nc - 1), 0)),
                     pl.BlockSpec((TM_MLP, D), lambda i, j: (jnp.maximum(i - nc, 0), 0))]
        out_shape = [jax.ShapeDtypeStruct((NCTX, D), F32), jax.ShapeDtypeStruct((NLAT, D), F32)]
    else:
        out_specs = pl.BlockSpec((TM_MLP, D), lambda i, j: (i, 0))
        out_shape = jax.ShapeDtypeStruct((NTOK, D), F32)
    return pl.pallas_call(
        functools.partial(_mlp_kernel, split_out=split_out),
        grid=(NTOK // TM_MLP, NJ_MLP),
        in_specs=[
            pl.BlockSpec((TM_MLP, D), lambda i, j: (i, 0)),
            _mod_spec(l),
            pl.BlockSpec((None, D, TF_MLP), lambda i, j: (l, 0, chunk(i, j))),
            pl.BlockSpec((None, TF_MLP, D), lambda i, j: (l, chunk(i, j), 0)),
            _stacked((1, D), 2 * l + 1), _stacked((1, D), 2 * l + 1),
        ],
        out_specs=out_specs,
        out_shape=out_shape,
        scratch_shapes=[pltpu.VMEM((NJ_MLP, D, TF_MLP), BF16), pltpu.VMEM((NJ_MLP, TF_MLP, D), BF16),
                        pltpu.VMEM((TM_MLP, D), BF16), pltpu.VMEM((TM_MLP, D), F32)],
        compiler_params=_cparams("arbitrary", "arbitrary"),
        name="mlp",
    )(x, mods, w1, w2, _vec3(ln_g), _vec3(ln_b))


CONF_PAD = 16
CONF_RB = 64
CONF_LB = 256
CONF_ROWS = (TM // GRID_W) * (GRID_W + 2 * CONF_PAD)


def _conf_conv(upad_ref, shf_ref, wdw_ref, cbuf_ref, u, seg):
    nseg = TM // seg
    stride = seg + 2 * CONF_PAD
    used = nseg * stride
    zpad = jnp.zeros((CONF_PAD, D), F32)
    for s in range(nseg):
        base = s * stride
        upad_ref[base:base + CONF_PAD, :] = zpad
        upad_ref[base + CONF_PAD:base + CONF_PAD + seg, :] = u[s * seg:(s + 1) * seg, :]
        upad_ref[base + CONF_PAD + seg:base + stride, :] = zpad

    def lane_block(lb, c):
        lanes = pl.ds(pl.multiple_of(lb * CONF_LB, CONF_LB), CONF_LB)
        for b in range(1, SUBLANE):
            shf_ref[b - 1, 0:used - SUBLANE, :] = upad_ref[b:b + used - SUBLANE, lanes]
        for s in range(nseg):
            for rb in range(seg // CONF_RB):
                r0 = s * stride + CONF_PAD + rb * CONF_RB
                acc = jnp.zeros((CONF_RB, CONF_LB), F32)
                for k in range(CONF_W):
                    a, b = divmod(k - CONF_W // 2, SUBLANE)
                    rows = slice(r0 + SUBLANE * a, r0 + SUBLANE * a + CONF_RB)
                    src = upad_ref[rows, lanes] if b == 0 else shf_ref[b - 1, rows, :]
                    acc = acc + wdw_ref[k:k + 1, lanes] * src
                o0 = s * seg + rb * CONF_RB
                cbuf_ref[o0:o0 + CONF_RB, lanes] = acc
        return c

    lax.fori_loop(0, D // CONF_LB, lane_block, 0)


def _conf_kernel(x_ref, mod_ref, w1_ref, b1_ref, wdw_ref, bdw_ref, cg_ref, cb_ref, w2_ref, b2_ref,
                 g_ref, b_ref, o_ref, w1s, w2s, upad, shf, cbuf):
    i = pl.program_id(0)
    shift, scale, gate = _mod_rows(mod_ref, i * TM, (0, 1, 2))

    @pl.when(i == 0)
    def _():
        _cast_rows(w1_ref, w1s, 128)
        _cast_rows(w2_ref, w2s, 128)

    x = x_ref[...]
    h = (x * (1.0 + scale) + shift).astype(BF16)
    ag = jnp.dot(h, w1s[...], preferred_element_type=F32) + b1_ref[...]
    u = ag[:, :D] * jax.nn.sigmoid(ag[:, D:])

    @pl.when(i < CTX_TILES)
    def _():
        _conf_conv(upad, shf, wdw_ref, cbuf, u, SEQ)

    @pl.when(i >= CTX_TILES)
    def _():
        _conf_conv(upad, shf, wdw_ref, cbuf, u, GRID_W)

    uc = _silu(_layer_norm(cbuf[...] + bdw_ref[...], cg_ref[...], cb_ref[...]))
    y = jnp.dot(uc.astype(BF16), w2s[...], preferred_element_type=F32) + b2_ref[...]
    o_ref[...] = _layer_norm(ALPHA * x + gate * y, g_ref[...], b_ref[...])


def _conformer(x, mods, l, j, w1, b1, wdw, bdw, cg, cb, w2, b2, ln_g, ln_b):
    return pl.pallas_call(
        _conf_kernel,
        grid=(NTILE,),
        in_specs=[
            pl.BlockSpec((TM, D), lambda i: (i, 0)),
            _mod_spec(l),
            _stacked((D, 2 * D), j, True), _stacked((1, 2 * D), j),
            _stacked((CONF_W, D), j), _stacked((1, D), j), _stacked((1, D), j), _stacked((1, D), j),
            _stacked((D, D), j, True), _stacked((1, D), j),
            _stacked((1, D), 2 * l), _stacked((1, D), 2 * l),
        ],
        out_specs=pl.BlockSpec((TM, D), lambda i: (i, 0)),
        out_shape=jax.ShapeDtypeStruct((NTOK, D), F32),
        scratch_shapes=[pltpu.VMEM((D, 2 * D), BF16), pltpu.VMEM((D, D), BF16),
                        pltpu.VMEM((CONF_ROWS, D), F32),
                        pltpu.VMEM((SUBLANE - 1, CONF_ROWS, CONF_LB), F32),
                        pltpu.VMEM((TM, D), F32)],
        compiler_params=_cparams("arbitrary"),
        name="conformer",
    )(x, mods, w1, _vec3(b1), wdw, _vec3(bdw), _vec3(cg), _vec3(cb), w2, _vec3(b2),
      _vec3(ln_g), _vec3(ln_b))


def _sconv_kernel(x_ref, xp_ref, xn_ref, mod_ref, win_ref, wc_ref, wout_ref, g_ref, b_ref, o_ref,
                  wins, wouts, cpad, ybuf):
    i = pl.program_id(0)
    shift, scale, gate = _mod_rows(mod_ref, i * TM, (0, 1, 2))

    @pl.when(i == 0)
    def _():
        _cast_rows(win_ref, wins, 128)
        _cast_rows(wout_ref, wouts, 128)

    scale = 1.0 + scale
    x = x_ref[...]
    h = (x * scale + shift).astype(BF16)
    bg = jnp.dot(h, wins[:, :D], preferred_element_type=F32)
    w0, w1, w2 = wc_ref[0:1, :], wc_ref[1:2, :], wc_ref[2:3, :]

    @pl.when(i < CTX_TILES)
    def _():
        cu = jnp.dot(h, wins[:, D:], preferred_element_type=F32)
        cu = cu[:, :D] * cu[:, D:]
        zrow = jnp.zeros((SUBLANE, D), F32)
        cpad[0:SUBLANE, :] = zrow
        cpad[SUBLANE:SUBLANE + TM, :] = cu
        cpad[SUBLANE + TM:2 * SUBLANE + TM, :] = zrow
        ybuf[...] = (w0 * cpad[SUBLANE - 1:SUBLANE - 1 + TM, :] + w1 * cu
                     + w2 * cpad[SUBLANE + 1:SUBLANE + 1 + TM, :])

    @pl.when(i >= CTX_TILES)
    def _():
        r = (i - CTX_TILES) % TILES_PER_LAT
        hp = (xp_ref[...] * scale + shift).astype(BF16)
        hn = (xn_ref[...] * scale + shift).astype(BF16)
        hcat = jnp.concatenate([hp, h, hn], axis=0)
        cu = jnp.dot(hcat, wins[:, D:], preferred_element_type=F32)
        cu = cu[:, :D] * cu[:, D:]
        halo_up = jnp.where(r > 0, cu[0:GRID_W, :], 0.0)
        halo_dn = jnp.where(r < TILES_PER_LAT - 1, cu[GRID_W + TM:, :], 0.0)
        up = jnp.concatenate([halo_up, cu[GRID_W:TM, :]], axis=0)
        dn = jnp.concatenate([cu[2 * GRID_W:GRID_W + TM, :], halo_dn], axis=0)
        ybuf[...] = w0 * up + w1 * cu[GRID_W:GRID_W + TM, :] + w2 * dn

    y = jnp.dot((bg * ybuf[...]).astype(BF16), wouts[...], preferred_element_type=F32)
    o_ref[...] = _layer_norm(ALPHA * x + gate * y, g_ref[...], b_ref[...])


def _short_conv(x, mods, l, j, w_in, w_conv, w_out, ln_g, ln_b):
    halo_per_tile = TM // GRID_W
    n_halo = NTOK // GRID_W
    return pl.pallas_call(
        _sconv_kernel,
        grid=(NTILE,),
        in_specs=[
            pl.BlockSpec((TM, D), lambda i: (i, 0)),
            pl.BlockSpec((GRID_W, D), lambda i: (jnp.maximum(i * halo_per_tile - 1, 0), 0)),
            pl.BlockSpec((GRID_W, D), lambda i: (jnp.minimum((i + 1) * halo_per_tile, n_halo - 1), 0)),
            _mod_spec(l),
            _stacked((D, 3 * D), j, True),
            _stacked((3, D), j),
            _stacked((D, D), j, True),
            _stacked((1, D), 2 * l), _stacked((1, D), 2 * l),
        ],
        out_specs=pl.BlockSpec((TM, D), lambda i: (i, 0)),
        out_shape=jax.ShapeDtypeStruct((NTOK, D), F32),
        scratch_shapes=[pltpu.VMEM((D, 3 * D), BF16), pltpu.VMEM((D, D), BF16),
                        pltpu.VMEM((TM + 2 * SUBLANE, D), F32), pltpu.VMEM((TM, D), F32)],
        compiler_params=_cparams("arbitrary"),
        name="short_conv",
    )(x, x, x, mods, w_in, w_conv, w_out, _vec3(ln_g), _vec3(ln_b))


N_CH = TM // CHUNK


def _gla_proj_kernel(*refs, n_x):
    x_refs = refs[:n_x]
    (mod_ref, win_ref, wga_ref, wgb_ref, bg_ref,
     qd_ref, kd_ref, ket_ref, v_ref, r_ref, dec_ref, wins) = refs[n_x:]
    i = pl.program_id(0)
    shift, scale = _mod_rows(mod_ref, i * TM, (0, 1))

    @pl.when(i == 0)
    def _():
        _cast_rows(win_ref, wins, 128)

    h = (_read_x(x_refs, TM) * (1.0 + scale) + shift).astype(BF16)
    proj = jnp.dot(h, wins[...], preferred_element_type=F32)
    q = proj[:, :KW] * (DK ** -0.5)
    k = proj[:, KW:2 * KW]
    v_ref[...] = proj[:, 2 * KW:2 * KW + D].astype(BF16)
    r_ref[...] = proj[:, 2 * KW + D:]

    row = lax.broadcasted_iota(jnp.int32, (TM, TM), 0)
    col = lax.broadcasted_iota(jnp.int32, (TM, TM), 1)
    same = (row // CHUNK) == (col // CHUNK)
    for d in range(2):
        za = jnp.dot(h, wga_ref[d].astype(BF16), preferred_element_type=F32)
        z = jnp.dot(za.astype(BF16), wgb_ref[d].astype(BF16), preferred_element_type=F32) + bg_ref[d]
        g = (jnp.minimum(z, 0.0) - jnp.log1p(jnp.exp(-jnp.abs(z)))) * (1.0 / GATE_NORM)
        g_hi = g.astype(BF16)
        g_lo = (g - g_hi.astype(F32)).astype(BF16)
        tri = jnp.where(same & ((col <= row) if d == 0 else (col >= row)), 1.0, 0.0).astype(BF16)
        bcum = (jnp.dot(tri, g_hi, preferred_element_type=F32)
                + jnp.dot(tri, g_lo, preferred_element_type=F32))
        k_end, dec = [], []
        for c in range(N_CH):
            rows = slice(c * CHUNK, (c + 1) * CHUNK)
            b = bcum[rows, :]
            last = b[CHUNK - 1:CHUNK, :] if d == 0 else b[0:1, :]
            qd_ref[d, rows, :] = (q[rows, :] * jnp.exp(b)).astype(BF16)
            kd_ref[d, rows, :] = (k[rows, :] * jnp.exp(-b)).astype(BF16)
            k_end.append(k[rows, :] * jnp.exp(last - b))
            dec.append(jnp.exp(last))
        ket_ref[d] = jnp.concatenate(k_end, axis=0).T.astype(BF16)
        dec_ref[d] = jnp.concatenate(dec + [jnp.zeros((SUBLANE - N_CH, KW), F32)], axis=0)


def _gla_proj(x, mods, l, j, w_in, w_ga, w_gb, b_g):
    xs = _x_args(x)
    return pl.pallas_call(
        functools.partial(_gla_proj_kernel, n_x=len(xs)),
        grid=(NTILE,),
        in_specs=_x_specs(len(xs) == 2, TM) + [
            _mod_spec(l),
            _stacked((D, 2 * KW + 2 * D), j, True),
            _stacked((2, D, RANK), j),
            _stacked((2, RANK, KW), j),
            _stacked((2, 1, KW), j),
        ],
        out_specs=[
            pl.BlockSpec((2, TM, KW), lambda i: (0, i, 0)),
            pl.BlockSpec((2, TM, KW), lambda i: (0, i, 0)),
            pl.BlockSpec((2, KW, TM), lambda i: (0, 0, i)),
            pl.BlockSpec((TM, D), lambda i: (i, 0)),
            pl.BlockSpec((TM, D), lambda i: (i, 0)),
            pl.BlockSpec((2, None, SUBLANE, KW), lambda i: (0, i, 0, 0)),
        ],
        out_shape=[
            jax.ShapeDtypeStruct((2, NTOK, KW), BF16),
            jax.ShapeDtypeStruct((2, NTOK, KW), BF16),
            jax.ShapeDtypeStruct((2, KW, NTOK), BF16),
            jax.ShapeDtypeStruct((NTOK, D), BF16),
            jax.ShapeDtypeStruct((NTOK, D), F32),
            jax.ShapeDtypeStruct((2, NTILE, SUBLANE, KW), F32),
        ],
        scratch_shapes=[pltpu.VMEM((D, 2 * KW + 2 * D), BF16)],
        compiler_params=_cparams("arbitrary"),
        name="gla_proj",
    )(*xs, mods, w_in, w_ga, w_gb, b_g.reshape(b_g.shape[0], 2, 1, KW))


def _scan_tile(i, rev):
    return NTILE - 1 - i if rev else i


def _scan_seq(t):
    return jnp.where(t < CTX_TILES, t, CTX_TILES + (t - CTX_TILES) // TILES_PER_LAT)


def _gla_scan_kernel(qd_ref, kd_ref, ket_ref, v_ref, dec_ref, s0_ref, o_ref, st_ref, s_scr, *, rev):
    t = _scan_tile(pl.program_id(0), rev)
    is_ctx = t < CTX_TILES
    r = (t - CTX_TILES) % TILES_PER_LAT
    first_r, last_r = (TILES_PER_LAT - 1, 0) if rev else (0, TILES_PER_LAT - 1)

    @pl.when(is_ctx)
    def _():
        s_scr[...] = jnp.zeros_like(s_scr)

    @pl.when(jnp.logical_and(jnp.logical_not(is_ctx), r == first_r))
    def _():
        s_scr[...] = s0_ref[...]

    row = lax.broadcasted_iota(jnp.int32, (TM, TM), 0)
    col = lax.broadcasted_iota(jnp.int32, (TM, TM), 1)
    same = (row // CHUNK) == (col // CHUNK)
    mask = same & ((col >= row) if rev else (col <= row))
    kcol = lax.broadcasted_iota(jnp.int32, (DK, TM), 1) // CHUNK
    chunks = range(N_CH - 1, -1, -1) if rev else range(N_CH)

    for hd in range(H):
        qh = qd_ref[:, hd * DK:(hd + 1) * DK]
        kh = kd_ref[:, hd * DK:(hd + 1) * DK]
        vh = v_ref[:, hd * DV:(hd + 1) * DV]
        keth = ket_ref[hd * DK:(hd + 1) * DK, :]
        sc = lax.dot_general(qh, kh, (((1,), (1,)), ((), ())), preferred_element_type=F32)
        p = jnp.where(mask, sc, 0.0).astype(BF16)
        o_intra = jnp.dot(p, vh, preferred_element_type=F32)
        s = s_scr[hd]
        for c in chunks:
            rows = slice(c * CHUNK, (c + 1) * CHUNK)
            o_inter = jnp.dot(qh[rows, :], s.astype(BF16), preferred_element_type=F32)
            o_ref[rows, hd * DV:(hd + 1) * DV] = o_intra[rows, :] + o_inter
            kv = jnp.dot(jnp.where(kcol == c, keth, jnp.zeros_like(keth)), vh,
                         preferred_element_type=F32)
            dec_row = dec_ref[c:c + 1, hd * DK:(hd + 1) * DK]
            dec_col = jnp.broadcast_to(dec_row, (DK, DK)).T
            s = s * jnp.concatenate([dec_col, dec_col], axis=1) + kv
        s_scr[hd] = s

    @pl.when(jnp.logical_or(is_ctx, r == last_r))
    def _():
        st_ref[...] = s_scr[...]


def _gla_scan(qd, kd, ket, v, dec, state_gla, j, rev):
    d = 1 if rev else 0
    tile = lambda i: _scan_tile(i, rev)
    lat = lambda i: jnp.clip((tile(i) - CTX_TILES) // TILES_PER_LAT, 0, DEC_BATCH - 1)
    return pl.pallas_call(
        functools.partial(_gla_scan_kernel, rev=rev),
        grid=(NTILE,),
        in_specs=[
            pl.BlockSpec((None, TM, KW), lambda i: (d, tile(i), 0)),
            pl.BlockSpec((None, TM, KW), lambda i: (d, tile(i), 0)),
            pl.BlockSpec((None, KW, TM), lambda i: (d, 0, tile(i))),
            pl.BlockSpec((TM, D), lambda i: (tile(i), 0)),
            pl.BlockSpec((None, None, SUBLANE, KW), lambda i: (d, tile(i), 0, 0)),
            pl.BlockSpec((None, None, None, H, DK, DV), lambda i: (lat(i), j, d, 0, 0, 0)),
        ],
        out_specs=[
            pl.BlockSpec((TM, D), lambda i: (tile(i), 0)),
            pl.BlockSpec((None, H, DK, DV), lambda i: (_scan_seq(tile(i)), 0, 0, 0)),
        ],
        out_shape=[
            jax.ShapeDtypeStruct((NTOK, D), F32),
            jax.ShapeDtypeStruct((N_SEQ, H, DK, DV), F32),
        ],
        scratch_shapes=[pltpu.VMEM((H, DK, DV), F32)],
        compiler_params=_cparams("arbitrary"),
        name="gla_scan_bwd" if rev else "gla_scan_fwd",
    )(qd, kd, ket, v, dec, state_gla)


def _gla_post_kernel(*refs, n_x):
    x_refs = refs[:n_x]
    of_ref, ob_ref, r_ref, mod_ref, gn_ref, wo_ref, g_ref, b_ref, o_ref, wos = refs[n_x:]
    i = pl.program_id(0)
    (gate,) = _mod_rows(mod_ref, i * TM, (2,))

    @pl.when(i == 0)
    def _():
        _cast_rows(wo_ref, wos, 128)

    o = of_ref[...] + ob_ref[...]
    parts = []
    for hd in range(H):
        oh = o[:, hd * DV:(hd + 1) * DV]
        ms = jnp.mean(oh * oh, axis=-1, keepdims=True)
        parts.append(oh * lax.rsqrt(ms + RMS_EPS))
    on = jnp.concatenate(parts, axis=1) * gn_ref[...]
    y = jnp.dot((on * _silu(r_ref[...])).astype(BF16), wos[...], preferred_element_type=F32)
    o_ref[...] = _layer_norm(ALPHA * _read_x(x_refs, TM) + gate * y, g_ref[...], b_ref[...])


def _gla_post(x, o_f, o_b, r, mods, l, j, gn_g, w_o, ln_g, ln_b):
    xs = _x_args(x)
    tile = pl.BlockSpec((TM, D), lambda i: (i, 0))
    return pl.pallas_call(
        functools.partial(_gla_post_kernel, n_x=len(xs)),
        grid=(NTILE,),
        in_specs=_x_specs(len(xs) == 2, TM) + [
            tile, tile, tile, _mod_spec(l),
            _stacked((1, D), j), _stacked((D, D), j, True),
            _stacked((1, D), 2 * l), _stacked((1, D), 2 * l)],
        out_specs=tile,
        out_shape=jax.ShapeDtypeStruct((NTOK, D), F32),
        scratch_shapes=[pltpu.VMEM((D, D), BF16)],
        compiler_params=_cparams("arbitrary"),
        name="gla_post",
    )(*xs, o_f, o_b, r, mods, _vec3(gn_g), w_o, _vec3(ln_g), _vec3(ln_b))


def kernel(x_prompt, x_sample, c, state_gla, c_ctx, mod_w, mod_b, ln_g, ln_b, ff_w1, ff_w2, gla_w_in, gla_w_ga, gla_w_gb, gla_b_g, gla_gn_g, gla_w_o, conf_w_pw1, conf_b_pw1, conf_w_dw, conf_b_dw, conf_ln_g, conf_ln_b, conf_w_pw2, conf_b_pw2, sc_w_in, sc_w_conv, sc_w_out):
    assert x_prompt.shape == (BATCH, SEQ, D) and x_sample.shape == (DEC_BATCH, DEC_SEQ, D)
    x = (x_prompt.reshape(NCTX, D), x_sample.reshape(NLAT, D))

    c8 = jnp.concatenate([c_ctx[None, :], c, jnp.zeros((SUBLANE - N_CVEC, D), F32)], axis=0)
    mods = _adaln(c8, mod_w, mod_b)

    states = []
    for l in range(DEPTH):
        kind, j = l % 3, l // 3
        if kind == 0:
            qd, kd, ket, v, r, dec = _gla_proj(x, mods, l, j, gla_w_in, gla_w_ga, gla_w_gb, gla_b_g)
            o_f, st_f = _gla_scan(qd, kd, ket, v, dec, state_gla, j, rev=False)
            o_b, st_b = _gla_scan(qd, kd, ket, v, dec, state_gla, j, rev=True)
            states.append(jnp.stack([st_f[:BATCH], st_b[:BATCH]], axis=1))
            x = _gla_post(x, o_f, o_b, r, mods, l, j, gla_gn_g, gla_w_o, ln_g, ln_b)
        elif kind == 1:
            x = _conformer(x, mods, l, j, conf_w_pw1, conf_b_pw1, conf_w_dw, conf_b_dw, conf_ln_g,
                           conf_ln_b, conf_w_pw2, conf_b_pw2, ln_g, ln_b)
        else:
            x = _short_conv(x, mods, l, j, sc_w_in, sc_w_conv, sc_w_out, ln_g, ln_b)
        x = _mlp(x, mods, l, ff_w1, ff_w2, ln_g, ln_b, split_out=(l == DEPTH - 1))

    y_prompt, y_sample = x
    return (y_prompt.reshape(BATCH, SEQ, D), y_sample.reshape(DEC_BATCH, DEC_SEQ, D),
            jnp.stack(states, axis=1))
```

```python
import functools

import jax
import jax.numpy as jnp
from jax import lax
from jax.experimental import pallas as pl
from jax.experimental.pallas import tpu as pltpu

F32 = jnp.float32
BF16 = jnp.bfloat16

D = 1024
DEPTH = 4
BATCH, SEQ = 16, 256
DEC_BATCH, DEC_SEQ = 2, 2048
GRID_W = 64
N_MOD = 6
N_CVEC = 1 + DEC_BATCH
NCTX = BATCH * SEQ
NLAT = DEC_BATCH * DEC_SEQ
NTOK = NCTX + NLAT
H, DK, DV = 4, 128, 256
KW = H * DK
RANK = 16
CHUNK = 64
GATE_NORM = 16.0
CONF_W = 31
D_FF = 4 * D
LN_EPS = 1e-5
RMS_EPS = 1e-6
ALPHA = (2 * DEPTH) ** 0.25

TM = 256
NTILE = NTOK // TM
CTX_TILES = NCTX // TM
TILES_PER_LAT = DEC_SEQ // TM
N_SEQ = BATCH + DEC_BATCH
TM_MLP = 512
TF_MLP = 1024
NJ_MLP = D_FF // TF_MLP
TN_MOD = 2048
SUBLANE = 8
VMEM_LIMIT = 58 * 1024 * 1024


def _cparams(*sem):
    return pltpu.CompilerParams(dimension_semantics=sem, vmem_limit_bytes=VMEM_LIMIT)


def _mod_rows(mod_ref, row0, ks):
    m = jnp.where(row0 < NCTX, 0, 1 + (row0 - NCTX) // DEC_SEQ)
    return [mod_ref[pl.ds(m, 1), k * D:(k + 1) * D] for k in ks]


def _mod_spec(l):
    return pl.BlockSpec((None, SUBLANE, N_MOD * D), lambda *_: (l, 0, 0))


def _stacked(block, j, single_buffer=False):
    nd = len(block)
    mode = dict(pipeline_mode=pl.Buffered(1)) if single_buffer else {}
    return pl.BlockSpec((None,) + block, lambda *_: (j,) + (0,) * nd, **mode)


def _x_specs(split, tile=lambda i: i):
    if not split:
        return [pl.BlockSpec((TM, D), lambda i: (tile(i), 0))]
    return [pl.BlockSpec((TM, D), lambda i: (jnp.clip(tile(i), 0, CTX_TILES - 1), 0)),
            pl.BlockSpec((TM, D), lambda i: (jnp.clip(tile(i) - CTX_TILES, 0, NTILE - CTX_TILES - 1), 0))]


def _x_args(x):
    return list(x) if isinstance(x, tuple) else [x]


def _read_x(x_refs, t):
    if len(x_refs) == 1:
        return x_refs[0][...]
    return jnp.where(t < CTX_TILES, x_refs[0][...], x_refs[1][...])


def _layer_norm(y, g, b):
    mu = jnp.mean(y, axis=-1, keepdims=True)
    yc = y - mu
    var = jnp.mean(yc * yc, axis=-1, keepdims=True)
    return yc * lax.rsqrt(var + LN_EPS) * g + b


def _silu(x):
    return x * jax.nn.sigmoid(x)


def _cast_rows(src_ref, dst_ref, step):
    n = src_ref.shape[0] // step

    def body(r, c):
        rows = pl.ds(pl.multiple_of(r * step, step), step)
        dst_ref[rows, :] = src_ref[rows, :].astype(BF16)
        return c

    lax.fori_loop(0, n, body, 0)


def _vec3(a):
    return a.reshape(-1, 1, a.shape[-1])


def _adaln_kernel(c_ref, w_ref, b_ref, o_ref):
    s = _silu(c_ref[...]).astype(BF16)
    o_ref[...] = jnp.dot(s, w_ref[...].astype(BF16), preferred_element_type=F32) + b_ref[...]


def _adaln(c8, mod_w, mod_b):
    return pl.pallas_call(
        _adaln_kernel,
        grid=(DEPTH, N_MOD * D // TN_MOD),
        in_specs=[
            pl.BlockSpec((SUBLANE, D), lambda l, n: (0, 0)),
            pl.BlockSpec((None, D, TN_MOD), lambda l, n: (l, 0, n)),
            pl.BlockSpec((None, 1, TN_MOD), lambda l, n: (l, 0, n)),
        ],
        out_specs=pl.BlockSpec((None, SUBLANE, TN_MOD), lambda l, n: (l, 0, n)),
        out_shape=jax.ShapeDtypeStruct((DEPTH, SUBLANE, N_MOD * D), F32),
        compiler_params=_cparams("parallel", "parallel"),
        name="adaln",
    )(c8, mod_w, _vec3(mod_b))


def _mlp_kernel(x_ref, mod_ref, w1_ref, w2_ref, g_ref, b_ref, *rest, split_out):
    o_refs, (w1s, w2s, h_scr, acc_scr) = rest[:-4], rest[-4:]
    i, j = pl.program_id(0), pl.program_id(1)
    shift, scale, gate = _mod_rows(mod_ref, i * TM_MLP, (3, 4, 5))

    @pl.when(i == 0)
    def _():
        w1s[j] = w1_ref[...].astype(BF16)
        w2s[j] = w2_ref[...].astype(BF16)

    @pl.when(j == 0)
    def _():
        h_scr[...] = (x_ref[...] * (1.0 + scale) + shift).astype(BF16)
        acc_scr[...] = jnp.zeros_like(acc_scr)

    a = jnp.dot(h_scr[...], w1s[j], preferred_element_type=F32)
    a = jnp.square(jnp.maximum(a, 0.0)).astype(BF16)
    acc_scr[...] += jnp.dot(a, w2s[j], preferred_element_type=F32)

    @pl.when(j == NJ_MLP - 1)
    def _():
        y = _layer_norm(ALPHA * x_ref[...] + gate * acc_scr[...], g_ref[...], b_ref[...])
        if not split_out:
            o_refs[0][...] = y
        else:
            @pl.when(i < NCTX // TM_MLP)
            def _():
                o_refs[0][...] = y

            @pl.when(i >= NCTX // TM_MLP)
            def _():
                o_refs[1][...] = y


def _mlp(x, mods, l, w1, w2, ln_g, ln_b, split_out):
    nc = NCTX // TM_MLP
    chunk = lambda i, j: jnp.where(i == 0, j, NJ_MLP - 1)
    if split_out:
        out_specs = [pl.BlockSpec((TM_MLP, D), lambda i, j: (jnp.minimum(i, nc - 1), 0)),
                     pl.BlockSpec((TM_MLP, D), lambda i, j: (jnp.maximum(i - nc, 0), 0))]
        out_shape = [jax.ShapeDtypeStruct((NCTX, D), F32), jax.ShapeDtypeStruct((NLAT, D), F32)]
    else:
        out_specs = pl.BlockSpec((TM_MLP, D), lambda i, j: (i, 0))
        out_shape = jax.ShapeDtypeStruct((NTOK, D), F32)
    return pl.pallas_call(
        functools.partial(_mlp_kernel, split_out=split_out),
        grid=(NTOK // TM_MLP, NJ_MLP),
        in_specs=[
            pl.BlockSpec((TM_MLP, D), lambda i, j: (i, 0)),
            _mod_spec(l),
            pl.BlockSpec((None, D, TF_MLP), lambda i, j: (l, 0, chunk(i, j))),
            pl.BlockSpec((None, TF_MLP, D), lambda i, j: (l, chunk(i, j), 0)),
            _stacked((1, D), 2 * l + 1), _stacked((1, D), 2 * l + 1),
        ],
        out_specs=out_specs,
        out_shape=out_shape,
        scratch_shapes=[pltpu.VMEM((NJ_MLP, D, TF_MLP), BF16), pltpu.VMEM((NJ_MLP, TF_MLP, D), BF16),
                        pltpu.VMEM((TM_MLP, D), BF16), pltpu.VMEM((TM_MLP, D), F32)],
        compiler_params=_cparams("arbitrary", "arbitrary"),
        name="mlp",
    )(x, mods, w1, w2, _vec3(ln_g), _vec3(ln_b))


CONF_PAD = 16
CONF_RB = 64
CONF_LB = 256
CONF_ROWS = (TM // GRID_W) * (GRID_W + 2 * CONF_PAD)


def _conf_conv(upad_ref, shf_ref, wdw_ref, cbuf_ref, u, seg):
    nseg = TM // seg
    stride = seg + 2 * CONF_PAD
    used = nseg * stride
    zpad = jnp.zeros((CONF_PAD, D), F32)
    for s in range(nseg):
        base = s * stride
        upad_ref[base:base + CONF_PAD, :] = zpad
        upad_ref[base + CONF_PAD:base + CONF_PAD + seg, :] = u[s * seg:(s + 1) * seg, :]
        upad_ref[base + CONF_PAD + seg:base + stride, :] = zpad

    def lane_block(lb, c):
        lanes = pl.ds(pl.multiple_of(lb * CONF_LB, CONF_LB), CONF_LB)
        for b in range(1, SUBLANE):
            shf_ref[b - 1, 0:used - SUBLANE, :] = upad_ref[b:b + used - SUBLANE, lanes]
        for s in range(nseg):
            for rb in range(seg // CONF_RB):
                r0 = s * stride + CONF_PAD + rb * CONF_RB
                acc = jnp.zeros((CONF_RB, CONF_LB), F32)
                for k in range(CONF_W):
                    a, b = divmod(k - CONF_W // 2, SUBLANE)
                    rows = slice(r0 + SUBLANE * a, r0 + SUBLANE * a + CONF_RB)
                    src = upad_ref[rows, lanes] if b == 0 else shf_ref[b - 1, rows, :]
                    acc = acc + wdw_ref[k:k + 1, lanes] * src
                o0 = s * seg + rb * CONF_RB
                cbuf_ref[o0:o0 + CONF_RB, lanes] = acc
        return c

    lax.fori_loop(0, D // CONF_LB, lane_block, 0)


def _conf_kernel(x_ref, mod_ref, w1_ref, b1_ref, wdw_ref, bdw_ref, cg_ref, cb_ref, w2_ref, b2_ref,
                 g_ref, b_ref, o_ref, w1s, w2s, upad, shf, cbuf):
    i = pl.program_id(0)
    shift, scale, gate = _mod_rows(mod_ref, i * TM, (0, 1, 2))

    @pl.when(i == 0)
    def _():
        _cast_rows(w1_ref, w1s, 128)
        _cast_rows(w2_ref, w2s, 128)

    x = x_ref[...]
    h = (x * (1.0 + scale) + shift).astype(BF16)
    ag = jnp.dot(h, w1s[...], preferred_element_type=F32) + b1_ref[...]
    u = ag[:, :D] * jax.nn.sigmoid(ag[:, D:])

    @pl.when(i < CTX_TILES)
    def _():
        _conf_conv(upad, shf, wdw_ref, cbuf, u, SEQ)

    @pl.when(i >= CTX_TILES)
    def _():
        _conf_conv(upad, shf, wdw_ref, cbuf, u, GRID_W)

    uc = _silu(_layer_norm(cbuf[...] + bdw_ref[...], cg_ref[...], cb_ref[...]))
    y = jnp.dot(uc.astype(BF16), w2s[...], preferred_element_type=F32) + b2_ref[...]
    o_ref[...] = _layer_norm(ALPHA * x + gate * y, g_ref[...], b_ref[...])


def _conformer(x, mods, l, j, w1, b1, wdw, bdw, cg, cb, w2, b2, ln_g, ln_b):
    return pl.pallas_call(
        _conf_kernel,
        grid=(NTILE,),
        in_specs=[
            pl.BlockSpec((TM, D), lambda i: (i, 0)),
            _mod_spec(l),
            _stacked((D, 2 * D), j, True), _stacked((1, 2 * D), j),
            _stacked((CONF_W, D), j), _stacked((1, D), j), _stacked((1, D), j), _stacked((1, D), j),
            _stacked((D, D), j, True), _stacked((1, D), j),
            _stacked((1, D), 2 * l), _stacked((1, D), 2 * l),
        ],
        out_specs=pl.BlockSpec((TM, D), lambda i: (i, 0)),
        out_shape=jax.ShapeDtypeStruct((NTOK, D), F32),
        scratch_shapes=[pltpu.VMEM((D, 2 * D), BF16), pltpu.VMEM((D, D), BF16),
                        pltpu.VMEM((CONF_ROWS, D), F32),
                        pltpu.VMEM((SUBLANE - 1, CONF_ROWS, CONF_LB), F32),
                        pltpu.VMEM((TM, D), F32)],
        compiler_params=_cparams("arbitrary"),
        name="conformer",
    )(x, mods, w1, _vec3(b1), wdw, _vec3(bdw), _vec3(cg), _vec3(cb), w2, _vec3(b2),
      _vec3(ln_g), _vec3(ln_b))


def _sconv_kernel(x_ref, xp_ref, xn_ref, mod_ref, win_ref, wc_ref, wout_ref, g_ref, b_ref, o_ref,
                  wins, wouts, cpad, ybuf):
    i = pl.program_id(0)
    shift, scale, gate = _mod_rows(mod_ref, i * TM, (0, 1, 2))

    @pl.when(i == 0)
    def _():
        _cast_rows(win_ref, wins, 128)
        _cast_rows(wout_ref, wouts, 128)

    scale = 1.0 + scale
    x = x_ref[...]
    h = (x * scale + shift).astype(BF16)
    bg = jnp.dot(h, wins[:, :D], preferred_element_type=F32)
    w0, w1, w2 = wc_ref[0:1, :], wc_ref[1:2, :], wc_ref[2:3, :]

    @pl.when(i < CTX_TILES)
    def _():
        cu = jnp.dot(h, wins[:, D:], preferred_element_type=F32)
        cu = cu[:, :D] * cu[:, D:]
        zrow = jnp.zeros((SUBLANE, D), F32)
        cpad[0:SUBLANE, :] = zrow
        cpad[SUBLANE:SUBLANE + TM, :] = cu
        cpad[SUBLANE + TM:2 * SUBLANE + TM, :] = zrow
        ybuf[...] = (w0 * cpad[SUBLANE - 1:SUBLANE - 1 + TM, :] + w1 * cu
                     + w2 * cpad[SUBLANE + 1:SUBLANE + 1 + TM, :])

    @pl.when(i >= CTX_TILES)
    def _():
        r = (i - CTX_TILES) % TILES_PER_LAT
        hp = (xp_ref[...] * scale + shift).astype(BF16)
        hn = (xn_ref[...] * scale + shift).astype(BF16)
        hcat = jnp.concatenate([hp, h, hn], axis=0)
        cu = jnp.dot(hcat, wins[:, D:], preferred_element_type=F32)
        cu = cu[:, :D] * cu[:, D:]
        halo_up = jnp.where(r > 0, cu[0:GRID_W, :], 0.0)
        halo_dn = jnp.where(r < TILES_PER_LAT - 1, cu[GRID_W + TM:, :], 0.0)
        up = jnp.concatenate([halo_up, cu[GRID_W:TM, :]], axis=0)
        dn = jnp.concatenate([cu[2 * GRID_W:GRID_W + TM, :], halo_dn], axis=0)
        ybuf[...] = w0 * up + w1 * cu[GRID_W:GRID_W + TM, :] + w2 * dn

    y = jnp.dot((bg * ybuf[...]).astype(BF16), wouts[...], preferred_element_type=F32)
    o_ref[...] = _layer_norm(ALPHA * x + gate * y, g_ref[...], b_ref[...])


def _short_conv(x, mods, l, j, w_in, w_conv, w_out, ln_g, ln_b):
    halo_per_tile = TM // GRID_W
    n_halo = NTOK // GRID_W
    return pl.pallas_call(
        _sconv_kernel,
        grid=(NTILE,),
        in_specs=[
            pl.BlockSpec((TM, D), lambda i: (i, 0)),
            pl.BlockSpec((GRID_W, D), lambda i: (jnp.maximum(i * halo_per_tile - 1, 0), 0)),
            pl.BlockSpec((GRID_W, D), lambda i: (jnp.minimum((i + 1) * halo_per_tile, n_halo - 1), 0)),
            _mod_spec(l),
            _stacked((D, 3 * D), j, True),
            _stacked((3, D), j),
            _stacked((D, D), j, True),
            _stacked((1, D), 2 * l), _stacked((1, D), 2 * l),
        ],
        out_specs=pl.BlockSpec((TM, D), lambda i: (i, 0)),
        out_shape=jax.ShapeDtypeStruct((NTOK, D), F32),
        scratch_shapes=[pltpu.VMEM((D, 3 * D), BF16), pltpu.VMEM((D, D), BF16),
                        pltpu.VMEM((TM + 2 * SUBLANE, D), F32), pltpu.VMEM((TM, D), F32)],
        compiler_params=_cparams("arbitrary"),
        name="short_conv",
    )(x, x, x, mods, w_in, w_conv, w_out, _vec3(ln_g), _vec3(ln_b))


N_CH = TM // CHUNK


def _scan_seq(t):
    return jnp.where(t < CTX_TILES, t, CTX_TILES + (t - CTX_TILES) // TILES_PER_LAT)


def _lat_seq(t):
    return jnp.clip((t - CTX_TILES) // TILES_PER_LAT, 0, DEC_BATCH - 1)


def _chunk_tri(rev):
    row = lax.broadcasted_iota(jnp.int32, (TM, TM), 0)
    col = lax.broadcasted_iota(jnp.int32, (TM, TM), 1)
    same = (row // CHUNK) == (col // CHUNK)
    return same & ((col >= row) if rev else (col <= row))


def _scan_state_init(s_scr, s0_ref, t, rev):
    first_r = TILES_PER_LAT - 1 if rev else 0

    @pl.when(t < CTX_TILES)
    def _():
        s_scr[...] = jnp.zeros_like(s_scr)

    @pl.when(jnp.logical_and(t >= CTX_TILES, (t - CTX_TILES) % TILES_PER_LAT == first_r))
    def _():
        s_scr[...] = s0_ref[...]


def _scan_state_emit(s_scr, st_ref, t, rev):
    last_r = 0 if rev else TILES_PER_LAT - 1

    @pl.when(jnp.logical_or(t < CTX_TILES, (t - CTX_TILES) % TILES_PER_LAT == last_r))
    def _():
        st_ref[...] = s_scr[...]


def _decay_operands(q, k, bcum, rev):
    qd, kd, ke, dec = [], [], [], []
    for c in range(N_CH):
        rows = slice(c * CHUNK, (c + 1) * CHUNK)
        b = bcum[rows, :]
        last = b[0:1, :] if rev else b[CHUNK - 1:CHUNK, :]
        qd.append((q[rows, :] * jnp.exp(b)).astype(BF16))
        kd.append((k[rows, :] * jnp.exp(-b)).astype(BF16))
        ke.append(k[rows, :] * jnp.exp(last - b))
        dec.append(jnp.exp(last))
    return (jnp.concatenate(qd, axis=0), jnp.concatenate(kd, axis=0),
            jnp.concatenate(ke, axis=0).T.astype(BF16), dec)


def _scan_head(qh, kh, keth, vh, dec_rows, s, mask, kcol, rev):
    sc = lax.dot_general(qh, kh, (((1,), (1,)), ((), ())), preferred_element_type=F32)
    p = jnp.where(mask, sc, 0.0).astype(BF16)
    o_intra = jnp.dot(p, vh, preferred_element_type=F32)
    outs = [None] * N_CH
    for c in (range(N_CH - 1, -1, -1) if rev else range(N_CH)):
        rows = slice(c * CHUNK, (c + 1) * CHUNK)
        outs[c] = o_intra[rows, :] + jnp.dot(qh[rows, :], s.astype(BF16), preferred_element_type=F32)
        kv = jnp.dot(jnp.where(kcol == c, keth, jnp.zeros_like(keth)), vh, preferred_element_type=F32)
        dec_col = jnp.broadcast_to(dec_rows[c], (DK, DK)).T
        s = s * jnp.concatenate([dec_col, dec_col], axis=1) + kv
    return outs, s


def _gla_fwd_kernel(*refs, n_x):
    x_refs = refs[:n_x]
    (mod_ref, win_ref, wga_ref, wgb_ref, bg_ref, s0_ref,
     of_ref, qd_ref, kd_ref, ket_ref, v_ref, r_ref, dec_ref, st_ref,
     wins, tri_scr, s_scr) = refs[n_x:]
    i = pl.program_id(0)
    shift, scale = _mod_rows(mod_ref, i * TM, (0, 1))

    @pl.when(i == 0)
    def _():
        _cast_rows(win_ref, wins, 128)
        tri_scr[0] = jnp.where(_chunk_tri(False), 1.0, 0.0).astype(BF16)
        tri_scr[1] = jnp.where(_chunk_tri(True), 1.0, 0.0).astype(BF16)

    _scan_state_init(s_scr, s0_ref, i, False)

    h = (_read_x(x_refs, i) * (1.0 + scale) + shift).astype(BF16)
    proj = jnp.dot(h, wins[...], preferred_element_type=F32)
    q = proj[:, :KW] * (DK ** -0.5)
    k = proj[:, KW:2 * KW]
    v = proj[:, 2 * KW:2 * KW + D].astype(BF16)
    v_ref[...] = v
    r_ref[...] = proj[:, 2 * KW + D:]

    bcum = []
    for d in range(2):
        za = jnp.dot(h, wga_ref[d].astype(BF16), preferred_element_type=F32)
        z = jnp.dot(za.astype(BF16), wgb_ref[d].astype(BF16), preferred_element_type=F32) + bg_ref[d]
        g = (jnp.minimum(z, 0.0) - jnp.log1p(jnp.exp(-jnp.abs(z)))) * (1.0 / GATE_NORM)
        g_hi = g.astype(BF16)
        g_lo = (g - g_hi.astype(F32)).astype(BF16)
        bcum.append(jnp.dot(tri_scr[d], g_hi, preferred_element_type=F32)
                    + jnp.dot(tri_scr[d], g_lo, preferred_element_type=F32))

    qd_b, kd_b, ket_b, dec_b = _decay_operands(q, k, bcum[1], True)
    qd_ref[...] = qd_b
    kd_ref[...] = kd_b
    ket_ref[...] = ket_b
    dec_ref[...] = jnp.concatenate(dec_b + [jnp.zeros((SUBLANE - N_CH, KW), F32)], axis=0)

    qd, kd, ket, dec = _decay_operands(q, k, bcum[0], False)
    mask = _chunk_tri(False)
    kcol = lax.broadcasted_iota(jnp.int32, (DK, TM), 1) // CHUNK
    for hd in range(H):
        ks, vs = slice(hd * DK, (hd + 1) * DK), slice(hd * DV, (hd + 1) * DV)
        outs, s = _scan_head(qd[:, ks], kd[:, ks], ket[ks, :], v[:, vs], [e[:, ks] for e in dec],
                             s_scr[hd], mask, kcol, False)
        for c in range(N_CH):
            of_ref[c * CHUNK:(c + 1) * CHUNK, vs] = outs[c]
        s_scr[hd] = s

    _scan_state_emit(s_scr, st_ref, i, False)


def _gla_fwd(x, mods, l, j, w_in, w_ga, w_gb, b_g, state_gla):
    xs = _x_args(x)
    tile = pl.BlockSpec((TM, D), lambda i: (i, 0))
    keys = pl.BlockSpec((TM, KW), lambda i: (i, 0))
    state = (H, DK, DV)
    return pl.pallas_call(
        functools.partial(_gla_fwd_kernel, n_x=len(xs)),
        grid=(NTILE,),
        in_specs=_x_specs(len(xs) == 2) + [
            _mod_spec(l),
            _stacked((D, 2 * KW + 2 * D), j, True),
            _stacked((2, D, RANK), j),
            _stacked((2, RANK, KW), j),
            _stacked((2, 1, KW), j),
            pl.BlockSpec((None, None, None) + state, lambda i: (_lat_seq(i), j, 0, 0, 0, 0)),
        ],
        out_specs=[
            tile, keys, keys,
            pl.BlockSpec((KW, TM), lambda i: (0, i)),
            tile, tile,
            pl.BlockSpec((None, SUBLANE, KW), lambda i: (i, 0, 0)),
            pl.BlockSpec((None,) + state, lambda i: (_scan_seq(i), 0, 0, 0)),
        ],
        out_shape=[
            jax.ShapeDtypeStruct((NTOK, D), F32),
            jax.ShapeDtypeStruct((NTOK, KW), BF16),
            jax.ShapeDtypeStruct((NTOK, KW), BF16),
            jax.ShapeDtypeStruct((KW, NTOK), BF16),
            jax.ShapeDtypeStruct((NTOK, D), BF16),
            jax.ShapeDtypeStruct((NTOK, D), F32),
            jax.ShapeDtypeStruct((NTILE, SUBLANE, KW), F32),
            jax.ShapeDtypeStruct((N_SEQ,) + state, F32),
        ],
        scratch_shapes=[pltpu.VMEM((D, 2 * KW + 2 * D), BF16), pltpu.VMEM((2, TM, TM), BF16),
                        pltpu.VMEM(state, F32)],
        compiler_params=_cparams("arbitrary"),
        name="gla_fwd",
    )(*xs, mods, w_in, w_ga, w_gb, b_g.reshape(b_g.shape[0], 2, 1, KW), state_gla)


def _gla_bwd_kernel(*refs, n_x):
    x_refs = refs[:n_x]
    (qd_ref, kd_ref, ket_ref, v_ref, dec_ref, s0_ref, of_ref, r_ref, mod_ref, gn_ref, wo_ref,
     g_ref, b_ref, o_ref, st_ref, wos, s_scr) = refs[n_x:]
    i = pl.program_id(0)
    t = NTILE - 1 - i
    (gate,) = _mod_rows(mod_ref, t * TM, (2,))

    @pl.when(i == 0)
    def _():
        _cast_rows(wo_ref, wos, 128)

    _scan_state_init(s_scr, s0_ref, t, True)

    mask = _chunk_tri(True)
    kcol = lax.broadcasted_iota(jnp.int32, (DK, TM), 1) // CHUNK
    parts = []
    for hd in range(H):
        ks, vs = slice(hd * DK, (hd + 1) * DK), slice(hd * DV, (hd + 1) * DV)
        outs, s = _scan_head(qd_ref[:, ks], kd_ref[:, ks], ket_ref[ks, :], v_ref[:, vs],
                             [dec_ref[c:c + 1, ks] for c in range(N_CH)], s_scr[hd], mask, kcol, True)
        s_scr[hd] = s
        oh = of_ref[:, vs] + jnp.concatenate(outs, axis=0)
        ms = jnp.mean(oh * oh, axis=-1, keepdims=True)
        parts.append(oh * lax.rsqrt(ms + RMS_EPS))

    _scan_state_emit(s_scr, st_ref, t, True)

    on = jnp.concatenate(parts, axis=1) * gn_ref[...]
    y = jnp.dot((on * _silu(r_ref[...])).astype(BF16), wos[...], preferred_element_type=F32)
    o_ref[...] = _layer_norm(ALPHA * _read_x(x_refs, t) + gate * y, g_ref[...], b_ref[...])


def _gla_bwd(x, o_f, qd, kd, ket, v, r, dec, state_gla, mods, l, j, gn_g, w_o, ln_g, ln_b):
    xs = _x_args(x)
    rtile = lambda i: NTILE - 1 - i
    tile = pl.BlockSpec((TM, D), lambda i: (rtile(i), 0))
    keys = pl.BlockSpec((TM, KW), lambda i: (rtile(i), 0))
    state = (H, DK, DV)
    return pl.pallas_call(
        functools.partial(_gla_bwd_kernel, n_x=len(xs)),
        grid=(NTILE,),
        in_specs=_x_specs(len(xs) == 2, rtile) + [
            keys, keys,
            pl.BlockSpec((KW, TM), lambda i: (0, rtile(i))),
            tile,
            pl.BlockSpec((None, SUBLANE, KW), lambda i: (rtile(i), 0, 0)),
            pl.BlockSpec((None, None, None) + state, lambda i: (_lat_seq(rtile(i)), j, 1, 0, 0, 0)),
            tile, tile, _mod_spec(l),
            _stacked((1, D), j), _stacked((D, D), j, True),
            _stacked((1, D), 2 * l), _stacked((1, D), 2 * l)],
        out_specs=[tile, pl.BlockSpec((None,) + state, lambda i: (_scan_seq(rtile(i)), 0, 0, 0))],
        out_shape=[jax.ShapeDtypeStruct((NTOK, D), F32), jax.ShapeDtypeStruct((N_SEQ,) + state, F32)],
        scratch_shapes=[pltpu.VMEM((D, D), BF16), pltpu.VMEM(state, F32)],
        compiler_params=_cparams("arbitrary"),
        name="gla_bwd",
    )(*xs, qd, kd, ket, v, dec, state_gla, o_f, r, mods, _vec3(gn_g), w_o, _vec3(ln_g), _vec3(ln_b))


def kernel(x_prompt, x_sample, c, state_gla, c_ctx, mod_w, mod_b, ln_g, ln_b, ff_w1, ff_w2, gla_w_in, gla_w_ga, gla_w_gb, gla_b_g, gla_gn_g, gla_w_o, conf_w_pw1, conf_b_pw1, conf_w_dw, conf_b_dw, conf_ln_g, conf_ln_b, conf_w_pw2, conf_b_pw2, sc_w_in, sc_w_conv, sc_w_out):
    assert x_prompt.shape == (BATCH, SEQ, D) and x_sample.shape == (DEC_BATCH, DEC_SEQ, D)
    x = (x_prompt.reshape(NCTX, D), x_sample.reshape(NLAT, D))

    c8 = jnp.concatenate([c_ctx[None, :], c, jnp.zeros((SUBLANE - N_CVEC, D), F32)], axis=0)
    mods = _adaln(c8, mod_w, mod_b)

    states = []
    for l in range(DEPTH):
        kind, j = l % 3, l // 3
        if kind == 0:
            o_f, qd, kd, ket, v, r, dec, st_f = _gla_fwd(x, mods, l, j, gla_w_in, gla_w_ga, gla_w_gb,
                                                          gla_b_g, state_gla)
            x, st_b = _gla_bwd(x, o_f, qd, kd, ket, v, r, dec, state_gla, mods, l, j, gla_gn_g, gla_w_o,
                               ln_g, ln_b)
            states.append(jnp.stack([st_f[:BATCH], st_b[:BATCH]], axis=1))
        elif kind == 1:
            x = _conformer(x, mods, l, j, conf_w_pw1, conf_b_pw1, conf_w_dw, conf_b_dw, conf_ln_g,
                           conf_ln_b, conf_w_pw2, conf_b_pw2, ln_g, ln_b)
        else:
            x = _short_conv(x, mods, l, j, sc_w_in, sc_w_conv, sc_w_out, ln_g, ln_b)
        x = _mlp(x, mods, l, ff_w1, ff_w2, ln_g, ln_b, split_out=(l == DEPTH - 1))

    y_prompt, y_sample = x
    return (y_prompt.reshape(BATCH, SEQ, D), y_sample.reshape(DEC_BATCH, DEC_SEQ, D),
            jnp.stack(states, axis=1))
```

```python
import functools

import jax
import jax.numpy as jnp
from jax import lax
from jax.experimental import pallas as pl
from jax.experimental.pallas import tpu as pltpu

F32 = jnp.float32
BF16 = jnp.bfloat16

D = 1024
DEPTH = 4
BATCH, SEQ = 16, 256
DEC_BATCH, DEC_SEQ = 2, 2048
GRID_W = 64
N_MOD = 6
N_CVEC = 1 + DEC_BATCH
NCTX = BATCH * SEQ
NLAT = DEC_BATCH * DEC_SEQ
NTOK = NCTX + NLAT
H, DK, DV = 4, 128, 256
KW = H * DK
RANK = 16
CHUNK = 64
GATE_NORM = 16.0
CONF_W = 31
D_FF = 4 * D
LN_EPS = 1e-5
RMS_EPS = 1e-6
ALPHA = (2 * DEPTH) ** 0.25

TM = 256
NTILE = NTOK // TM
CTX_TILES = NCTX // TM
TILES_PER_LAT = DEC_SEQ // TM
N_SEQ = BATCH + DEC_BATCH
TM_MLP = 512
TF_MLP = 1024
NJ_MLP = D_FF // TF_MLP
TN_MOD = 2048
SUBLANE = 8
VMEM_LIMIT = 58 * 1024 * 1024


def _cparams(*sem):
    return pltpu.CompilerParams(dimension_semantics=sem, vmem_limit_bytes=VMEM_LIMIT)


def _mod_rows(mod_ref, row0, ks):
    m = jnp.where(row0 < NCTX, 0, 1 + (row0 - NCTX) // DEC_SEQ)
    return [mod_ref[pl.ds(m, 1), k * D:(k + 1) * D] for k in ks]


def _mod_spec(l):
    return pl.BlockSpec((None, SUBLANE, N_MOD * D), lambda *_: (l, 0, 0))


def _stacked(block, j, single_buffer=False):
    nd = len(block)
    mode = dict(pipeline_mode=pl.Buffered(1)) if single_buffer else {}
    return pl.BlockSpec((None,) + block, lambda *_: (j,) + (0,) * nd, **mode)


def _x_specs(split, tile=lambda i: i):
    if not split:
        return [pl.BlockSpec((TM, D), lambda i: (tile(i), 0))]
    return [pl.BlockSpec((TM, D), lambda i: (jnp.clip(tile(i), 0, CTX_TILES - 1), 0)),
            pl.BlockSpec((TM, D), lambda i: (jnp.clip(tile(i) - CTX_TILES, 0, NTILE - CTX_TILES - 1), 0))]


def _x_args(x):
    return list(x) if isinstance(x, tuple) else [x]


def _read_x(x_refs, t):
    if len(x_refs) == 1:
        return x_refs[0][...]
    return jnp.where(t < CTX_TILES, x_refs[0][...], x_refs[1][...])


def _layer_norm(y, g, b):
    mu = jnp.mean(y, axis=-1, keepdims=True)
    yc = y - mu
    var = jnp.mean(yc * yc, axis=-1, keepdims=True)
    return yc * lax.rsqrt(var + LN_EPS) * g + b


def _silu(x):
    return x * jax.nn.sigmoid(x)


def _cast_rows(src_ref, dst_ref, step):
    n = src_ref.shape[0] // step

    def body(r, c):
        rows = pl.ds(pl.multiple_of(r * step, step), step)
        dst_ref[rows, :] = src_ref[rows, :].astype(BF16)
        return c

    lax.fori_loop(0, n, body, 0)


def _vec3(a):
    return a.reshape(-1, 1, a.shape[-1])


def _adaln_kernel(c_ref, w_ref, b_ref, o_ref):
    s = _silu(c_ref[...]).astype(BF16)
    o_ref[...] = jnp.dot(s, w_ref[...].astype(BF16), preferred_element_type=F32) + b_ref[...]


def _adaln(c8, mod_w, mod_b):
    return pl.pallas_call(
        _adaln_kernel,
        grid=(DEPTH, N_MOD * D // TN_MOD),
        in_specs=[
            pl.BlockSpec((SUBLANE, D), lambda l, n: (0, 0)),
            pl.BlockSpec((None, D, TN_MOD), lambda l, n: (l, 0, n)),
            pl.BlockSpec((None, 1, TN_MOD), lambda l, n: (l, 0, n)),
        ],
        out_specs=pl.BlockSpec((None, SUBLANE, TN_MOD), lambda l, n: (l, 0, n)),
        out_shape=jax.ShapeDtypeStruct((DEPTH, SUBLANE, N_MOD * D), F32),
        compiler_params=_cparams("parallel", "parallel"),
        name="adaln",
    )(c8, mod_w, _vec3(mod_b))


def _mlp_kernel(x_ref, mod_ref, w1_ref, w2_ref, g_ref, b_ref, *rest, split_out):
    o_refs, (w1s, w2s, h_scr, acc_scr) = rest[:-4], rest[-4:]
    i, j = pl.program_id(0), pl.program_id(1)
    shift, scale, gate = _mod_rows(mod_ref, i * TM_MLP, (3, 4, 5))

    @pl.when(i == 0)
    def _():
        w1s[j] = w1_ref[...].astype(BF16)
        w2s[j] = w2_ref[...].astype(BF16)

    @pl.when(j == 0)
    def _():
        h_scr[...] = (x_ref[...] * (1.0 + scale) + shift).astype(BF16)
        acc_scr[...] = jnp.zeros_like(acc_scr)

    a = jnp.dot(h_scr[...], w1s[j], preferred_element_type=F32)
    a = jnp.square(jnp.maximum(a, 0.0)).astype(BF16)
    acc_scr[...] += jnp.dot(a, w2s[j], preferred_element_type=F32)

    @pl.when(j == NJ_MLP - 1)
    def _():
        y = _layer_norm(ALPHA * x_ref[...] + gate * acc_scr[...], g_ref[...], b_ref[...])
        if not split_out:
            o_refs[0][...] = y
        else:
            @pl.when(i < NCTX // TM_MLP)
            def _():
                o_refs[0][...] = y

            @pl.when(i >= NCTX // TM_MLP)
            def _():
                o_refs[1][...] = y


def _mlp(x, mods, l, w1, w2, ln_g, ln_b, split_out):
    nc = NCTX // TM_MLP
    chunk = lambda i, j: jnp.where(i == 0, j, NJ_MLP - 1)
    if split_out:
        out_specs = [pl.BlockSpec((TM_MLP, D), lambda i, j: (jnp.minimum(i, nc - 1), 0)),
                     pl.BlockSpec((TM_MLP, D), lambda i, j: (jnp.maximum(i - nc, 0), 0))]
        out_shape = [jax.ShapeDtypeStruct((NCTX, D), F32), jax.ShapeDtypeStruct((NLAT, D), F32)]
    else:
        out_specs = pl.BlockSpec((TM_MLP, D), lambda i, j: (i, 0))
        out_shape = jax.ShapeDtypeStruct((NTOK, D), F32)
    return pl.pallas_call(
        functools.partial(_mlp_kernel, split_out=split_out),
        grid=(NTOK // TM_MLP, NJ_MLP),
        in_specs=[
            pl.BlockSpec((TM_MLP, D), lambda i, j: (i, 0)),
            _mod_spec(l),
            pl.BlockSpec((None, D, TF_MLP), lambda i, j: (l, 0, chunk(i, j))),
            pl.BlockSpec((None, TF_MLP, D), lambda i, j: (l, chunk(i, j), 0)),
            _stacked((1, D), 2 * l + 1), _stacked((1, D), 2 * l + 1),
        ],
        out_specs=out_specs,
        out_shape=out_shape,
        scratch_shapes=[pltpu.VMEM((NJ_MLP, D, TF_MLP), BF16), pltpu.VMEM((NJ_MLP, TF_MLP, D), BF16),
                        pltpu.VMEM((TM_MLP, D), BF16), pltpu.VMEM((TM_MLP, D), F32)],
        compiler_params=_cparams("arbitrary", "arbitrary"),
        name="mlp",
    )(x, mods, w1, w2, _vec3(ln_g), _vec3(ln_b))


CONF_PAD = 16
CONF_RB = 64
CONF_LB = 256
CONF_ROWS = (TM // GRID_W) * (GRID_W + 2 * CONF_PAD)


def _conf_conv(upad_ref, shf_ref, wdw_ref, cbuf_ref, u, seg):
    nseg = TM // seg
    stride = seg + 2 * CONF_PAD
    used = nseg * stride
    zpad = jnp.zeros((CONF_PAD, D), F32)
    for s in range(nseg):
        base = s * stride
        upad_ref[base:base + CONF_PAD, :] = zpad
        upad_ref[base + CONF_PAD:base + CONF_PAD + seg, :] = u[s * seg:(s + 1) * seg, :]
        upad_ref[base + CONF_PAD + seg:base + stride, :] = zpad

    def lane_block(lb, c):
        lanes = pl.ds(pl.multiple_of(lb * CONF_LB, CONF_LB), CONF_LB)
        for b in range(1, SUBLANE):
            shf_ref[b - 1, 0:used - SUBLANE, :] = upad_ref[b:b + used - SUBLANE, lanes]
        for s in range(nseg):
            for rb in range(seg // CONF_RB):
                r0 = s * stride + CONF_PAD + rb * CONF_RB
                acc = jnp.zeros((CONF_RB, CONF_LB), F32)
                for k in range(CONF_W):
                    a, b = divmod(k - CONF_W // 2, SUBLANE)
                    rows = slice(r0 + SUBLANE * a, r0 + SUBLANE * a + CONF_RB)
                    src = upad_ref[rows, lanes] if b == 0 else shf_ref[b - 1, rows, :]
                    acc = acc + wdw_ref[k:k + 1, lanes] * src
                o0 = s * seg + rb * CONF_RB
                cbuf_ref[o0:o0 + CONF_RB, lanes] = acc
        return c

    lax.fori_loop(0, D // CONF_LB, lane_block, 0)


def _conf_kernel(x_ref, mod_ref, w1_ref, b1_ref, wdw_ref, bdw_ref, cg_ref, cb_ref, w2_ref, b2_ref,
                 g_ref, b_ref, o_ref, w1s, w2s, upad, shf, cbuf):
    i = pl.program_id(0)
    shift, scale, gate = _mod_rows(mod_ref, i * TM, (0, 1, 2))

    @pl.when(i == 0)
    def _():
        _cast_rows(w1_ref, w1s, 128)
        _cast_rows(w2_ref, w2s, 128)

    x = x_ref[...]
    h = (x * (1.0 + scale) + shift).astype(BF16)
    ag = jnp.dot(h, w1s[...], preferred_element_type=F32) + b1_ref[...]
    u = ag[:, :D] * jax.nn.sigmoid(ag[:, D:])

    @pl.when(i < CTX_TILES)
    def _():
        _conf_conv(upad, shf, wdw_ref, cbuf, u, SEQ)

    @pl.when(i >= CTX_TILES)
    def _():
        _conf_conv(upad, shf, wdw_ref, cbuf, u, GRID_W)

    uc = _silu(_layer_norm(cbuf[...] + bdw_ref[...], cg_ref[...], cb_ref[...]))
    y = jnp.dot(uc.astype(BF16), w2s[...], preferred_element_type=F32) + b2_ref[...]
    o_ref[...] = _layer_norm(ALPHA * x + gate * y, g_ref[...], b_ref[...])


def _conformer(x, mods, l, j, w1, b1, wdw, bdw, cg, cb, w2, b2, ln_g, ln_b):
    return pl.pallas_call(
        _conf_kernel,
        grid=(NTILE,),
        in_specs=[
            pl.BlockSpec((TM, D), lambda i: (i, 0)),
            _mod_spec(l),
            _stacked((D, 2 * D), j, True), _stacked((1, 2 * D), j),
            _stacked((CONF_W, D), j), _stacked((1, D), j), _stacked((1, D), j), _stacked((1, D), j),
            _stacked((D, D), j, True), _stacked((1, D), j),
            _stacked((1, D), 2 * l), _stacked((1, D), 2 * l),
        ],
        out_specs=pl.BlockSpec((TM, D), lambda i: (i, 0)),
        out_shape=jax.ShapeDtypeStruct((NTOK, D), F32),
        scratch_shapes=[pltpu.VMEM((D, 2 * D), BF16), pltpu.VMEM((D, D), BF16),
                        pltpu.VMEM((CONF_ROWS, D), F32),
                        pltpu.VMEM((SUBLANE - 1, CONF_ROWS, CONF_LB), F32),
                        pltpu.VMEM((TM, D), F32)],
        compiler_params=_cparams("arbitrary"),
        name="conformer",
    )(x, mods, w1, _vec3(b1), wdw, _vec3(bdw), _vec3(cg), _vec3(cb), w2, _vec3(b2),
      _vec3(ln_g), _vec3(ln_b))


def _sconv_kernel(x_ref, xp_ref, xn_ref, mod_ref, win_ref, wc_ref, wout_ref, g_ref, b_ref, o_ref,
                  wins, wouts, cpad, ybuf):
    i = pl.program_id(0)
    shift, scale, gate = _mod_rows(mod_ref, i * TM, (0, 1, 2))

    @pl.when(i == 0)
    def _():
        _cast_rows(win_ref, wins, 128)
        _cast_rows(wout_ref, wouts, 128)

    scale = 1.0 + scale
    x = x_ref[...]
    h = (x * scale + shift).astype(BF16)
    bg = jnp.dot(h, wins[:, :D], preferred_element_type=F32)
    w0, w1, w2 = wc_ref[0:1, :], wc_ref[1:2, :], wc_ref[2:3, :]

    @pl.when(i < CTX_TILES)
    def _():
        cu = jnp.dot(h, wins[:, D:], preferred_element_type=F32)
        cu = cu[:, :D] * cu[:, D:]
        zrow = jnp.zeros((SUBLANE, D), F32)
        cpad[0:SUBLANE, :] = zrow
        cpad[SUBLANE:SUBLANE + TM, :] = cu
        cpad[SUBLANE + TM:2 * SUBLANE + TM, :] = zrow
        ybuf[...] = (w0 * cpad[SUBLANE - 1:SUBLANE - 1 + TM, :] + w1 * cu
                     + w2 * cpad[SUBLANE + 1:SUBLANE + 1 + TM, :])

    @pl.when(i >= CTX_TILES)
    def _():
        r = (i - CTX_TILES) % TILES_PER_LAT
        hp = (xp_ref[...] * scale + shift).astype(BF16)
        hn = (xn_ref[...] * scale + shift).astype(BF16)
        hcat = jnp.concatenate([hp, h, hn], axis=0)
        cu = jnp.dot(hcat, wins[:, D:], preferred_element_type=F32)
        cu = cu[:, :D] * cu[:, D:]
        halo_up = jnp.where(r > 0, cu[0:GRID_W, :], 0.0)
        halo_dn = jnp.where(r < TILES_PER_LAT - 1, cu[GRID_W + TM:, :], 0.0)
        up = jnp.concatenate([halo_up, cu[GRID_W:TM, :]], axis=0)
        dn = jnp.concatenate([cu[2 * GRID_W:GRID_W + TM, :], halo_dn], axis=0)
        ybuf[...] = w0 * up + w1 * cu[GRID_W:GRID_W + TM, :] + w2 * dn

    y = jnp.dot((bg * ybuf[...]).astype(BF16), wouts[...], preferred_element_type=F32)
    o_ref[...] = _layer_norm(ALPHA * x + gate * y, g_ref[...], b_ref[...])


def _short_conv(x, mods, l, j, w_in, w_conv, w_out, ln_g, ln_b):
    halo_per_tile = TM // GRID_W
    n_halo = NTOK // GRID_W
    return pl.pallas_call(
        _sconv_kernel,
        grid=(NTILE,),
        in_specs=[
            pl.BlockSpec((TM, D), lambda i: (i, 0)),
            pl.BlockSpec((GRID_W, D), lambda i: (jnp.maximum(i * halo_per_tile - 1, 0), 0)),
            pl.BlockSpec((GRID_W, D), lambda i: (jnp.minimum((i + 1) * halo_per_tile, n_halo - 1), 0)),
            _mod_spec(l),
            _stacked((D, 3 * D), j, True),
            _stacked((3, D), j),
            _stacked((D, D), j, True),
            _stacked((1, D), 2 * l), _stacked((1, D), 2 * l),
        ],
        out_specs=pl.BlockSpec((TM, D), lambda i: (i, 0)),
        out_shape=jax.ShapeDtypeStruct((NTOK, D), F32),
        scratch_shapes=[pltpu.VMEM((D, 3 * D), BF16), pltpu.VMEM((D, D), BF16),
                        pltpu.VMEM((TM + 2 * SUBLANE, D), F32), pltpu.VMEM((TM, D), F32)],
        compiler_params=_cparams("arbitrary"),
        name="short_conv",
    )(x, x, x, mods, w_in, w_conv, w_out, _vec3(ln_g), _vec3(ln_b))


N_CH = TM // CHUNK


def _scan_seq(t):
    return jnp.where(t < CTX_TILES, t, CTX_TILES + (t - CTX_TILES) // TILES_PER_LAT)


def _lat_seq(t):
    return jnp.clip((t - CTX_TILES) // TILES_PER_LAT, 0, DEC_BATCH - 1)


def _chunk_tri(rev):
    row = lax.broadcasted_iota(jnp.int32, (TM, TM), 0)
    col = lax.broadcasted_iota(jnp.int32, (TM, TM), 1)
    same = (row // CHUNK) == (col // CHUNK)
    return same & ((col >= row) if rev else (col <= row))


def _scan_state_in(s_scr, s0_ref, t, rev):
    first_r = TILES_PER_LAT - 1 if rev else 0
    fresh = jnp.logical_and(t >= CTX_TILES, (t - CTX_TILES) % TILES_PER_LAT == first_r)
    s = jnp.where(fresh, s0_ref[...], s_scr[...])
    return jnp.where(t < CTX_TILES, 0.0, s)


def _decay_operands(q, k, bcum, rev):
    qd, kd, ke, dec = [], [], [], []
    for c in range(N_CH):
        rows = slice(c * CHUNK, (c + 1) * CHUNK)
        b = bcum[rows, :]
        last = b[0:1, :] if rev else b[CHUNK - 1:CHUNK, :]
        qd.append((q[rows, :] * jnp.exp(b)).astype(BF16))
        kd.append((k[rows, :] * jnp.exp(-b)).astype(BF16))
        ke.append(k[rows, :] * jnp.exp(last - b))
        dec.append(jnp.exp(last))
    return (jnp.concatenate(qd, axis=0), jnp.concatenate(kd, axis=0),
            jnp.concatenate(ke, axis=0).T.astype(BF16), dec)


def _scan_tile(heads, s_in, rev):
    mask = _chunk_tri(rev)
    kcol = lax.broadcasted_iota(jnp.int32, (DK, TM), 1) // CHUNK
    order = range(N_CH - 1, -1, -1) if rev else range(N_CH)
    sc, kv = [], []
    for qh, kh, keth, vh, _ in heads:
        sc.append(lax.dot_general(qh, kh, (((1,), (1,)), ((), ())), preferred_element_type=F32))
        kst = jnp.concatenate([jnp.where(kcol == c, keth, jnp.zeros_like(keth)) for c in range(N_CH)],
                              axis=0)
        kv.append(jnp.dot(kst, vh, preferred_element_type=F32))
    o_intra = [jnp.dot(jnp.where(mask, s, 0.0).astype(BF16), hd[3], preferred_element_type=F32)
               for s, hd in zip(sc, heads)]
    outs, s_out = [], []
    for hd, (qh, _, _, _, dec_rows) in enumerate(heads):
        s, o = s_in[hd], [None] * N_CH
        for c in order:
            rows = slice(c * CHUNK, (c + 1) * CHUNK)
            o[c] = o_intra[hd][rows, :] + jnp.dot(qh[rows, :], s.astype(BF16),
                                                  preferred_element_type=F32)
            dec_col = jnp.broadcast_to(dec_rows[c], (DK, DK)).T
            s = s * jnp.concatenate([dec_col, dec_col], axis=1) + kv[hd][c * DK:(c + 1) * DK, :]
        outs.append(o)
        s_out.append(s)
    return outs, s_out


def _gla_fwd_kernel(*refs, n_x):
    x_refs = refs[:n_x]
    (mod_ref, win_ref, wga_ref, wgb_ref, bg_ref, s0_ref,
     of_ref, qd_ref, kd_ref, ket_ref, v_ref, r_ref, dec_ref, st_ref,
     wins, tri_scr, s_scr) = refs[n_x:]
    i = pl.program_id(0)
    shift, scale = _mod_rows(mod_ref, i * TM, (0, 1))

    @pl.when(i == 0)
    def _():
        _cast_rows(win_ref, wins, 128)
        tri_scr[0] = jnp.where(_chunk_tri(False), 1.0, 0.0).astype(BF16)
        tri_scr[1] = jnp.where(_chunk_tri(True), 1.0, 0.0).astype(BF16)
        s_scr[...] = jnp.zeros_like(s_scr)

    h = (_read_x(x_refs, i) * (1.0 + scale) + shift).astype(BF16)

    g_parts = []
    for d in range(2):
        za = jnp.dot(h, wga_ref[d].astype(BF16), preferred_element_type=F32)
        z = jnp.dot(za.astype(BF16), wgb_ref[d].astype(BF16), preferred_element_type=F32) + bg_ref[d]
        g = (jnp.minimum(z, 0.0) - jnp.log1p(jnp.exp(-jnp.abs(z)))) * (1.0 / GATE_NORM)
        g_hi = g.astype(BF16)
        g_parts.append((g_hi, (g - g_hi.astype(F32)).astype(BF16)))

    proj = jnp.dot(h, wins[...], preferred_element_type=F32)
    q = proj[:, :KW] * (DK ** -0.5)
    k = proj[:, KW:2 * KW]
    v = proj[:, 2 * KW:2 * KW + D].astype(BF16)
    v_ref[...] = v
    r_ref[...] = proj[:, 2 * KW + D:]

    bcum = [jnp.dot(tri_scr[d], g_hi, preferred_element_type=F32)
            + jnp.dot(tri_scr[d], g_lo, preferred_element_type=F32)
            for d, (g_hi, g_lo) in enumerate(g_parts)]

    qd_b, kd_b, ket_b, dec_b = _decay_operands(q, k, bcum[1], True)
    qd_ref[...] = qd_b
    kd_ref[...] = kd_b
    ket_ref[...] = ket_b
    dec_ref[...] = jnp.concatenate(dec_b + [jnp.zeros((SUBLANE - N_CH, KW), F32)], axis=0)

    qd, kd, ket, dec = _decay_operands(q, k, bcum[0], False)
    heads = []
    for hd in range(H):
        ks, vs = slice(hd * DK, (hd + 1) * DK), slice(hd * DV, (hd + 1) * DV)
        heads.append((qd[:, ks], kd[:, ks], ket[ks, :], v[:, vs], [e[:, ks] for e in dec]))
    outs, s_out = _scan_tile(heads, _scan_state_in(s_scr, s0_ref, i, False), False)
    for hd in range(H):
        for c in range(N_CH):
            of_ref[c * CHUNK:(c + 1) * CHUNK, hd * DV:(hd + 1) * DV] = outs[hd][c]
        s_scr[hd] = s_out[hd]
        st_ref[hd] = s_out[hd]


def _gla_fwd(x, mods, l, j, w_in, w_ga, w_gb, b_g, state_gla):
    xs = _x_args(x)
    tile = pl.BlockSpec((TM, D), lambda i: (i, 0))
    keys = pl.BlockSpec((TM, KW), lambda i: (i, 0))
    state = (H, DK, DV)
    return pl.pallas_call(
        functools.partial(_gla_fwd_kernel, n_x=len(xs)),
        grid=(NTILE,),
        in_specs=_x_specs(len(xs) == 2) + [
            _mod_spec(l),
            _stacked((D, 2 * KW + 2 * D), j, True),
            _stacked((2, D, RANK), j),
            _stacked((2, RANK, KW), j),
            _stacked((2, 1, KW), j),
            pl.BlockSpec((None, None, None) + state, lambda i: (_lat_seq(i), j, 0, 0, 0, 0)),
        ],
        out_specs=[
            tile, keys, keys,
            pl.BlockSpec((KW, TM), lambda i: (0, i)),
            tile, tile,
            pl.BlockSpec((None, SUBLANE, KW), lambda i: (i, 0, 0)),
            pl.BlockSpec((None,) + state, lambda i: (_scan_seq(i), 0, 0, 0)),
        ],
        out_shape=[
            jax.ShapeDtypeStruct((NTOK, D), F32),
            jax.ShapeDtypeStruct((NTOK, KW), BF16),
            jax.ShapeDtypeStruct((NTOK, KW), BF16),
            jax.ShapeDtypeStruct((KW, NTOK), BF16),
            jax.ShapeDtypeStruct((NTOK, D), BF16),
            jax.ShapeDtypeStruct((NTOK, D), F32),
            jax.ShapeDtypeStruct((NTILE, SUBLANE, KW), F32),
            jax.ShapeDtypeStruct((N_SEQ,) + state, F32),
        ],
        scratch_shapes=[pltpu.VMEM((D, 2 * KW + 2 * D), BF16), pltpu.VMEM((2, TM, TM), BF16),
                        pltpu.VMEM(state, F32)],
        compiler_params=_cparams("arbitrary"),
        name="gla_fwd",
    )(*xs, mods, w_in, w_ga, w_gb, b_g.reshape(b_g.shape[0], 2, 1, KW), state_gla)


def _gla_bwd_kernel(*refs, n_x):
    x_refs = refs[:n_x]
    (qd_ref, kd_ref, ket_ref, v_ref, dec_ref, s0_ref, of_ref, r_ref, mod_ref, gn_ref, wo_ref,
     g_ref, b_ref, o_ref, st_ref, wos, s_scr) = refs[n_x:]
    i = pl.program_id(0)
    t = NTILE - 1 - i
    (gate,) = _mod_rows(mod_ref, t * TM, (2,))

    @pl.when(i == 0)
    def _():
        _cast_rows(wo_ref, wos, 128)
        s_scr[...] = jnp.zeros_like(s_scr)

    heads = []
    for hd in range(H):
        ks, vs = slice(hd * DK, (hd + 1) * DK), slice(hd * DV, (hd + 1) * DV)
        heads.append((qd_ref[:, ks], kd_ref[:, ks], ket_ref[ks, :], v_ref[:, vs],
                      [dec_ref[c:c + 1, ks] for c in range(N_CH)]))
    outs, s_out = _scan_tile(heads, _scan_state_in(s_scr, s0_ref, t, True), True)
    parts = []
    for hd in range(H):
        s_scr[hd] = s_out[hd]
        st_ref[hd] = s_out[hd]
        oh = of_ref[:, hd * DV:(hd + 1) * DV] + jnp.concatenate(outs[hd], axis=0)
        ms = jnp.mean(oh * oh, axis=-1, keepdims=True)
        parts.append(oh * lax.rsqrt(ms + RMS_EPS))

    on = jnp.concatenate(parts, axis=1) * gn_ref[...]
    y = jnp.dot((on * _silu(r_ref[...])).astype(BF16), wos[...], preferred_element_type=F32)
    o_ref[...] = _layer_norm(ALPHA * _read_x(x_refs, t) + gate * y, g_ref[...], b_ref[...])


def _gla_bwd(x, o_f, qd, kd, ket, v, r, dec, state_gla, mods, l, j, gn_g, w_o, ln_g, ln_b):
    xs = _x_args(x)
    rtile = lambda i: NTILE - 1 - i
    tile = pl.BlockSpec((TM, D), lambda i: (rtile(i), 0))
    keys = pl.BlockSpec((TM, KW), lambda i: (rtile(i), 0))
    state = (H, DK, DV)
    return pl.pallas_call(
        functools.partial(_gla_bwd_kernel, n_x=len(xs)),
        grid=(NTILE,),
        in_specs=_x_specs(len(xs) == 2, rtile) + [
            keys, keys,
            pl.BlockSpec((KW, TM), lambda i: (0, rtile(i))),
            tile,
            pl.BlockSpec((None, SUBLANE, KW), lambda i: (rtile(i), 0, 0)),
            pl.BlockSpec((None, None, None) + state, lambda i: (_lat_seq(rtile(i)), j, 1, 0, 0, 0)),
            tile, tile, _mod_spec(l),
            _stacked((1, D), j), _stacked((D, D), j, True),
            _stacked((1, D), 2 * l), _stacked((1, D), 2 * l)],
        out_specs=[tile, pl.BlockSpec((None,) + state, lambda i: (_scan_seq(rtile(i)), 0, 0, 0))],
        out_shape=[jax.ShapeDtypeStruct((NTOK, D), F32), jax.ShapeDtypeStruct((N_SEQ,) + state, F32)],
        scratch_shapes=[pltpu.VMEM((D, D), BF16), pltpu.VMEM(state, F32)],
        compiler_params=_cparams("arbitrary"),
        name="gla_bwd",
    )(*xs, qd, kd, ket, v, dec, state_gla, o_f, r, mods, _vec3(gn_g), w_o, _vec3(ln_g), _vec3(ln_b))


def kernel(x_prompt, x_sample, c, state_gla, c_ctx, mod_w, mod_b, ln_g, ln_b, ff_w1, ff_w2, gla_w_in, gla_w_ga, gla_w_gb, gla_b_g, gla_gn_g, gla_w_o, conf_w_pw1, conf_b_pw1, conf_w_dw, conf_b_dw, conf_ln_g, conf_ln_b, conf_w_pw2, conf_b_pw2, sc_w_in, sc_w_conv, sc_w_out):
    assert x_prompt.shape == (BATCH, SEQ, D) and x_sample.shape == (DEC_BATCH, DEC_SEQ, D)
    x = (x_prompt.reshape(NCTX, D), x_sample.reshape(NLAT, D))

    c8 = jnp.concatenate([c_ctx[None, :], c, jnp.zeros((SUBLANE - N_CVEC, D), F32)], axis=0)
    mods = _adaln(c8, mod_w, mod_b)

    states = []
    for l in range(DEPTH):
        kind, j = l % 3, l // 3
        if kind == 0:
            o_f, qd, kd, ket, v, r, dec, st_f = _gla_fwd(x, mods, l, j, gla_w_in, gla_w_ga, gla_w_gb,
                                                          gla_b_g, state_gla)
            x, st_b = _gla_bwd(x, o_f, qd, kd, ket, v, r, dec, state_gla, mods, l, j, gla_gn_g, gla_w_o,
                               ln_g, ln_b)
            states.append(jnp.stack([st_f[:BATCH], st_b[:BATCH]], axis=1))
        elif kind == 1:
            x = _conformer(x, mods, l, j, conf_w_pw1, conf_b_pw1, conf_w_dw, conf_b_dw, conf_ln_g,
                           conf_ln_b, conf_w_pw2, conf_b_pw2, ln_g, ln_b)
        else:
            x = _short_conv(x, mods, l, j, sc_w_in, sc_w_conv, sc_w_out, ln_g, ln_b)
        x = _mlp(x, mods, l, ff_w1, ff_w2, ln_g, ln_b, split_out=(l == DEPTH - 1))

    y_prompt, y_sample = x
    return (y_prompt.reshape(BATCH, SEQ, D), y_sample.reshape(DEC_BATCH, DEC_SEQ, D),
            jnp.stack(states, axis=1))
```

```python
import functools

import jax
import jax.numpy as jnp
from jax import lax
from jax.experimental import pallas as pl
from jax.experimental.pallas import tpu as pltpu

F32 = jnp.float32
BF16 = jnp.bfloat16

D = 1024
DEPTH = 4
BATCH, SEQ = 16, 256
DEC_BATCH, DEC_SEQ = 2, 2048
GRID_W = 64
N_MOD = 6
N_CVEC = 1 + DEC_BATCH
NCTX = BATCH * SEQ
NLAT = DEC_BATCH * DEC_SEQ
NTOK = NCTX + NLAT
H, DK, DV = 4, 128, 256
KW = H * DK
RANK = 16
CHUNK = 64
GATE_NORM = 16.0
CONF_W = 31
D_FF = 4 * D
LN_EPS = 1e-5
RMS_EPS = 1e-6
ALPHA = (2 * DEPTH) ** 0.25

TM = 256
NTILE = NTOK // TM
CTX_TILES = NCTX // TM
TILES_PER_LAT = DEC_SEQ // TM
N_SEQ = BATCH + DEC_BATCH
TM_MLP = 512
NT_MLP = NTOK // TM_MLP
TF_MLP = 1024
NJ_MLP = D_FF // TF_MLP
TC_MLP = 512
TN_MOD = 2048
SUBLANE = 8
VMEM_LIMIT = 58 * 1024 * 1024


def _cparams(*sem):
    return pltpu.CompilerParams(dimension_semantics=sem, vmem_limit_bytes=VMEM_LIMIT)


def _mod_rows(mod_ref, row0, ks):
    m = jnp.where(row0 < NCTX, 0, 1 + (row0 - NCTX) // DEC_SEQ)
    return [mod_ref[pl.ds(m, 1), k * D:(k + 1) * D] for k in ks]


def _mod_spec(l):
    return pl.BlockSpec((None, SUBLANE, N_MOD * D), lambda *_: (l, 0, 0))


def _stacked(block, j, single_buffer=False):
    nd = len(block)
    mode = dict(pipeline_mode=pl.Buffered(1)) if single_buffer else {}
    return pl.BlockSpec((None,) + block, lambda *_: (j,) + (0,) * nd, **mode)


def _x_specs(split, tile=lambda i: i):
    if not split:
        return [pl.BlockSpec((TM, D), lambda i: (tile(i), 0))]
    return [pl.BlockSpec((TM, D), lambda i: (jnp.clip(tile(i), 0, CTX_TILES - 1), 0)),
            pl.BlockSpec((TM, D), lambda i: (jnp.clip(tile(i) - CTX_TILES, 0, NTILE - CTX_TILES - 1), 0))]


def _x_args(x):
    return list(x) if isinstance(x, tuple) else [x]


def _read_x(x_refs, t):
    if len(x_refs) == 1:
        return x_refs[0][...]
    return jnp.where(t < CTX_TILES, x_refs[0][...], x_refs[1][...])


def _layer_norm(y, g, b):
    mu = jnp.mean(y, axis=-1, keepdims=True)
    yc = y - mu
    var = jnp.mean(yc * yc, axis=-1, keepdims=True)
    return yc * lax.rsqrt(var + LN_EPS) * g + b


def _silu(x):
    return x * jax.nn.sigmoid(x)


def _cast_rows(src_ref, dst_ref, step):
    n = src_ref.shape[0] // step

    def body(r, c):
        rows = pl.ds(pl.multiple_of(r * step, step), step)
        dst_ref[rows, :] = src_ref[rows, :].astype(BF16)
        return c

    lax.fori_loop(0, n, body, 0)


def _vec3(a):
    return a.reshape(-1, 1, a.shape[-1])


def _adaln_kernel(c_ref, w_ref, b_ref, o_ref):
    s = _silu(c_ref[...]).astype(BF16)
    o_ref[...] = jnp.dot(s, w_ref[...].astype(BF16), preferred_element_type=F32) + b_ref[...]


def _adaln(c8, mod_w, mod_b):
    return pl.pallas_call(
        _adaln_kernel,
        grid=(DEPTH, N_MOD * D // TN_MOD),
        in_specs=[
            pl.BlockSpec((SUBLANE, D), lambda l, n: (0, 0)),
            pl.BlockSpec((None, D, TN_MOD), lambda l, n: (l, 0, n)),
            pl.BlockSpec((None, 1, TN_MOD), lambda l, n: (l, 0, n)),
        ],
        out_specs=pl.BlockSpec((None, SUBLANE, TN_MOD), lambda l, n: (l, 0, n)),
        out_shape=jax.ShapeDtypeStruct((DEPTH, SUBLANE, N_MOD * D), F32),
        compiler_params=_cparams("parallel", "parallel"),
        name="adaln",
    )(c8, mod_w, _vec3(mod_b))


def _mlp_tile(s):
    return jnp.clip(s - (NJ_MLP - 1), 0, NT_MLP - 1)


def _mlp_done(s):
    return jnp.clip(s - NJ_MLP, 0, NT_MLP - 1)


def _sqrelu(a):
    return jnp.square(jnp.maximum(a, 0.0)).astype(BF16)


def _mlp_kernel(x_ref, xd_ref, mod_ref, w1_ref, w2_ref, g_ref, b_ref, *rest, split_out):
    n_out = 2 if split_out else 1
    o_refs, (w1s, w2s, h_scr, acc_scr), y_scr = rest[:n_out], rest[n_out:n_out + 4], rest[n_out + 4:]
    s = pl.program_id(0)
    shift, scale = _mod_rows(mod_ref, _mlp_tile(s) * TM_MLP, (3, 4))
    done = _mlp_done(s)

    def finish():
        (gate,) = _mod_rows(mod_ref, done * TM_MLP, (5,))
        y = _layer_norm(ALPHA * xd_ref[...] + gate * acc_scr[...], g_ref[...], b_ref[...])
        if split_out:
            y_scr[0][...] = y
        else:
            o_refs[0][...] = y

    def route():
        if split_out:
            @pl.when(done < NCTX // TM_MLP)
            def _():
                o_refs[0][...] = y_scr[0][...]

            @pl.when(done >= NCTX // TM_MLP)
            def _():
                o_refs[1][...] = y_scr[0][...]

    @pl.when(s < NJ_MLP)
    def _():
        w1s[s] = w1_ref[...].astype(BF16)
        w2s[s] = w2_ref[...].astype(BF16)

        @pl.when(s == 0)
        def _():
            h_scr[...] = (x_ref[...] * (1.0 + scale) + shift).astype(BF16)
            acc_scr[...] = jnp.zeros_like(acc_scr)

        a = _sqrelu(jnp.dot(h_scr[...], w1s[s], preferred_element_type=F32))
        acc_scr[...] += jnp.dot(a, w2s[s], preferred_element_type=F32)

    @pl.when(jnp.logical_and(s >= NJ_MLP, s < NJ_MLP + NT_MLP - 1))
    def _():
        finish()
        h_scr[...] = (x_ref[...] * (1.0 + scale) + shift).astype(BF16)
        chunks = [(j, slice(c * TC_MLP, (c + 1) * TC_MLP))
                  for j in range(NJ_MLP) for c in range(TF_MLP // TC_MLP)]
        up = lambda j, cols: _sqrelu(jnp.dot(h_scr[...], w1s[j, :, cols], preferred_element_type=F32))
        a_prev, acc = up(*chunks[0]), None
        for (j, cols), nxt in zip(chunks, chunks[1:] + [None]):
            a_next = up(*nxt) if nxt is not None else None
            part = jnp.dot(a_prev, w2s[j, cols, :], preferred_element_type=F32)
            acc = part if acc is None else acc + part
            a_prev = a_next
        acc_scr[...] = acc
        route()

    @pl.when(s == NJ_MLP + NT_MLP - 1)
    def _():
        finish()
        route()


def _mlp(x, mods, l, w1, w2, ln_g, ln_b, split_out):
    nc = NCTX // TM_MLP
    chunk = lambda s: jnp.minimum(s, NJ_MLP - 1)
    scratch = [pltpu.VMEM((NJ_MLP, D, TF_MLP), BF16), pltpu.VMEM((NJ_MLP, TF_MLP, D), BF16),
               pltpu.VMEM((TM_MLP, D), BF16), pltpu.VMEM((TM_MLP, D), F32)]
    if split_out:
        out_specs = [pl.BlockSpec((TM_MLP, D), lambda s: (jnp.minimum(_mlp_done(s), nc - 1), 0)),
                     pl.BlockSpec((TM_MLP, D), lambda s: (jnp.maximum(_mlp_done(s) - nc, 0), 0))]
        out_shape = [jax.ShapeDtypeStruct((NCTX, D), F32), jax.ShapeDtypeStruct((NLAT, D), F32)]
        scratch.append(pltpu.VMEM((TM_MLP, D), F32))
    else:
        out_specs = pl.BlockSpec((TM_MLP, D), lambda s: (_mlp_done(s), 0))
        out_shape = jax.ShapeDtypeStruct((NTOK, D), F32)
    return pl.pallas_call(
        functools.partial(_mlp_kernel, split_out=split_out),
        grid=(NJ_MLP + NT_MLP,),
        in_specs=[
            pl.BlockSpec((TM_MLP, D), lambda s: (_mlp_tile(s), 0)),
            pl.BlockSpec((TM_MLP, D), lambda s: (_mlp_done(s), 0)),
            _mod_spec(l),
            pl.BlockSpec((None, D, TF_MLP), lambda s: (l, 0, chunk(s))),
            pl.BlockSpec((None, TF_MLP, D), lambda s: (l, chunk(s), 0)),
            _stacked((1, D), 2 * l + 1), _stacked((1, D), 2 * l + 1),
        ],
        out_specs=out_specs,
        out_shape=out_shape,
        scratch_shapes=scratch,
        compiler_params=_cparams("arbitrary"),
        name="mlp",
    )(x, x, mods, w1, w2, _vec3(ln_g), _vec3(ln_b))


CONF_PAD = 16
CONF_RB = 64
CONF_LB = 256
CONF_ROWS = (TM // GRID_W) * (GRID_W + 2 * CONF_PAD)


def _conf_conv(upad_ref, shf_ref, w_taps, u, joined):
    nseg = TM // GRID_W
    stride = GRID_W + 2 * CONF_PAD
    zpad = jnp.zeros((CONF_PAD, CONF_LB), F32)
    for s in range(nseg):
        base, r0 = s * stride, s * GRID_W
        above = jnp.where(joined, u[r0 - CONF_PAD:r0, :], 0.0) if s > 0 else zpad
        below = jnp.where(joined, u[r0 + GRID_W:r0 + GRID_W + CONF_PAD, :], 0.0) if s < nseg - 1 else zpad
        upad_ref[base:base + CONF_PAD, :] = above
        upad_ref[base + CONF_PAD:base + CONF_PAD + GRID_W, :] = u[r0:r0 + GRID_W, :]
        upad_ref[base + CONF_PAD + GRID_W:base + stride, :] = below
    for b in range(1, SUBLANE):
        shf_ref[b - 1, 0:CONF_ROWS - SUBLANE, :] = upad_ref[b:b + CONF_ROWS - SUBLANE, :]
    out = []
    for s in range(nseg):
        r0 = s * stride + CONF_PAD
        acc = jnp.zeros((GRID_W, CONF_LB), F32)
        for k in range(CONF_W):
            a, b = divmod(k - CONF_W // 2, SUBLANE)
            rows = slice(r0 + SUBLANE * a, r0 + SUBLANE * a + GRID_W)
            acc = acc + w_taps[k] * (upad_ref[rows, :] if b == 0 else shf_ref[b - 1, rows, :])
        out.append(acc)
    return jnp.concatenate(out, axis=0)


def _conf_kernel(x_ref, mod_ref, w1_ref, b1_ref, wdw_ref, bdw_ref, cg_ref, cb_ref, w2_ref, b2_ref,
                 g_ref, b_ref, o_ref, w1s, w2s, upad, shf):
    i = pl.program_id(0)
    shift, scale, gate = _mod_rows(mod_ref, i * TM, (0, 1, 2))

    @pl.when(i == 0)
    def _():
        _cast_rows(w1_ref, w1s, 128)
        _cast_rows(w2_ref, w2s, 128)

    x = x_ref[...]
    h = (x * (1.0 + scale) + shift).astype(BF16)

    def glu(lb):
        cols = slice(lb * CONF_LB, (lb + 1) * CONF_LB)
        gcols = slice(D + lb * CONF_LB, D + (lb + 1) * CONF_LB)
        a = jnp.dot(h, w1s[:, cols], preferred_element_type=F32) + b1_ref[:, cols]
        g = jnp.dot(h, w1s[:, gcols], preferred_element_type=F32) + b1_ref[:, gcols]
        return a * jax.nn.sigmoid(g)

    n_lb = D // CONF_LB
    conv = []
    u = glu(0)
    for lb in range(n_lb):
        u_next = glu(lb + 1) if lb + 1 < n_lb else None
        cols = slice(lb * CONF_LB, (lb + 1) * CONF_LB)
        taps = [wdw_ref[k:k + 1, cols] for k in range(CONF_W)]
        conv.append(_conf_conv(upad, shf, taps, u, i < CTX_TILES))
        u = u_next

    uc = jnp.concatenate(conv, axis=1) + bdw_ref[...]
    uc = _silu(_layer_norm(uc, cg_ref[...], cb_ref[...]))
    y = jnp.dot(uc.astype(BF16), w2s[...], preferred_element_type=F32) + b2_ref[...]
    o_ref[...] = _layer_norm(ALPHA * x + gate * y, g_ref[...], b_ref[...])


def _conformer(x, mods, l, j, w1, b1, wdw, bdw, cg, cb, w2, b2, ln_g, ln_b):
    return pl.pallas_call(
        _conf_kernel,
        grid=(NTILE,),
        in_specs=[
            pl.BlockSpec((TM, D), lambda i: (i, 0)),
            _mod_spec(l),
            _stacked((D, 2 * D), j, True), _stacked((1, 2 * D), j),
            _stacked((CONF_W, D), j), _stacked((1, D), j), _stacked((1, D), j), _stacked((1, D), j),
            _stacked((D, D), j, True), _stacked((1, D), j),
            _stacked((1, D), 2 * l), _stacked((1, D), 2 * l),
        ],
        out_specs=pl.BlockSpec((TM, D), lambda i: (i, 0)),
        out_shape=jax.ShapeDtypeStruct((NTOK, D), F32),
        scratch_shapes=[pltpu.VMEM((D, 2 * D), BF16), pltpu.VMEM((D, D), BF16),
                        pltpu.VMEM((CONF_ROWS, CONF_LB), F32),
                        pltpu.VMEM((SUBLANE - 1, CONF_ROWS, CONF_LB), F32)],
        compiler_params=_cparams("arbitrary"),
        name="conformer",
    )(x, mods, w1, _vec3(b1), wdw, _vec3(bdw), _vec3(cg), _vec3(cb), w2, _vec3(b2),
      _vec3(ln_g), _vec3(ln_b))


def _sconv_kernel(x_ref, xp_ref, xn_ref, mod_ref, win_ref, wc_ref, wout_ref, g_ref, b_ref, o_ref,
                  wins, wouts, cpad, ybuf):
    i = pl.program_id(0)
    shift, scale, gate = _mod_rows(mod_ref, i * TM, (0, 1, 2))

    @pl.when(i == 0)
    def _():
        _cast_rows(win_ref, wins, 128)
        _cast_rows(wout_ref, wouts, 128)

    scale = 1.0 + scale
    x = x_ref[...]
    h = (x * scale + shift).astype(BF16)
    bg = jnp.dot(h, wins[:, :D], preferred_element_type=F32)
    w0, w1, w2 = wc_ref[0:1, :], wc_ref[1:2, :], wc_ref[2:3, :]

    @pl.when(i < CTX_TILES)
    def _():
        cu = jnp.dot(h, wins[:, D:], preferred_element_type=F32)
        cu = cu[:, :D] * cu[:, D:]
        zrow = jnp.zeros((SUBLANE, D), F32)
        cpad[0:SUBLANE, :] = zrow
        cpad[SUBLANE:SUBLANE + TM, :] = cu
        cpad[SUBLANE + TM:2 * SUBLANE + TM, :] = zrow
        ybuf[...] = (w0 * cpad[SUBLANE - 1:SUBLANE - 1 + TM, :] + w1 * cu
                     + w2 * cpad[SUBLANE + 1:SUBLANE + 1 + TM, :])

    @pl.when(i >= CTX_TILES)
    def _():
        r = (i - CTX_TILES) % TILES_PER_LAT
        hp = (xp_ref[...] * scale + shift).astype(BF16)
        hn = (xn_ref[...] * scale + shift).astype(BF16)
        hcat = jnp.concatenate([hp, h, hn], axis=0)
        cu = jnp.dot(hcat, wins[:, D:], preferred_element_type=F32)
        cu = cu[:, :D] * cu[:, D:]
        halo_up = jnp.where(r > 0, cu[0:GRID_W, :], 0.0)
        halo_dn = jnp.where(r < TILES_PER_LAT - 1, cu[GRID_W + TM:, :], 0.0)
        up = jnp.concatenate([halo_up, cu[GRID_W:TM, :]], axis=0)
        dn = jnp.concatenate([cu[2 * GRID_W:GRID_W + TM, :], halo_dn], axis=0)
        ybuf[...] = w0 * up + w1 * cu[GRID_W:GRID_W + TM, :] + w2 * dn

    y = jnp.dot((bg * ybuf[...]).astype(BF16), wouts[...], preferred_element_type=F32)
    o_ref[...] = _layer_norm(ALPHA * x + gate * y, g_ref[...], b_ref[...])


def _short_conv(x, mods, l, j, w_in, w_conv, w_out, ln_g, ln_b):
    halo_per_tile = TM // GRID_W
    n_halo = NTOK // GRID_W
    return pl.pallas_call(
        _sconv_kernel,
        grid=(NTILE,),
        in_specs=[
            pl.BlockSpec((TM, D), lambda i: (i, 0)),
            pl.BlockSpec((GRID_W, D), lambda i: (jnp.maximum(i * halo_per_tile - 1, 0), 0)),
            pl.BlockSpec((GRID_W, D), lambda i: (jnp.minimum((i + 1) * halo_per_tile, n_halo - 1), 0)),
            _mod_spec(l),
            _stacked((D, 3 * D), j, True),
            _stacked((3, D), j),
            _stacked((D, D), j, True),
            _stacked((1, D), 2 * l), _stacked((1, D), 2 * l),
        ],
        out_specs=pl.BlockSpec((TM, D), lambda i: (i, 0)),
        out_shape=jax.ShapeDtypeStruct((NTOK, D), F32),
        scratch_shapes=[pltpu.VMEM((D, 3 * D), BF16), pltpu.VMEM((D, D), BF16),
                        pltpu.VMEM((TM + 2 * SUBLANE, D), F32), pltpu.VMEM((TM, D), F32)],
        compiler_params=_cparams("arbitrary"),
        name="short_conv",
    )(x, x, x, mods, w_in, w_conv, w_out, _vec3(ln_g), _vec3(ln_b))


N_CH = TM // CHUNK


STATE = (H, DK, DV)


def _state_out_spec(tile, j, d):
    return pl.BlockSpec((None, None, None) + STATE,
                        lambda i: (jnp.minimum(tile(i), CTX_TILES - 1), j, d, 0, 0, 0))


def _emit_state(st_ref, s_out, t):
    @pl.when(t < CTX_TILES)
    def _():
        for hd in range(H):
            st_ref[hd] = s_out[hd]


def _lat_seq(t):
    return jnp.clip((t - CTX_TILES) // TILES_PER_LAT, 0, DEC_BATCH - 1)


def _chunk_tri(rev):
    row = lax.broadcasted_iota(jnp.int32, (TM, TM), 0)
    col = lax.broadcasted_iota(jnp.int32, (TM, TM), 1)
    same = (row // CHUNK) == (col // CHUNK)
    return same & ((col >= row) if rev else (col <= row))


def _scan_state_in(s_scr, s0_ref, t, rev):
    first_r = TILES_PER_LAT - 1 if rev else 0
    fresh = jnp.logical_and(t >= CTX_TILES, (t - CTX_TILES) % TILES_PER_LAT == first_r)
    s = jnp.where(fresh, s0_ref[...], s_scr[...])
    return jnp.where(t < CTX_TILES, 0.0, s)


def _decay_operands(q, k, bcum, rev):
    qd, kd, ke, dec = [], [], [], []
    for c in range(N_CH):
        rows = slice(c * CHUNK, (c + 1) * CHUNK)
        b = bcum[rows, :]
        last = b[0:1, :] if rev else b[CHUNK - 1:CHUNK, :]
        qd.append((q[rows, :] * jnp.exp(b)).astype(BF16))
        kd.append((k[rows, :] * jnp.exp(-b)).astype(BF16))
        ke.append(k[rows, :] * jnp.exp(last - b))
        dec.append(jnp.exp(last))
    return (jnp.concatenate(qd, axis=0), jnp.concatenate(kd, axis=0),
            jnp.concatenate(ke, axis=0).T.astype(BF16), dec)


def _scan_tile(heads, s_in, rev):
    mask = _chunk_tri(rev)
    kcol = lax.broadcasted_iota(jnp.int32, (DK, TM), 1) // CHUNK
    order = range(N_CH - 1, -1, -1) if rev else range(N_CH)
    sc, kv = [], []
    for qh, kh, keth, vh, _ in heads:
        sc.append(lax.dot_general(qh, kh, (((1,), (1,)), ((), ())), preferred_element_type=F32))
        kst = jnp.concatenate([jnp.where(kcol == c, keth, jnp.zeros_like(keth)) for c in range(N_CH)],
                              axis=0)
        kv.append(jnp.dot(kst, vh, preferred_element_type=F32))
    o_intra = [jnp.dot(jnp.where(mask, s, 0.0).astype(BF16), hd[3], preferred_element_type=F32)
               for s, hd in zip(sc, heads)]
    outs, s_out = [], []
    for hd, (qh, _, _, _, dec_rows) in enumerate(heads):
        s, o = s_in[hd], [None] * N_CH
        for c in order:
            rows = slice(c * CHUNK, (c + 1) * CHUNK)
            o[c] = o_intra[hd][rows, :] + jnp.dot(qh[rows, :], s.astype(BF16),
                                                  preferred_element_type=F32)
            dec_col = jnp.broadcast_to(dec_rows[c], (DK, DK)).T
            s = s * jnp.concatenate([dec_col, dec_col], axis=1) + kv[hd][c * DK:(c + 1) * DK, :]
        outs.append(o)
        s_out.append(s)
    return outs, s_out


def _gla_fwd_kernel(*refs, n_x):
    x_refs = refs[:n_x]
    (mod_ref, win_ref, wga_ref, wgb_ref, bg_ref, s0_ref, _,
     of_ref, qd_ref, kd_ref, ket_ref, v_ref, r_ref, dec_ref, st_ref,
     wins, tri_scr, s_scr) = refs[n_x:]
    i = pl.program_id(0)
    shift, scale = _mod_rows(mod_ref, i * TM, (0, 1))

    @pl.when(i == 0)
    def _():
        _cast_rows(win_ref, wins, 128)
        tri_scr[0] = jnp.where(_chunk_tri(False), 1.0, 0.0).astype(BF16)
        tri_scr[1] = jnp.where(_chunk_tri(True), 1.0, 0.0).astype(BF16)
        s_scr[...] = jnp.zeros_like(s_scr)

    h = (_read_x(x_refs, i) * (1.0 + scale) + shift).astype(BF16)

    g_parts = []
    for d in range(2):
        za = jnp.dot(h, wga_ref[d].astype(BF16), preferred_element_type=F32)
        z = jnp.dot(za.astype(BF16), wgb_ref[d].astype(BF16), preferred_element_type=F32) + bg_ref[d]
        g = (jnp.minimum(z, 0.0) - jnp.log1p(jnp.exp(-jnp.abs(z)))) * (1.0 / GATE_NORM)
        g_hi = g.astype(BF16)
        g_parts.append((g_hi, (g - g_hi.astype(F32)).astype(BF16)))

    proj = jnp.dot(h, wins[...], preferred_element_type=F32)
    q = proj[:, :KW] * (DK ** -0.5)
    k = proj[:, KW:2 * KW]
    v = proj[:, 2 * KW:2 * KW + D].astype(BF16)
    v_ref[...] = v
    r_ref[...] = proj[:, 2 * KW + D:]

    bcum = [jnp.dot(tri_scr[d], g_hi, preferred_element_type=F32)
            + jnp.dot(tri_scr[d], g_lo, preferred_element_type=F32)
            for d, (g_hi, g_lo) in enumerate(g_parts)]

    qd_b, kd_b, ket_b, dec_b = _decay_operands(q, k, bcum[1], True)
    qd_ref[...] = qd_b
    kd_ref[...] = kd_b
    ket_ref[...] = ket_b
    dec_ref[...] = jnp.concatenate(dec_b + [jnp.zeros((SUBLANE - N_CH, KW), F32)], axis=0)

    qd, kd, ket, dec = _decay_operands(q, k, bcum[0], False)
    heads = []
    for hd in range(H):
        ks, vs = slice(hd * DK, (hd + 1) * DK), slice(hd * DV, (hd + 1) * DV)
        heads.append((qd[:, ks], kd[:, ks], ket[ks, :], v[:, vs], [e[:, ks] for e in dec]))
    outs, s_out = _scan_tile(heads, _scan_state_in(s_scr, s0_ref, i, False), False)
    for hd in range(H):
        for c in range(N_CH):
            of_ref[c * CHUNK:(c + 1) * CHUNK, hd * DV:(hd + 1) * DV] = outs[hd][c]
        s_scr[hd] = s_out[hd]
    _emit_state(st_ref, s_out, i)


def _gla_fwd(x, mods, l, j, w_in, w_ga, w_gb, b_g, state_gla, new_states):
    xs = _x_args(x)
    tile = pl.BlockSpec((TM, D), lambda i: (i, 0))
    keys = pl.BlockSpec((TM, KW), lambda i: (i, 0))
    n_in = len(xs) + 7
    return pl.pallas_call(
        functools.partial(_gla_fwd_kernel, n_x=len(xs)),
        grid=(NTILE,),
        in_specs=_x_specs(len(xs) == 2) + [
            _mod_spec(l),
            _stacked((D, 2 * KW + 2 * D), j, True),
            _stacked((2, D, RANK), j),
            _stacked((2, RANK, KW), j),
            _stacked((2, 1, KW), j),
            pl.BlockSpec((None, None, None) + STATE, lambda i: (_lat_seq(i), j, 0, 0, 0, 0)),
            pl.BlockSpec(memory_space=pl.ANY),
        ],
        out_specs=[
            tile, keys, keys,
            pl.BlockSpec((KW, TM), lambda i: (0, i)),
            tile, tile,
            pl.BlockSpec((None, SUBLANE, KW), lambda i: (i, 0, 0)),
            _state_out_spec(lambda i: i, j, 0),
        ],
        out_shape=[
            jax.ShapeDtypeStruct((NTOK, D), F32),
            jax.ShapeDtypeStruct((NTOK, KW), BF16),
            jax.ShapeDtypeStruct((NTOK, KW), BF16),
            jax.ShapeDtypeStruct((KW, NTOK), BF16),
            jax.ShapeDtypeStruct((NTOK, D), BF16),
            jax.ShapeDtypeStruct((NTOK, D), F32),
            jax.ShapeDtypeStruct((NTILE, SUBLANE, KW), F32),
            jax.ShapeDtypeStruct(new_states.shape, F32),
        ],
        input_output_aliases={n_in - 1: 7},
        scratch_shapes=[pltpu.VMEM((D, 2 * KW + 2 * D), BF16), pltpu.VMEM((2, TM, TM), BF16),
                        pltpu.VMEM(STATE, F32)],
        compiler_params=_cparams("arbitrary"),
        name="gla_fwd",
    )(*xs, mods, w_in, w_ga, w_gb, b_g.reshape(b_g.shape[0], 2, 1, KW), state_gla, new_states)


def _gla_bwd_kernel(*refs, n_x):
    x_refs = refs[:n_x]
    (qd_ref, kd_ref, ket_ref, v_ref, dec_ref, s0_ref, of_ref, r_ref, mod_ref, gn_ref, wo_ref,
     g_ref, b_ref, _, o_ref, st_ref, wos, s_scr) = refs[n_x:]
    i = pl.program_id(0)
    t = NTILE - 1 - i
    (gate,) = _mod_rows(mod_ref, t * TM, (2,))

    @pl.when(i == 0)
    def _():
        _cast_rows(wo_ref, wos, 128)
        s_scr[...] = jnp.zeros_like(s_scr)

    heads = []
    for hd in range(H):
        ks, vs = slice(hd * DK, (hd + 1) * DK), slice(hd * DV, (hd + 1) * DV)
        heads.append((qd_ref[:, ks], kd_ref[:, ks], ket_ref[ks, :], v_ref[:, vs],
                      [dec_ref[c:c + 1, ks] for c in range(N_CH)]))
    outs, s_out = _scan_tile(heads, _scan_state_in(s_scr, s0_ref, t, True), True)
    parts = []
    for hd in range(H):
        s_scr[hd] = s_out[hd]
        oh = of_ref[:, hd * DV:(hd + 1) * DV] + jnp.concatenate(outs[hd], axis=0)
        ms = jnp.mean(oh * oh, axis=-1, keepdims=True)
        parts.append(oh * lax.rsqrt(ms + RMS_EPS))

    on = jnp.concatenate(parts, axis=1) * gn_ref[...]
    y = jnp.dot((on * _silu(r_ref[...])).astype(BF16), wos[...], preferred_element_type=F32)
    o_ref[...] = _layer_norm(ALPHA * _read_x(x_refs, t) + gate * y, g_ref[...], b_ref[...])
    _emit_state(st_ref, s_out, t)


def _gla_bwd(x, o_f, qd, kd, ket, v, r, dec, state_gla, new_states, mods, l, j, gn_g, w_o, ln_g, ln_b):
    xs = _x_args(x)
    rtile = lambda i: NTILE - 1 - i
    tile = pl.BlockSpec((TM, D), lambda i: (rtile(i), 0))
    keys = pl.BlockSpec((TM, KW), lambda i: (rtile(i), 0))
    n_in = len(xs) + 14
    return pl.pallas_call(
        functools.partial(_gla_bwd_kernel, n_x=len(xs)),
        grid=(NTILE,),
        in_specs=_x_specs(len(xs) == 2, rtile) + [
            keys, keys,
            pl.BlockSpec((KW, TM), lambda i: (0, rtile(i))),
            tile,
            pl.BlockSpec((None, SUBLANE, KW), lambda i: (rtile(i), 0, 0)),
            pl.BlockSpec((None, None, None) + STATE, lambda i: (_lat_seq(rtile(i)), j, 1, 0, 0, 0)),
            tile, tile, _mod_spec(l),
            _stacked((1, D), j), _stacked((D, D), j, True),
            _stacked((1, D), 2 * l), _stacked((1, D), 2 * l),
            pl.BlockSpec(memory_space=pl.ANY)],
        out_specs=[tile, _state_out_spec(rtile, j, 1)],
        out_shape=[jax.ShapeDtypeStruct((NTOK, D), F32), jax.ShapeDtypeStruct(new_states.shape, F32)],
        input_output_aliases={n_in - 1: 1},
        scratch_shapes=[pltpu.VMEM((D, D), BF16), pltpu.VMEM(STATE, F32)],
        compiler_params=_cparams("arbitrary"),
        name="gla_bwd",
    )(*xs, qd, kd, ket, v, dec, state_gla, o_f, r, mods, _vec3(gn_g), w_o, _vec3(ln_g), _vec3(ln_b),
      new_states)


def kernel(x_prompt, x_sample, c, state_gla, c_ctx, mod_w, mod_b, ln_g, ln_b, ff_w1, ff_w2, gla_w_in, gla_w_ga, gla_w_gb, gla_b_g, gla_gn_g, gla_w_o, conf_w_pw1, conf_b_pw1, conf_w_dw, conf_b_dw, conf_ln_g, conf_ln_b, conf_w_pw2, conf_b_pw2, sc_w_in, sc_w_conv, sc_w_out):
    assert x_prompt.shape == (BATCH, SEQ, D) and x_sample.shape == (DEC_BATCH, DEC_SEQ, D)
    x = (x_prompt.reshape(NCTX, D), x_sample.reshape(NLAT, D))

    c8 = jnp.concatenate([c_ctx[None, :], c, jnp.zeros((SUBLANE - N_CVEC, D), F32)], axis=0)
    mods = _adaln(c8, mod_w, mod_b)

    states = jnp.zeros((BATCH, state_gla.shape[1], 2) + STATE, F32)
    for l in range(DEPTH):
        kind, j = l % 3, l // 3
        if kind == 0:
            o_f, qd, kd, ket, v, r, dec, states = _gla_fwd(x, mods, l, j, gla_w_in, gla_w_ga, gla_w_gb,
                                                           gla_b_g, state_gla, states)
            x, states = _gla_bwd(x, o_f, qd, kd, ket, v, r, dec, state_gla, states, mods, l, j,
                                 gla_gn_g, gla_w_o, ln_g, ln_b)
        elif kind == 1:
            x = _conformer(x, mods, l, j, conf_w_pw1, conf_b_pw1, conf_w_dw, conf_b_dw, conf_ln_g,
                           conf_ln_b, conf_w_pw2, conf_b_pw2, ln_g, ln_b)
        else:
            x = _short_conv(x, mods, l, j, sc_w_in, sc_w_conv, sc_w_out, ln_g, ln_b)
        x = _mlp(x, mods, l, ff_w1, ff_w2, ln_g, ln_b, split_out=(l == DEPTH - 1))

    y_prompt, y_sample = x
    return (y_prompt.reshape(BATCH, SEQ, D), y_sample.reshape(DEC_BATCH, DEC_SEQ, D), states)
```

```python
import functools

import jax
import jax.numpy as jnp
from jax import lax
from jax.experimental import pallas as pl
from jax.experimental.pallas import tpu as pltpu

F32 = jnp.float32
BF16 = jnp.bfloat16

D = 1024
DEPTH = 4
BATCH, SEQ = 16, 256
DEC_BATCH, DEC_SEQ = 2, 2048
GRID_W = 64
N_MOD = 6
N_CVEC = 1 + DEC_BATCH
NCTX = BATCH * SEQ
NLAT = DEC_BATCH * DEC_SEQ
NTOK = NCTX + NLAT
H, DK, DV = 4, 128, 256
KW = H * DK
RANK = 16
CHUNK = 64
GATE_NORM = 16.0
CONF_W = 31
D_FF = 4 * D
LN_EPS = 1e-5
RMS_EPS = 1e-6
ALPHA = (2 * DEPTH) ** 0.25

TM = 256
NTILE = NTOK // TM
CTX_TILES = NCTX // TM
TILES_PER_LAT = DEC_SEQ // TM
N_SEQ = BATCH + DEC_BATCH
TM_MLP = 512
NT_MLP = NTOK // TM_MLP
TF_MLP = 1024
NJ_MLP = D_FF // TF_MLP
TC_MLP = 512
TN_MOD = 2048
SUBLANE = 8
VMEM_LIMIT = 58 * 1024 * 1024


def _cparams(*sem):
    return pltpu.CompilerParams(dimension_semantics=sem, vmem_limit_bytes=VMEM_LIMIT)


def _mod_rows(mod_ref, row0, ks):
    m = jnp.where(row0 < NCTX, 0, 1 + (row0 - NCTX) // DEC_SEQ)
    return [mod_ref[pl.ds(m, 1), k * D:(k + 1) * D] for k in ks]


def _mod_spec(l):
    return pl.BlockSpec((None, SUBLANE, N_MOD * D), lambda *_: (l, 0, 0))


def _stacked(block, j, single_buffer=False):
    nd = len(block)
    mode = dict(pipeline_mode=pl.Buffered(1)) if single_buffer else {}
    return pl.BlockSpec((None,) + block, lambda *_: (j,) + (0,) * nd, **mode)


def _x_specs(split, tile=lambda i: i):
    if not split:
        return [pl.BlockSpec((TM, D), lambda i: (tile(i), 0))]
    return [pl.BlockSpec((TM, D), lambda i: (jnp.clip(tile(i), 0, CTX_TILES - 1), 0)),
            pl.BlockSpec((TM, D), lambda i: (jnp.clip(tile(i) - CTX_TILES, 0, NTILE - CTX_TILES - 1), 0))]


def _x_args(x):
    return list(x) if isinstance(x, tuple) else [x]


def _read_x(x_refs, t):
    if len(x_refs) == 1:
        return x_refs[0][...]
    return jnp.where(t < CTX_TILES, x_refs[0][...], x_refs[1][...])


def _layer_norm(y, g, b):
    mu = jnp.mean(y, axis=-1, keepdims=True)
    yc = y - mu
    var = jnp.mean(yc * yc, axis=-1, keepdims=True)
    return yc * lax.rsqrt(var + LN_EPS) * g + b


def _silu(x):
    return x * jax.nn.sigmoid(x)


def _cast_rows(src_ref, dst_ref, step):
    n = src_ref.shape[0] // step

    def body(r, c):
        rows = pl.ds(pl.multiple_of(r * step, step), step)
        dst_ref[rows, :] = src_ref[rows, :].astype(BF16)
        return c

    lax.fori_loop(0, n, body, 0)


def _vec3(a):
    return a.reshape(-1, 1, a.shape[-1])


def _adaln_kernel(c_ref, w_ref, b_ref, o_ref):
    s = _silu(c_ref[...]).astype(BF16)
    o_ref[...] = jnp.dot(s, w_ref[...].astype(BF16), preferred_element_type=F32) + b_ref[...]


def _adaln(c8, mod_w, mod_b):
    return pl.pallas_call(
        _adaln_kernel,
        grid=(DEPTH, N_MOD * D // TN_MOD),
        in_specs=[
            pl.BlockSpec((SUBLANE, D), lambda l, n: (0, 0)),
            pl.BlockSpec((None, D, TN_MOD), lambda l, n: (l, 0, n)),
            pl.BlockSpec((None, 1, TN_MOD), lambda l, n: (l, 0, n)),
        ],
        out_specs=pl.BlockSpec((None, SUBLANE, TN_MOD), lambda l, n: (l, 0, n)),
        out_shape=jax.ShapeDtypeStruct((DEPTH, SUBLANE, N_MOD * D), F32),
        compiler_params=_cparams("parallel", "parallel"),
        name="adaln",
    )(c8, mod_w, _vec3(mod_b))


def _mlp_tile(s):
    return jnp.clip(s - (NJ_MLP - 1), 0, NT_MLP - 1)


def _mlp_done(s):
    return jnp.clip(s - NJ_MLP, 0, NT_MLP - 1)


def _sqrelu(a):
    return jnp.square(jnp.maximum(a, 0.0)).astype(BF16)


def _mlp_kernel(x_ref, xd_ref, mod_ref, w1_ref, w2_ref, g_ref, b_ref, *rest, split_out):
    n_out = 2 if split_out else 1
    o_refs, (w1s, w2s, h_scr, acc_scr), y_scr = rest[:n_out], rest[n_out:n_out + 4], rest[n_out + 4:]
    s = pl.program_id(0)
    shift, scale = _mod_rows(mod_ref, _mlp_tile(s) * TM_MLP, (3, 4))
    done = _mlp_done(s)

    def finish():
        (gate,) = _mod_rows(mod_ref, done * TM_MLP, (5,))
        y = _layer_norm(ALPHA * xd_ref[...] + gate * acc_scr[...], g_ref[...], b_ref[...])
        if split_out:
            y_scr[0][...] = y
        else:
            o_refs[0][...] = y

    def route():
        if split_out:
            @pl.when(done < NCTX // TM_MLP)
            def _():
                o_refs[0][...] = y_scr[0][...]

            @pl.when(done >= NCTX // TM_MLP)
            def _():
                o_refs[1][...] = y_scr[0][...]

    @pl.when(s < NJ_MLP)
    def _():
        w1s[s] = w1_ref[...].astype(BF16)
        w2s[s] = w2_ref[...].astype(BF16)

        @pl.when(s == 0)
        def _():
            h_scr[...] = (x_ref[...] * (1.0 + scale) + shift).astype(BF16)
            acc_scr[...] = jnp.zeros_like(acc_scr)

        a = _sqrelu(jnp.dot(h_scr[...], w1s[s], preferred_element_type=F32))
        acc_scr[...] += jnp.dot(a, w2s[s], preferred_element_type=F32)

    @pl.when(jnp.logical_and(s >= NJ_MLP, s < NJ_MLP + NT_MLP - 1))
    def _():
        finish()
        h_scr[...] = (x_ref[...] * (1.0 + scale) + shift).astype(BF16)
        chunks = [(j, slice(c * TC_MLP, (c + 1) * TC_MLP))
                  for j in range(NJ_MLP) for c in range(TF_MLP // TC_MLP)]
        up = lambda j, cols: _sqrelu(jnp.dot(h_scr[...], w1s[j, :, cols], preferred_element_type=F32))
        a_prev, acc = up(*chunks[0]), None
        for (j, cols), nxt in zip(chunks, chunks[1:] + [None]):
            a_next = up(*nxt) if nxt is not None else None
            part = jnp.dot(a_prev, w2s[j, cols, :], preferred_element_type=F32)
            acc = part if acc is None else acc + part
            a_prev = a_next
        acc_scr[...] = acc
        route()

    @pl.when(s == NJ_MLP + NT_MLP - 1)
    def _():
        finish()
        route()


def _mlp(x, mods, l, w1, w2, ln_g, ln_b, split_out):
    nc = NCTX // TM_MLP
    chunk = lambda s: jnp.minimum(s, NJ_MLP - 1)
    scratch = [pltpu.VMEM((NJ_MLP, D, TF_MLP), BF16), pltpu.VMEM((NJ_MLP, TF_MLP, D), BF16),
               pltpu.VMEM((TM_MLP, D), BF16), pltpu.VMEM((TM_MLP, D), F32)]
    if split_out:
        out_specs = [pl.BlockSpec((TM_MLP, D), lambda s: (jnp.minimum(_mlp_done(s), nc - 1), 0)),
                     pl.BlockSpec((TM_MLP, D), lambda s: (jnp.maximum(_mlp_done(s) - nc, 0), 0))]
        out_shape = [jax.ShapeDtypeStruct((NCTX, D), F32), jax.ShapeDtypeStruct((NLAT, D), F32)]
        scratch.append(pltpu.VMEM((TM_MLP, D), F32))
    else:
        out_specs = pl.BlockSpec((TM_MLP, D), lambda s: (_mlp_done(s), 0))
        out_shape = jax.ShapeDtypeStruct((NTOK, D), F32)
    return pl.pallas_call(
        functools.partial(_mlp_kernel, split_out=split_out),
        grid=(NJ_MLP + NT_MLP,),
        in_specs=[
            pl.BlockSpec((TM_MLP, D), lambda s: (_mlp_tile(s), 0)),
            pl.BlockSpec((TM_MLP, D), lambda s: (_mlp_done(s), 0)),
            _mod_spec(l),
            pl.BlockSpec((None, D, TF_MLP), lambda s: (l, 0, chunk(s))),
            pl.BlockSpec((None, TF_MLP, D), lambda s: (l, chunk(s), 0)),
            _stacked((1, D), 2 * l + 1), _stacked((1, D), 2 * l + 1),
        ],
        out_specs=out_specs,
        out_shape=out_shape,
        scratch_shapes=scratch,
        compiler_params=_cparams("arbitrary"),
        name="mlp",
    )(x, x, mods, w1, w2, _vec3(ln_g), _vec3(ln_b))


CONF_PAD = 16
CONF_RB = 64
CONF_LB = 256
CONF_ROWS = (TM // GRID_W) * (GRID_W + 2 * CONF_PAD)


def _conf_conv(upad_ref, shf_ref, w_taps, u, joined):
    nseg = TM // GRID_W
    stride = GRID_W + 2 * CONF_PAD
    zpad = jnp.zeros((CONF_PAD, CONF_LB), F32)
    for s in range(nseg):
        base, r0 = s * stride, s * GRID_W
        above = jnp.where(joined, u[r0 - CONF_PAD:r0, :], 0.0) if s > 0 else zpad
        below = jnp.where(joined, u[r0 + GRID_W:r0 + GRID_W + CONF_PAD, :], 0.0) if s < nseg - 1 else zpad
        upad_ref[base:base + CONF_PAD, :] = above
        upad_ref[base + CONF_PAD:base + CONF_PAD + GRID_W, :] = u[r0:r0 + GRID_W, :]
        upad_ref[base + CONF_PAD + GRID_W:base + stride, :] = below
    for b in range(1, SUBLANE):
        shf_ref[b - 1, 0:CONF_ROWS - SUBLANE, :] = upad_ref[b:b + CONF_ROWS - SUBLANE, :]
    out = []
    for s in range(nseg):
        r0 = s * stride + CONF_PAD
        acc = jnp.zeros((GRID_W, CONF_LB), F32)
        for k in range(CONF_W):
            a, b = divmod(k - CONF_W // 2, SUBLANE)
            rows = slice(r0 + SUBLANE * a, r0 + SUBLANE * a + GRID_W)
            acc = acc + w_taps[k] * (upad_ref[rows, :] if b == 0 else shf_ref[b - 1, rows, :])
        out.append(acc)
    return jnp.concatenate(out, axis=0)


def _conf_kernel(x_ref, mod_ref, w1_ref, b1_ref, wdw_ref, bdw_ref, cg_ref, cb_ref, w2_ref, b2_ref,
                 g_ref, b_ref, o_ref, w1s, w2s, upad, shf):
    i = pl.program_id(0)
    shift, scale, gate = _mod_rows(mod_ref, i * TM, (0, 1, 2))

    @pl.when(i == 0)
    def _():
        _cast_rows(w1_ref, w1s, 128)
        _cast_rows(w2_ref, w2s, 128)

    x = x_ref[...]
    h = (x * (1.0 + scale) + shift).astype(BF16)

    def glu(lb):
        cols = slice(lb * CONF_LB, (lb + 1) * CONF_LB)
        gcols = slice(D + lb * CONF_LB, D + (lb + 1) * CONF_LB)
        a = jnp.dot(h, w1s[:, cols], preferred_element_type=F32) + b1_ref[:, cols]
        g = jnp.dot(h, w1s[:, gcols], preferred_element_type=F32) + b1_ref[:, gcols]
        return a * jax.nn.sigmoid(g)

    n_lb = D // CONF_LB
    conv = []
    u = glu(0)
    for lb in range(n_lb):
        u_next = glu(lb + 1) if lb + 1 < n_lb else None
        cols = slice(lb * CONF_LB, (lb + 1) * CONF_LB)
        taps = [wdw_ref[k:k + 1, cols] for k in range(CONF_W)]
        conv.append(_conf_conv(upad, shf, taps, u, i < CTX_TILES))
        u = u_next

    uc = jnp.concatenate(conv, axis=1) + bdw_ref[...]
    uc = _silu(_layer_norm(uc, cg_ref[...], cb_ref[...]))
    y = jnp.dot(uc.astype(BF16), w2s[...], preferred_element_type=F32) + b2_ref[...]
    o_ref[...] = _layer_norm(ALPHA * x + gate * y, g_ref[...], b_ref[...])


def _conformer(x, mods, l, j, w1, b1, wdw, bdw, cg, cb, w2, b2, ln_g, ln_b):
    return pl.pallas_call(
        _conf_kernel,
        grid=(NTILE,),
        in_specs=[
            pl.BlockSpec((TM, D), lambda i: (i, 0)),
            _mod_spec(l),
            _stacked((D, 2 * D), j, True), _stacked((1, 2 * D), j),
            _stacked((CONF_W, D), j), _stacked((1, D), j), _stacked((1, D), j), _stacked((1, D), j),
            _stacked((D, D), j, True), _stacked((1, D), j),
            _stacked((1, D), 2 * l), _stacked((1, D), 2 * l),
        ],
        out_specs=pl.BlockSpec((TM, D), lambda i: (i, 0)),
        out_shape=jax.ShapeDtypeStruct((NTOK, D), F32),
        scratch_shapes=[pltpu.VMEM((D, 2 * D), BF16), pltpu.VMEM((D, D), BF16),
                        pltpu.VMEM((CONF_ROWS, CONF_LB), F32),
                        pltpu.VMEM((SUBLANE - 1, CONF_ROWS, CONF_LB), F32)],
        compiler_params=_cparams("arbitrary"),
        name="conformer",
    )(x, mods, w1, _vec3(b1), wdw, _vec3(bdw), _vec3(cg), _vec3(cb), w2, _vec3(b2),
      _vec3(ln_g), _vec3(ln_b))


def _sconv_kernel(x_ref, xp_ref, xn_ref, mod_ref, win_ref, wc_ref, wout_ref, g_ref, b_ref, o_ref,
                  wins, wouts, cpad):
    i = pl.program_id(0)
    shift, scale, gate = _mod_rows(mod_ref, i * TM, (0, 1, 2))

    @pl.when(i == 0)
    def _():
        _cast_rows(win_ref, wins, 128)
        _cast_rows(wout_ref, wouts, 128)

    scale = 1.0 + scale
    w0, w1, w2 = wc_ref[0:1, :], wc_ref[1:2, :], wc_ref[2:3, :]

    def finish(x, h, y):
        bg = jnp.dot(h, wins[:, :D], preferred_element_type=F32)
        out = jnp.dot((bg * y).astype(BF16), wouts[...], preferred_element_type=F32)
        o_ref[...] = _layer_norm(ALPHA * x + gate * out, g_ref[...], b_ref[...])

    @pl.when(i < CTX_TILES)
    def _():
        x = x_ref[...]
        h = (x * scale + shift).astype(BF16)
        cu = jnp.dot(h, wins[:, D:], preferred_element_type=F32)
        cu = cu[:, :D] * cu[:, D:]
        zrow = jnp.zeros((SUBLANE, D), F32)
        cpad[0:SUBLANE, :] = zrow
        cpad[SUBLANE:SUBLANE + TM, :] = cu
        cpad[SUBLANE + TM:2 * SUBLANE + TM, :] = zrow
        finish(x, h, w0 * cpad[SUBLANE - 1:SUBLANE - 1 + TM, :] + w1 * cu
               + w2 * cpad[SUBLANE + 1:SUBLANE + 1 + TM, :])

    @pl.when(i >= CTX_TILES)
    def _():
        r = (i - CTX_TILES) % TILES_PER_LAT
        x = x_ref[...]
        h = (x * scale + shift).astype(BF16)
        hp = (xp_ref[...] * scale + shift).astype(BF16)
        hn = (xn_ref[...] * scale + shift).astype(BF16)
        hcat = jnp.concatenate([hp, h, hn], axis=0)
        cu = jnp.dot(hcat, wins[:, D:], preferred_element_type=F32)
        cu = cu[:, :D] * cu[:, D:]
        halo_up = jnp.where(r > 0, cu[0:GRID_W, :], 0.0)
        halo_dn = jnp.where(r < TILES_PER_LAT - 1, cu[GRID_W + TM:, :], 0.0)
        up = jnp.concatenate([halo_up, cu[GRID_W:TM, :]], axis=0)
        dn = jnp.concatenate([cu[2 * GRID_W:GRID_W + TM, :], halo_dn], axis=0)
        finish(x, h, w0 * up + w1 * cu[GRID_W:GRID_W + TM, :] + w2 * dn)


def _short_conv(x, mods, l, j, w_in, w_conv, w_out, ln_g, ln_b):
    halo_per_tile = TM // GRID_W
    n_halo = NTOK // GRID_W
    return pl.pallas_call(
        _sconv_kernel,
        grid=(NTILE,),
        in_specs=[
            pl.BlockSpec((TM, D), lambda i: (i, 0)),
            pl.BlockSpec((GRID_W, D), lambda i: (jnp.maximum(i * halo_per_tile - 1, 0), 0)),
            pl.BlockSpec((GRID_W, D), lambda i: (jnp.minimum((i + 1) * halo_per_tile, n_halo - 1), 0)),
            _mod_spec(l),
            _stacked((D, 3 * D), j, True),
            _stacked((3, D), j),
            _stacked((D, D), j, True),
            _stacked((1, D), 2 * l), _stacked((1, D), 2 * l),
        ],
        out_specs=pl.BlockSpec((TM, D), lambda i: (i, 0)),
        out_shape=jax.ShapeDtypeStruct((NTOK, D), F32),
        scratch_shapes=[pltpu.VMEM((D, 3 * D), BF16), pltpu.VMEM((D, D), BF16),
                        pltpu.VMEM((TM + 2 * SUBLANE, D), F32)],
        compiler_params=_cparams("arbitrary"),
        name="short_conv",
    )(x, x, x, mods, w_in, w_conv, w_out, _vec3(ln_g), _vec3(ln_b))


N_CH = TM // CHUNK


STATE = (H, DK, DV)


def _state_out_spec(tile, j, d):
    return pl.BlockSpec((None, None, None) + STATE,
                        lambda i: (jnp.minimum(tile(i), CTX_TILES - 1), j, d, 0, 0, 0))


def _emit_state(st_ref, s_out, t):
    @pl.when(t < CTX_TILES)
    def _():
        for hd in range(H):
            st_ref[hd] = s_out[hd]


def _lat_seq(t):
    return jnp.clip((t - CTX_TILES) // TILES_PER_LAT, 0, DEC_BATCH - 1)


def _chunk_tri(rev):
    row = lax.broadcasted_iota(jnp.int32, (TM, TM), 0)
    col = lax.broadcasted_iota(jnp.int32, (TM, TM), 1)
    same = (row // CHUNK) == (col // CHUNK)
    return same & ((col >= row) if rev else (col <= row))


def _scan_state_in(s_scr, s0_ref, t, rev):
    first_r = TILES_PER_LAT - 1 if rev else 0
    fresh = jnp.logical_and(t >= CTX_TILES, (t - CTX_TILES) % TILES_PER_LAT == first_r)
    s = jnp.where(fresh, s0_ref[...], s_scr[...])
    return jnp.where(t < CTX_TILES, 0.0, s)


def _decay_operands(q, k, bcum, rev):
    qd, kd, ke, dec = [], [], [], []
    for c in range(N_CH):
        rows = slice(c * CHUNK, (c + 1) * CHUNK)
        b = bcum[rows, :]
        last = b[0:1, :] if rev else b[CHUNK - 1:CHUNK, :]
        qd.append((q[rows, :] * jnp.exp(b)).astype(BF16))
        kd.append((k[rows, :] * jnp.exp(-b)).astype(BF16))
        ke.append(k[rows, :] * jnp.exp(last - b))
        dec.append(jnp.exp(last))
    return (jnp.concatenate(qd, axis=0), jnp.concatenate(kd, axis=0),
            jnp.concatenate(ke, axis=0).T.astype(BF16), dec)


def _scan_tile(heads, s_in, rev):
    mask = _chunk_tri(rev)
    kcol = lax.broadcasted_iota(jnp.int32, (DK, TM), 1) // CHUNK
    order = range(N_CH - 1, -1, -1) if rev else range(N_CH)
    sc, kv = [], []
    for qh, kh, keth, vh, _ in heads:
        sc.append(lax.dot_general(qh, kh, (((1,), (1,)), ((), ())), preferred_element_type=F32))
        kst = jnp.concatenate([jnp.where(kcol == c, keth, jnp.zeros_like(keth)) for c in range(N_CH)],
                              axis=0)
        kv.append(jnp.dot(kst, vh, preferred_element_type=F32))
    o_intra = [jnp.dot(jnp.where(mask, s, 0.0).astype(BF16), hd[3], preferred_element_type=F32)
               for s, hd in zip(sc, heads)]
    outs, s_out = [], []
    for hd, (qh, _, _, _, dec_rows) in enumerate(heads):
        s, o = s_in[hd], [None] * N_CH
        for c in order:
            rows = slice(c * CHUNK, (c + 1) * CHUNK)
            o[c] = o_intra[hd][rows, :] + jnp.dot(qh[rows, :], s.astype(BF16),
                                                  preferred_element_type=F32)
            dec_col = jnp.broadcast_to(dec_rows[c], (DK, DK)).T
            s = s * jnp.concatenate([dec_col, dec_col], axis=1) + kv[hd][c * DK:(c + 1) * DK, :]
        outs.append(o)
        s_out.append(s)
    return outs, s_out


def _gla_fwd_kernel(*refs, n_x):
    x_refs = refs[:n_x]
    (mod_ref, win_ref, wga_ref, wgb_ref, bg_ref, s0_ref, _,
     of_ref, qd_ref, kd_ref, ket_ref, v_ref, r_ref, dec_ref, st_ref,
     wins, tri_scr, s_scr) = refs[n_x:]
    i = pl.program_id(0)
    shift, scale = _mod_rows(mod_ref, i * TM, (0, 1))

    @pl.when(i == 0)
    def _():
        _cast_rows(win_ref, wins, 128)
        tri_scr[0] = jnp.where(_chunk_tri(False), 1.0, 0.0).astype(BF16)
        tri_scr[1] = jnp.where(_chunk_tri(True), 1.0, 0.0).astype(BF16)
        s_scr[...] = jnp.zeros_like(s_scr)

    h = (_read_x(x_refs, i) * (1.0 + scale) + shift).astype(BF16)

    g_parts = []
    for d in range(2):
        za = jnp.dot(h, wga_ref[d].astype(BF16), preferred_element_type=F32)
        z = jnp.dot(za.astype(BF16), wgb_ref[d].astype(BF16), preferred_element_type=F32) + bg_ref[d]
        g = (jnp.minimum(z, 0.0) - jnp.log1p(jnp.exp(-jnp.abs(z)))) * (1.0 / GATE_NORM)
        g_hi = g.astype(BF16)
        g_parts.append((g_hi, (g - g_hi.astype(F32)).astype(BF16)))

    proj = jnp.dot(h, wins[...], preferred_element_type=F32)
    q = proj[:, :KW] * (DK ** -0.5)
    k = proj[:, KW:2 * KW]
    v = proj[:, 2 * KW:2 * KW + D].astype(BF16)
    v_ref[...] = v
    r_ref[...] = proj[:, 2 * KW + D:].astype(BF16)

    bcum = [jnp.dot(tri_scr[d], g_hi, preferred_element_type=F32)
            + jnp.dot(tri_scr[d], g_lo, preferred_element_type=F32)
            for d, (g_hi, g_lo) in enumerate(g_parts)]

    qd_b, kd_b, ket_b, dec_b = _decay_operands(q, k, bcum[1], True)
    qd_ref[...] = qd_b
    kd_ref[...] = kd_b
    ket_ref[...] = ket_b
    dec_ref[...] = jnp.concatenate(dec_b + [jnp.zeros((SUBLANE - N_CH, KW), F32)], axis=0)

    qd, kd, ket, dec = _decay_operands(q, k, bcum[0], False)
    heads = []
    for hd in range(H):
        ks, vs = slice(hd * DK, (hd + 1) * DK), slice(hd * DV, (hd + 1) * DV)
        heads.append((qd[:, ks], kd[:, ks], ket[ks, :], v[:, vs], [e[:, ks] for e in dec]))
    outs, s_out = _scan_tile(heads, _scan_state_in(s_scr, s0_ref, i, False), False)
    for hd in range(H):
        for c in range(N_CH):
            of_ref[c * CHUNK:(c + 1) * CHUNK, hd * DV:(hd + 1) * DV] = outs[hd][c].astype(BF16)
        s_scr[hd] = s_out[hd]
    _emit_state(st_ref, s_out, i)


def _gla_fwd(x, mods, l, j, w_in, w_ga, w_gb, b_g, state_gla, new_states):
    xs = _x_args(x)
    tile = pl.BlockSpec((TM, D), lambda i: (i, 0))
    keys = pl.BlockSpec((TM, KW), lambda i: (i, 0))
    n_in = len(xs) + 7
    return pl.pallas_call(
        functools.partial(_gla_fwd_kernel, n_x=len(xs)),
        grid=(NTILE,),
        in_specs=_x_specs(len(xs) == 2) + [
            _mod_spec(l),
            _stacked((D, 2 * KW + 2 * D), j, True),
            _stacked((2, D, RANK), j),
            _stacked((2, RANK, KW), j),
            _stacked((2, 1, KW), j),
            pl.BlockSpec((None, None, None) + STATE, lambda i: (_lat_seq(i), j, 0, 0, 0, 0)),
            pl.BlockSpec(memory_space=pl.ANY),
        ],
        out_specs=[
            tile, keys, keys,
            pl.BlockSpec((KW, TM), lambda i: (0, i)),
            tile, tile,
            pl.BlockSpec((None, SUBLANE, KW), lambda i: (i, 0, 0)),
            _state_out_spec(lambda i: i, j, 0),
        ],
        out_shape=[
            jax.ShapeDtypeStruct((NTOK, D), BF16),
            jax.ShapeDtypeStruct((NTOK, KW), BF16),
            jax.ShapeDtypeStruct((NTOK, KW), BF16),
            jax.ShapeDtypeStruct((KW, NTOK), BF16),
            jax.ShapeDtypeStruct((NTOK, D), BF16),
            jax.ShapeDtypeStruct((NTOK, D), BF16),
            jax.ShapeDtypeStruct((NTILE, SUBLANE, KW), F32),
            jax.ShapeDtypeStruct(new_states.shape, F32),
        ],
        input_output_aliases={n_in - 1: 7},
        scratch_shapes=[pltpu.VMEM((D, 2 * KW + 2 * D), BF16), pltpu.VMEM((2, TM, TM), BF16),
                        pltpu.VMEM(STATE, F32)],
        compiler_params=_cparams("arbitrary"),
        name="gla_fwd",
    )(*xs, mods, w_in, w_ga, w_gb, b_g.reshape(b_g.shape[0], 2, 1, KW), state_gla, new_states)


def _gla_bwd_kernel(*refs, n_x):
    x_refs = refs[:n_x]
    (qd_ref, kd_ref, ket_ref, v_ref, dec_ref, s0_ref, of_ref, r_ref, mod_ref, gn_ref, wo_ref,
     g_ref, b_ref, _, o_ref, st_ref, wos, s_scr) = refs[n_x:]
    i = pl.program_id(0)
    t = NTILE - 1 - i
    (gate,) = _mod_rows(mod_ref, t * TM, (2,))

    @pl.when(i == 0)
    def _():
        _cast_rows(wo_ref, wos, 128)
        s_scr[...] = jnp.zeros_like(s_scr)

    heads = []
    for hd in range(H):
        ks, vs = slice(hd * DK, (hd + 1) * DK), slice(hd * DV, (hd + 1) * DV)
        heads.append((qd_ref[:, ks], kd_ref[:, ks], ket_ref[ks, :], v_ref[:, vs],
                      [dec_ref[c:c + 1, ks] for c in range(N_CH)]))
    outs, s_out = _scan_tile(heads, _scan_state_in(s_scr, s0_ref, t, True), True)
    parts = []
    for hd in range(H):
        s_scr[hd] = s_out[hd]
        oh = of_ref[:, hd * DV:(hd + 1) * DV].astype(F32) + jnp.concatenate(outs[hd], axis=0)
        ms = jnp.mean(oh * oh, axis=-1, keepdims=True)
        parts.append(oh * lax.rsqrt(ms + RMS_EPS))

    on = jnp.concatenate(parts, axis=1) * gn_ref[...]
    y = jnp.dot((on * _silu(r_ref[...].astype(F32))).astype(BF16), wos[...], preferred_element_type=F32)
    o_ref[...] = _layer_norm(ALPHA * _read_x(x_refs, t) + gate * y, g_ref[...], b_ref[...])
    _emit_state(st_ref, s_out, t)


def _gla_bwd(x, o_f, qd, kd, ket, v, r, dec, state_gla, new_states, mods, l, j, gn_g, w_o, ln_g, ln_b):
    xs = _x_args(x)
    rtile = lambda i: NTILE - 1 - i
    tile = pl.BlockSpec((TM, D), lambda i: (rtile(i), 0))
    keys = pl.BlockSpec((TM, KW), lambda i: (rtile(i), 0))
    n_in = len(xs) + 14
    return pl.pallas_call(
        functools.partial(_gla_bwd_kernel, n_x=len(xs)),
        grid=(NTILE,),
        in_specs=_x_specs(len(xs) == 2, rtile) + [
            keys, keys,
            pl.BlockSpec((KW, TM), lambda i: (0, rtile(i))),
            tile,
            pl.BlockSpec((None, SUBLANE, KW), lambda i: (rtile(i), 0, 0)),
            pl.BlockSpec((None, None, None) + STATE, lambda i: (_lat_seq(rtile(i)), j, 1, 0, 0, 0)),
            tile, tile, _mod_spec(l),
            _stacked((1, D), j), _stacked((D, D), j, True),
            _stacked((1, D), 2 * l), _stacked((1, D), 2 * l),
            pl.BlockSpec(memory_space=pl.ANY)],
        out_specs=[tile, _state_out_spec(rtile, j, 1)],
        out_shape=[jax.ShapeDtypeStruct((NTOK, D), F32), jax.ShapeDtypeStruct(new_states.shape, F32)],
        input_output_aliases={n_in - 1: 1},
        scratch_shapes=[pltpu.VMEM((D, D), BF16), pltpu.VMEM(STATE, F32)],
        compiler_params=_cparams("arbitrary"),
        name="gla_bwd",
    )(*xs, qd, kd, ket, v, dec, state_gla, o_f, r, mods, _vec3(gn_g), w_o, _vec3(ln_g), _vec3(ln_b),
      new_states)


def kernel(x_prompt, x_sample, c, state_gla, c_ctx, mod_w, mod_b, ln_g, ln_b, ff_w1, ff_w2, gla_w_in, gla_w_ga, gla_w_gb, gla_b_g, gla_gn_g, gla_w_o, conf_w_pw1, conf_b_pw1, conf_w_dw, conf_b_dw, conf_ln_g, conf_ln_b, conf_w_pw2, conf_b_pw2, sc_w_in, sc_w_conv, sc_w_out):
    assert x_prompt.shape == (BATCH, SEQ, D) and x_sample.shape == (DEC_BATCH, DEC_SEQ, D)
    x = (x_prompt.reshape(NCTX, D), x_sample.reshape(NLAT, D))

    c8 = jnp.concatenate([c_ctx[None, :], c, jnp.zeros((SUBLANE - N_CVEC, D), F32)], axis=0)
    mods = _adaln(c8, mod_w, mod_b)

    states = jnp.zeros((BATCH, state_gla.shape[1], 2) + STATE, F32)
    for l in range(DEPTH):
        kind, j = l % 3, l // 3
        if kind == 0:
            o_f, qd, kd, ket, v, r, dec, states = _gla_fwd(x, mods, l, j, gla_w_in, gla_w_ga, gla_w_gb,
                                                           gla_b_g, state_gla, states)
            x, states = _gla_bwd(x, o_f, qd, kd, ket, v, r, dec, state_gla, states, mods, l, j,
                                 gla_gn_g, gla_w_o, ln_g, ln_b)
        elif kind == 1:
            x = _conformer(x, mods, l, j, conf_w_pw1, conf_b_pw1, conf_w_dw, conf_b_dw, conf_ln_g,
                           conf_ln_b, conf_w_pw2, conf_b_pw2, ln_g, ln_b)
        else:
            x = _short_conv(x, mods, l, j, sc_w_in, sc_w_conv, sc_w_out, ln_g, ln_b)
        x = _mlp(x, mods, l, ff_w1, ff_w2, ln_g, ln_b, split_out=(l == DEPTH - 1))

    y_prompt, y_sample = x
    return (y_prompt.reshape(BATCH, SEQ, D), y_sample.reshape(DEC_BATCH, DEC_SEQ, D), states)
```

```python
import dataclasses
import functools

import jax
import jax.numpy as jnp
from jax import lax
from jax.experimental import pallas as pl
from jax.experimental.pallas import tpu as pltpu

F32 = jnp.float32
BF16 = jnp.bfloat16

D = 1024
DEPTH = 4
BATCH, SEQ = 16, 256
DEC_BATCH, DEC_SEQ = 2, 2048
GRID_W = 64
N_MOD = 6
N_CVEC = 1 + DEC_BATCH
NCTX = BATCH * SEQ
NLAT = DEC_BATCH * DEC_SEQ
NTOK = NCTX + NLAT
H, DK, DV = 4, 128, 256
KW = H * DK
RANK = 16
CHUNK = 64
GATE_NORM = 16.0
CONF_W = 31
D_FF = 4 * D
LN_EPS = 1e-5
RMS_EPS = 1e-6
ALPHA = (2 * DEPTH) ** 0.25

TM = 256
NTILE = NTOK // TM
CTX_TILES = NCTX // TM
TILES_PER_LAT = DEC_SEQ // TM
TM_MLP = 512
TF_MLP = 1024
TC_MLP = 512
TN_MOD = 2048
SUBLANE = 8
VMEM_LIMIT = 58 * 1024 * 1024


def _cparams(*sem):
    return pltpu.CompilerParams(dimension_semantics=sem, vmem_limit_bytes=VMEM_LIMIT)


def _mod_rows(mod_ref, row0, ks):
    m = jnp.where(row0 < NCTX, 0, 1 + (row0 - NCTX) // DEC_SEQ)
    return [mod_ref[pl.ds(m, 1), k * D:(k + 1) * D] for k in ks]


def _mod_spec(l):
    return pl.BlockSpec((None, SUBLANE, N_MOD * D), lambda *_: (l, 0, 0))


def _stacked(block, j, single_buffer=False):
    nd = len(block)
    mode = dict(pipeline_mode=pl.Buffered(1)) if single_buffer else {}
    return pl.BlockSpec((None,) + block, lambda *_: (j,) + (0,) * nd, **mode)


def _x_specs(split, tile=lambda i: i):
    if not split:
        return [pl.BlockSpec((TM, D), lambda i: (tile(i), 0))]
    return [pl.BlockSpec((TM, D), lambda i: (jnp.clip(tile(i), 0, CTX_TILES - 1), 0)),
            pl.BlockSpec((TM, D), lambda i: (jnp.clip(tile(i) - CTX_TILES, 0, NTILE - CTX_TILES - 1), 0))]


def _x_args(x):
    return list(x) if isinstance(x, tuple) else [x]


def _read_x(x_refs, t):
    if len(x_refs) == 1:
        return x_refs[0][...]
    return jnp.where(t < CTX_TILES, x_refs[0][...], x_refs[1][...])


def _layer_norm(y, g, b):
    mu = jnp.mean(y, axis=-1, keepdims=True)
    yc = y - mu
    var = jnp.mean(yc * yc, axis=-1, keepdims=True)
    return yc * lax.rsqrt(var + LN_EPS) * g + b


def _silu(x):
    return x * jax.nn.sigmoid(x)


def _cast_rows(src_ref, dst_ref, step):
    n = src_ref.shape[0] // step

    def body(r, c):
        rows = pl.ds(pl.multiple_of(r * step, step), step)
        dst_ref[rows, :] = src_ref[rows, :].astype(BF16)
        return c

    lax.fori_loop(0, n, body, 0)


def _vec3(a):
    return a.reshape(-1, 1, a.shape[-1])


def _adaln_kernel(c_ref, w_ref, b_ref, o_ref):
    s = _silu(c_ref[...]).astype(BF16)
    o_ref[...] = jnp.dot(s, w_ref[...].astype(BF16), preferred_element_type=F32) + b_ref[...]


def _adaln(c8, mod_w, mod_b):
    return pl.pallas_call(
        _adaln_kernel,
        grid=(DEPTH, N_MOD * D // TN_MOD),
        in_specs=[
            pl.BlockSpec((SUBLANE, D), lambda l, n: (0, 0)),
            pl.BlockSpec((None, D, TN_MOD), lambda l, n: (l, 0, n)),
            pl.BlockSpec((None, 1, TN_MOD), lambda l, n: (l, 0, n)),
        ],
        out_specs=pl.BlockSpec((None, SUBLANE, TN_MOD), lambda l, n: (l, 0, n)),
        out_shape=jax.ShapeDtypeStruct((DEPTH, SUBLANE, N_MOD * D), F32),
        compiler_params=_cparams("parallel", "parallel"),
        name="adaln",
    )(c8, mod_w, _vec3(mod_b))


@dataclasses.dataclass(frozen=True)
class _MlpCfg:
    tm: int
    tf: int
    split_out: bool
    conformer: bool

    @property
    def nj(self):
        return D_FF // self.tf

    @property
    def nt(self):
        return NTOK // self.tm

    def tile(self, s):
        return jnp.clip(s - (self.nj - 1), 0, self.nt - 1)

    def done(self, s):
        return jnp.clip(s - self.nj, 0, self.nt - 1)


N_CONF_IN, N_CONF_SCRATCH = 11, 4


def _sqrelu(a):
    return jnp.square(jnp.maximum(a, 0.0)).astype(BF16)


def _mlp_kernel(*refs, cfg):
    it = iter(refs)
    take = lambda n: [next(it) for _ in range(n)]
    x_ref, xd_ref, mod_ref, w1_ref, w2_ref, g_ref, b_ref = take(7)
    conf_in = take(N_CONF_IN) if cfg.conformer else None
    o_refs = take(2 if cfg.split_out else 1)
    w1s, w2s, h_scr, acc_scr = take(4)
    y_scr = take(1)[0] if cfg.split_out else None
    conf_scr = take(N_CONF_SCRATCH) if cfg.conformer else None
    nj, nt, tm = cfg.nj, cfg.nt, cfg.tm

    s = pl.program_id(0)
    shift, scale = _mod_rows(mod_ref, cfg.tile(s) * tm, (3, 4))
    done = cfg.done(s)

    def finish(emit_matmuls=None):
        (gate,) = _mod_rows(mod_ref, done * tm, (5,))
        y = _layer_norm(ALPHA * xd_ref[...] + gate * acc_scr[...], g_ref[...], b_ref[...])
        if cfg.conformer:
            y = _conformer_tile(y, done, conf_in, conf_scr, emit_matmuls)
        if cfg.split_out:
            y_scr[...] = y
        else:
            o_refs[0][...] = y

    def route():
        if cfg.split_out:
            @pl.when(done < NCTX // tm)
            def _():
                o_refs[0][...] = y_scr[...]

            @pl.when(done >= NCTX // tm)
            def _():
                o_refs[1][...] = y_scr[...]

    @pl.when(s < nj)
    def _():
        w1s[s] = w1_ref[...].astype(BF16)
        w2s[s] = w2_ref[...].astype(BF16)
        if cfg.conformer:
            for src, dst in ((conf_in[1], conf_scr[0]), (conf_in[7], conf_scr[1])):
                rows = D // nj
                dst[pl.ds(pl.multiple_of(s * rows, rows), rows), :] = src[...].astype(BF16)

        @pl.when(s == 0)
        def _():
            h_scr[...] = (x_ref[...] * (1.0 + scale) + shift).astype(BF16)
            acc_scr[...] = jnp.zeros_like(acc_scr)

        a = _sqrelu(jnp.dot(h_scr[...], w1s[s], preferred_element_type=F32))
        acc_scr[...] += jnp.dot(a, w2s[s], preferred_element_type=F32)

    @pl.when(jnp.logical_and(s >= nj, s < nj + nt - 1))
    def _():
        h_scr[...] = (x_ref[...] * (1.0 + scale) + shift).astype(BF16)
        chunks = [(j, slice(c * TC_MLP, (c + 1) * TC_MLP))
                  for j in range(nj) for c in range(cfg.tf // TC_MLP)]
        queue = [("up", 0)]
        for k in range(len(chunks)):
            queue += ([("up", k + 1)] if k + 1 < len(chunks) else []) + [("down", k)]
        act, st = {}, dict(pos=0, acc=None)

        def emit_matmuls(n):
            for kind, k in queue[st["pos"]:st["pos"] + n]:
                j, cols = chunks[k]
                if kind == "up":
                    act[k] = _sqrelu(jnp.dot(h_scr[...], w1s[j, :, cols], preferred_element_type=F32))
                else:
                    part = jnp.dot(act.pop(k), w2s[j, cols, :], preferred_element_type=F32)
                    st["acc"] = part if st["acc"] is None else st["acc"] + part
            st["pos"] = min(st["pos"] + n, len(queue))

        emit_matmuls(2)
        finish(lambda: emit_matmuls(1))
        emit_matmuls(len(queue))
        acc_scr[...] = st["acc"]
        route()

    @pl.when(s == nj + nt - 1)
    def _():
        finish()
        route()


def _mlp(x, mods, l, w1, w2, ln_g, ln_b, split_out, conformer=None):
    cfg = _MlpCfg(tm=TM if conformer else TM_MLP, tf=TC_MLP if conformer else TF_MLP,
                  split_out=split_out, conformer=conformer is not None)
    tm, tf, nj = cfg.tm, cfg.tf, cfg.nj
    nc = NCTX // tm
    chunk = lambda s: jnp.minimum(s, nj - 1)
    in_specs = [
        pl.BlockSpec((tm, D), lambda s: (cfg.tile(s), 0)),
        pl.BlockSpec((tm, D), lambda s: (cfg.done(s), 0)),
        _mod_spec(l),
        pl.BlockSpec((None, D, tf), lambda s: (l, 0, chunk(s))),
        pl.BlockSpec((None, tf, D), lambda s: (l, chunk(s), 0)),
        _stacked((1, D), 2 * l + 1), _stacked((1, D), 2 * l + 1),
    ]
    args = [x, x, mods, w1, w2, _vec3(ln_g), _vec3(ln_b)]
    scratch = [pltpu.VMEM((nj, D, tf), BF16), pltpu.VMEM((nj, tf, D), BF16),
               pltpu.VMEM((tm, D), BF16), pltpu.VMEM((tm, D), F32)]
    if split_out:
        out_specs = [pl.BlockSpec((tm, D), lambda s: (jnp.minimum(cfg.done(s), nc - 1), 0)),
                     pl.BlockSpec((tm, D), lambda s: (jnp.maximum(cfg.done(s) - nc, 0), 0))]
        out_shape = [jax.ShapeDtypeStruct((NCTX, D), F32), jax.ShapeDtypeStruct((NLAT, D), F32)]
        scratch.append(pltpu.VMEM((tm, D), F32))
    else:
        out_specs = pl.BlockSpec((tm, D), lambda s: (cfg.done(s), 0))
        out_shape = jax.ShapeDtypeStruct((NTOK, D), F32)
    if conformer:
        j, cw1, cb1, wdw, bdw, cg, cb, cw2, cb2 = conformer
        slab = lambda cols: pl.BlockSpec((None, D // nj, cols), lambda s: (j, chunk(s), 0))
        in_specs += [_mod_spec(l + 1), slab(2 * D), _stacked((1, 2 * D), j), _stacked((CONF_W, D), j),
                     _stacked((1, D), j), _stacked((1, D), j), _stacked((1, D), j),
                     slab(D), _stacked((1, D), j),
                     _stacked((1, D), 2 * (l + 1)), _stacked((1, D), 2 * (l + 1))]
        args += [mods, cw1, _vec3(cb1), wdw, _vec3(bdw), _vec3(cg), _vec3(cb), cw2, _vec3(cb2),
                 _vec3(ln_g), _vec3(ln_b)]
        scratch += [pltpu.VMEM((D, 2 * D), BF16), pltpu.VMEM((D, D), BF16),
                    pltpu.VMEM((CONF_ROWS, CONF_LB), F32),
                    pltpu.VMEM((SUBLANE - 1, CONF_ROWS, CONF_LB), F32)]
        assert len(in_specs) == 7 + N_CONF_IN
    return pl.pallas_call(
        functools.partial(_mlp_kernel, cfg=cfg),
        grid=(nj + cfg.nt,),
        in_specs=in_specs,
        out_specs=out_specs,
        out_shape=out_shape,
        scratch_shapes=scratch,
        compiler_params=_cparams("arbitrary"),
        name="mlp_conformer" if conformer else "mlp",
    )(*args)


CONF_PAD = 16
CONF_LB = 256
CONF_ROWS = (TM // GRID_W) * (GRID_W + 2 * CONF_PAD)


def _conf_conv(upad_ref, shf_ref, w_taps, u, joined, between=None):
    nseg = TM // GRID_W
    stride = GRID_W + 2 * CONF_PAD
    zpad = jnp.zeros((CONF_PAD, CONF_LB), F32)
    for s in range(nseg):
        base, r0 = s * stride, s * GRID_W
        above = jnp.where(joined, u[r0 - CONF_PAD:r0, :], 0.0) if s > 0 else zpad
        below = jnp.where(joined, u[r0 + GRID_W:r0 + GRID_W + CONF_PAD, :], 0.0) if s < nseg - 1 else zpad
        upad_ref[base:base + CONF_PAD, :] = above
        upad_ref[base + CONF_PAD:base + CONF_PAD + GRID_W, :] = u[r0:r0 + GRID_W, :]
        upad_ref[base + CONF_PAD + GRID_W:base + stride, :] = below
    for b in range(1, SUBLANE):
        shf_ref[b - 1, 0:CONF_ROWS - SUBLANE, :] = upad_ref[b:b + CONF_ROWS - SUBLANE, :]
    out = []
    for s in range(nseg):
        if between is not None:
            between()
        r0 = s * stride + CONF_PAD
        acc = jnp.zeros((GRID_W, CONF_LB), F32)
        for k in range(CONF_W):
            a, b = divmod(k - CONF_W // 2, SUBLANE)
            rows = slice(r0 + SUBLANE * a, r0 + SUBLANE * a + GRID_W)
            acc = acc + w_taps[k] * (upad_ref[rows, :] if b == 0 else shf_ref[b - 1, rows, :])
        out.append(acc)
    return jnp.concatenate(out, axis=0)


def _conformer_tile(x, t, conf_in, conf_scr, between=None):
    mod_ref, _, b1_ref, wdw_ref, bdw_ref, cg_ref, cb_ref, _, b2_ref, g_ref, b_ref = conf_in
    w1s, w2s, upad, shf = conf_scr
    shift, scale, gate = _mod_rows(mod_ref, t * TM, (0, 1, 2))
    h = (x * (1.0 + scale) + shift).astype(BF16)

    def glu(lb):
        cols = slice(lb * CONF_LB, (lb + 1) * CONF_LB)
        gcols = slice(D + lb * CONF_LB, D + (lb + 1) * CONF_LB)
        a = jnp.dot(h, w1s[:, cols], preferred_element_type=F32) + b1_ref[:, cols]
        g = jnp.dot(h, w1s[:, gcols], preferred_element_type=F32) + b1_ref[:, gcols]
        return a * jax.nn.sigmoid(g)

    n_lb = D // CONF_LB
    conv = []
    u = glu(0)
    for lb in range(n_lb):
        u_next = glu(lb + 1) if lb + 1 < n_lb else None
        cols = slice(lb * CONF_LB, (lb + 1) * CONF_LB)
        taps = [wdw_ref[k:k + 1, cols] for k in range(CONF_W)]
        conv.append(_conf_conv(upad, shf, taps, u, t < CTX_TILES, between))
        u = u_next

    hook = between if between is not None else (lambda: None)
    hook()
    uc = jnp.concatenate(conv, axis=1) + bdw_ref[...]
    uc = _silu(_layer_norm(uc, cg_ref[...], cb_ref[...]))
    y = jnp.dot(uc.astype(BF16), w2s[...], preferred_element_type=F32) + b2_ref[...]
    hook()
    return _layer_norm(ALPHA * x + gate * y, g_ref[...], b_ref[...])


def _sconv_kernel(x_ref, xp_ref, xn_ref, mod_ref, win_ref, wc_ref, wout_ref, g_ref, b_ref, o_ref,
                  wins, wouts, cpad):
    i = pl.program_id(0)
    shift, scale, gate = _mod_rows(mod_ref, i * TM, (0, 1, 2))

    @pl.when(i == 0)
    def _():
        _cast_rows(win_ref, wins, 128)
        _cast_rows(wout_ref, wouts, 128)

    scale = 1.0 + scale
    w0, w1, w2 = wc_ref[0:1, :], wc_ref[1:2, :], wc_ref[2:3, :]

    def finish(x, h, y):
        bg = jnp.dot(h, wins[:, :D], preferred_element_type=F32)
        out = jnp.dot((bg * y).astype(BF16), wouts[...], preferred_element_type=F32)
        o_ref[...] = _layer_norm(ALPHA * x + gate * out, g_ref[...], b_ref[...])

    @pl.when(i < CTX_TILES)
    def _():
        x = x_ref[...]
        h = (x * scale + shift).astype(BF16)
        cu = jnp.dot(h, wins[:, D:], preferred_element_type=F32)
        cu = cu[:, :D] * cu[:, D:]
        zrow = jnp.zeros((SUBLANE, D), F32)
        cpad[0:SUBLANE, :] = zrow
        cpad[SUBLANE:SUBLANE + TM, :] = cu
        cpad[SUBLANE + TM:2 * SUBLANE + TM, :] = zrow
        finish(x, h, w0 * cpad[SUBLANE - 1:SUBLANE - 1 + TM, :] + w1 * cu
               + w2 * cpad[SUBLANE + 1:SUBLANE + 1 + TM, :])

    @pl.when(i >= CTX_TILES)
    def _():
        r = (i - CTX_TILES) % TILES_PER_LAT
        x = x_ref[...]
        h = (x * scale + shift).astype(BF16)
        hp = (xp_ref[...] * scale + shift).astype(BF16)
        hn = (xn_ref[...] * scale + shift).astype(BF16)
        hcat = jnp.concatenate([hp, h, hn], axis=0)
        cu = jnp.dot(hcat, wins[:, D:], preferred_element_type=F32)
        cu = cu[:, :D] * cu[:, D:]
        halo_up = jnp.where(r > 0, cu[0:GRID_W, :], 0.0)
        halo_dn = jnp.where(r < TILES_PER_LAT - 1, cu[GRID_W + TM:, :], 0.0)
        up = jnp.concatenate([halo_up, cu[GRID_W:TM, :]], axis=0)
        dn = jnp.concatenate([cu[2 * GRID_W:GRID_W + TM, :], halo_dn], axis=0)
        finish(x, h, w0 * up + w1 * cu[GRID_W:GRID_W + TM, :] + w2 * dn)


def _short_conv(x, mods, l, j, w_in, w_conv, w_out, ln_g, ln_b):
    halo_per_tile = TM // GRID_W
    n_halo = NTOK // GRID_W
    return pl.pallas_call(
        _sconv_kernel,
        grid=(NTILE,),
        in_specs=[
            pl.BlockSpec((TM, D), lambda i: (i, 0)),
            pl.BlockSpec((GRID_W, D), lambda i: (jnp.maximum(i * halo_per_tile - 1, 0), 0)),
            pl.BlockSpec((GRID_W, D), lambda i: (jnp.minimum((i + 1) * halo_per_tile, n_halo - 1), 0)),
            _mod_spec(l),
            _stacked((D, 3 * D), j, True),
            _stacked((3, D), j),
            _stacked((D, D), j, True),
            _stacked((1, D), 2 * l), _stacked((1, D), 2 * l),
        ],
        out_specs=pl.BlockSpec((TM, D), lambda i: (i, 0)),
        out_shape=jax.ShapeDtypeStruct((NTOK, D), F32),
        scratch_shapes=[pltpu.VMEM((D, 3 * D), BF16), pltpu.VMEM((D, D), BF16),
                        pltpu.VMEM((TM + 2 * SUBLANE, D), F32)],
        compiler_params=_cparams("arbitrary"),
        name="short_conv",
    )(x, x, x, mods, w_in, w_conv, w_out, _vec3(ln_g), _vec3(ln_b))


N_CH = TM // CHUNK


STATE = (H, DK, DV)


def _state_out_spec(tile, j, d):
    return pl.BlockSpec((None, None, None) + STATE,
                        lambda i: (jnp.minimum(tile(i), CTX_TILES - 1), j, d, 0, 0, 0))


def _emit_state(st_ref, s_out, t):
    @pl.when(t < CTX_TILES)
    def _():
        for hd in range(H):
            st_ref[hd] = s_out[hd]


def _lat_seq(t):
    return jnp.clip((t - CTX_TILES) // TILES_PER_LAT, 0, DEC_BATCH - 1)


def _chunk_tri(rev):
    row = lax.broadcasted_iota(jnp.int32, (TM, TM), 0)
    col = lax.broadcasted_iota(jnp.int32, (TM, TM), 1)
    same = (row // CHUNK) == (col // CHUNK)
    return same & ((col >= row) if rev else (col <= row))


def _scan_state_in(s_scr, s0_ref, t, rev):
    first_r = TILES_PER_LAT - 1 if rev else 0
    fresh = jnp.logical_and(t >= CTX_TILES, (t - CTX_TILES) % TILES_PER_LAT == first_r)
    s = jnp.where(fresh, s0_ref[...], s_scr[...])
    return jnp.where(t < CTX_TILES, 0.0, s)


def _decay_operands(q, k, bcum, rev):
    qd, kd, ke, dec = [], [], [], []
    for c in range(N_CH):
        rows = slice(c * CHUNK, (c + 1) * CHUNK)
        b = bcum[rows, :]
        last = b[0:1, :] if rev else b[CHUNK - 1:CHUNK, :]
        qd.append((q[rows, :] * jnp.exp(b)).astype(BF16))
        kd.append((k[rows, :] * jnp.exp(-b)).astype(BF16))
        ke.append(k[rows, :] * jnp.exp(last - b))
        dec.append(jnp.exp(last))
    return (jnp.concatenate(qd, axis=0), jnp.concatenate(kd, axis=0),
            jnp.concatenate(ke, axis=0).T.astype(BF16), dec)


def _scan_tile(heads, s_in, rev):
    mask = _chunk_tri(rev)
    kcol = lax.broadcasted_iota(jnp.int32, (DK, TM), 1) // CHUNK
    order = range(N_CH - 1, -1, -1) if rev else range(N_CH)
    sc, kv = [], []
    for qh, kh, keth, vh, _ in heads:
        sc.append(lax.dot_general(qh, kh, (((1,), (1,)), ((), ())), preferred_element_type=F32))
        kst = jnp.concatenate([jnp.where(kcol == c, keth, jnp.zeros_like(keth)) for c in range(N_CH)],
                              axis=0)
        kv.append(jnp.dot(kst, vh, preferred_element_type=F32))
    o_intra = [jnp.dot(jnp.where(mask, s, 0.0).astype(BF16), hd[3], preferred_element_type=F32)
               for s, hd in zip(sc, heads)]
    outs, s_out = [], []
    for hd, (qh, _, _, _, dec_rows) in enumerate(heads):
        s, o = s_in[hd], [None] * N_CH
        for c in order:
            rows = slice(c * CHUNK, (c + 1) * CHUNK)
            o[c] = o_intra[hd][rows, :] + jnp.dot(qh[rows, :], s.astype(BF16),
                                                  preferred_element_type=F32)
            dec_col = jnp.broadcast_to(dec_rows[c], (DK, DK)).T
            s = s * jnp.concatenate([dec_col, dec_col], axis=1) + kv[hd][c * DK:(c + 1) * DK, :]
        outs.append(o)
        s_out.append(s)
    return outs, s_out


def _gla_fwd_kernel(*refs, n_x):
    x_refs = refs[:n_x]
    (mod_ref, win_ref, wga_ref, wgb_ref, bg_ref, s0_ref, _,
     of_ref, qd_ref, kd_ref, ket_ref, v_ref, r_ref, dec_ref, st_ref,
     wins, tri_scr, s_scr) = refs[n_x:]
    i = pl.program_id(0)
    shift, scale = _mod_rows(mod_ref, i * TM, (0, 1))

    @pl.when(i == 0)
    def _():
        _cast_rows(win_ref, wins, 128)
        tri_scr[0] = jnp.where(_chunk_tri(False), 1.0, 0.0).astype(BF16)
        tri_scr[1] = jnp.where(_chunk_tri(True), 1.0, 0.0).astype(BF16)
        s_scr[...] = jnp.zeros_like(s_scr)

    h = (_read_x(x_refs, i) * (1.0 + scale) + shift).astype(BF16)

    g_parts = []
    for d in range(2):
        za = jnp.dot(h, wga_ref[d].astype(BF16), preferred_element_type=F32)
        z = jnp.dot(za.astype(BF16), wgb_ref[d].astype(BF16), preferred_element_type=F32) + bg_ref[d]
        g = (jnp.minimum(z, 0.0) - jnp.log1p(jnp.exp(-jnp.abs(z)))) * (1.0 / GATE_NORM)
        g_hi = g.astype(BF16)
        g_parts.append((g_hi, (g - g_hi.astype(F32)).astype(BF16)))

    proj = jnp.dot(h, wins[...], preferred_element_type=F32)
    q = proj[:, :KW] * (DK ** -0.5)
    k = proj[:, KW:2 * KW]
    v = proj[:, 2 * KW:2 * KW + D].astype(BF16)
    v_ref[...] = v
    r_ref[...] = proj[:, 2 * KW + D:].astype(BF16)

    bcum = [jnp.dot(tri_scr[d], g_hi, preferred_element_type=F32)
            + jnp.dot(tri_scr[d], g_lo, preferred_element_type=F32)
            for d, (g_hi, g_lo) in enumerate(g_parts)]

    qd_b, kd_b, ket_b, dec_b = _decay_operands(q, k, bcum[1], True)
    qd_ref[...] = qd_b
    kd_ref[...] = kd_b
    ket_ref[...] = ket_b
    dec_ref[...] = jnp.concatenate(dec_b + [jnp.zeros((SUBLANE - N_CH, KW), F32)], axis=0)

    qd, kd, ket, dec = _decay_operands(q, k, bcum[0], False)
    heads = []
    for hd in range(H):
        ks, vs = slice(hd * DK, (hd + 1) * DK), slice(hd * DV, (hd + 1) * DV)
        heads.append((qd[:, ks], kd[:, ks], ket[ks, :], v[:, vs], [e[:, ks] for e in dec]))
    outs, s_out = _scan_tile(heads, _scan_state_in(s_scr, s0_ref, i, False), False)
    for hd in range(H):
        for c in range(N_CH):
            of_ref[c * CHUNK:(c + 1) * CHUNK, hd * DV:(hd + 1) * DV] = outs[hd][c].astype(BF16)
        s_scr[hd] = s_out[hd]
    _emit_state(st_ref, s_out, i)


def _gla_fwd(x, mods, l, j, w_in, w_ga, w_gb, b_g, state_gla, new_states):
    xs = _x_args(x)
    tile = pl.BlockSpec((TM, D), lambda i: (i, 0))
    keys = pl.BlockSpec((TM, KW), lambda i: (i, 0))
    n_in = len(xs) + 7
    return pl.pallas_call(
        functools.partial(_gla_fwd_kernel, n_x=len(xs)),
        grid=(NTILE,),
        in_specs=_x_specs(len(xs) == 2) + [
            _mod_spec(l),
            _stacked((D, 2 * KW + 2 * D), j, True),
            _stacked((2, D, RANK), j),
            _stacked((2, RANK, KW), j),
            _stacked((2, 1, KW), j),
            pl.BlockSpec((None, None, None) + STATE, lambda i: (_lat_seq(i), j, 0, 0, 0, 0)),
            pl.BlockSpec(memory_space=pl.ANY),
        ],
        out_specs=[
            tile, keys, keys,
            pl.BlockSpec((KW, TM), lambda i: (0, i)),
            tile, tile,
            pl.BlockSpec((None, SUBLANE, KW), lambda i: (i, 0, 0)),
            _state_out_spec(lambda i: i, j, 0),
        ],
        out_shape=[
            jax.ShapeDtypeStruct((NTOK, D), BF16),
            jax.ShapeDtypeStruct((NTOK, KW), BF16),
            jax.ShapeDtypeStruct((NTOK, KW), BF16),
            jax.ShapeDtypeStruct((KW, NTOK), BF16),
            jax.ShapeDtypeStruct((NTOK, D), BF16),
            jax.ShapeDtypeStruct((NTOK, D), BF16),
            jax.ShapeDtypeStruct((NTILE, SUBLANE, KW), F32),
            jax.ShapeDtypeStruct(new_states.shape, F32),
        ],
        input_output_aliases={n_in - 1: 7},
        scratch_shapes=[pltpu.VMEM((D, 2 * KW + 2 * D), BF16), pltpu.VMEM((2, TM, TM), BF16),
                        pltpu.VMEM(STATE, F32)],
        compiler_params=_cparams("arbitrary"),
        name="gla_fwd",
    )(*xs, mods, w_in, w_ga, w_gb, b_g.reshape(b_g.shape[0], 2, 1, KW), state_gla, new_states)


def _gla_bwd_kernel(*refs, n_x):
    x_refs = refs[:n_x]
    (qd_ref, kd_ref, ket_ref, v_ref, dec_ref, s0_ref, of_ref, r_ref, mod_ref, gn_ref, wo_ref,
     g_ref, b_ref, _, o_ref, st_ref, wos, s_scr) = refs[n_x:]
    i = pl.program_id(0)
    t = NTILE - 1 - i
    (gate,) = _mod_rows(mod_ref, t * TM, (2,))

    @pl.when(i == 0)
    def _():
        _cast_rows(wo_ref, wos, 128)
        s_scr[...] = jnp.zeros_like(s_scr)

    heads = []
    for hd in range(H):
        ks, vs = slice(hd * DK, (hd + 1) * DK), slice(hd * DV, (hd + 1) * DV)
        heads.append((qd_ref[:, ks], kd_ref[:, ks], ket_ref[ks, :], v_ref[:, vs],
                      [dec_ref[c:c + 1, ks] for c in range(N_CH)]))
    outs, s_out = _scan_tile(heads, _scan_state_in(s_scr, s0_ref, t, True), True)
    parts = []
    for hd in range(H):
        s_scr[hd] = s_out[hd]
        oh = of_ref[:, hd * DV:(hd + 1) * DV].astype(F32) + jnp.concatenate(outs[hd], axis=0)
        ms = jnp.mean(oh * oh, axis=-1, keepdims=True)
        parts.append(oh * lax.rsqrt(ms + RMS_EPS))

    on = jnp.concatenate(parts, axis=1) * gn_ref[...]
    y = jnp.dot((on * _silu(r_ref[...].astype(F32))).astype(BF16), wos[...], preferred_element_type=F32)
    o_ref[...] = _layer_norm(ALPHA * _read_x(x_refs, t) + gate * y, g_ref[...], b_ref[...])
    _emit_state(st_ref, s_out, t)


def _gla_bwd(x, o_f, qd, kd, ket, v, r, dec, state_gla, new_states, mods, l, j, gn_g, w_o, ln_g, ln_b):
    xs = _x_args(x)
    rtile = lambda i: NTILE - 1 - i
    tile = pl.BlockSpec((TM, D), lambda i: (rtile(i), 0))
    keys = pl.BlockSpec((TM, KW), lambda i: (rtile(i), 0))
    n_in = len(xs) + 14
    return pl.pallas_call(
        functools.partial(_gla_bwd_kernel, n_x=len(xs)),
        grid=(NTILE,),
        in_specs=_x_specs(len(xs) == 2, rtile) + [
            keys, keys,
            pl.BlockSpec((KW, TM), lambda i: (0, rtile(i))),
            tile,
            pl.BlockSpec((None, SUBLANE, KW), lambda i: (rtile(i), 0, 0)),
            pl.BlockSpec((None, None, None) + STATE, lambda i: (_lat_seq(rtile(i)), j, 1, 0, 0, 0)),
            tile, tile, _mod_spec(l),
            _stacked((1, D), j), _stacked((D, D), j, True),
            _stacked((1, D), 2 * l), _stacked((1, D), 2 * l),
            pl.BlockSpec(memory_space=pl.ANY)],
        out_specs=[tile, _state_out_spec(rtile, j, 1)],
        out_shape=[jax.ShapeDtypeStruct((NTOK, D), F32), jax.ShapeDtypeStruct(new_states.shape, F32)],
        input_output_aliases={n_in - 1: 1},
        scratch_shapes=[pltpu.VMEM((D, D), BF16), pltpu.VMEM(STATE, F32)],
        compiler_params=_cparams("arbitrary"),
        name="gla_bwd",
    )(*xs, qd, kd, ket, v, dec, state_gla, o_f, r, mods, _vec3(gn_g), w_o, _vec3(ln_g), _vec3(ln_b),
      new_states)


def kernel(x_prompt, x_sample, c, state_gla, c_ctx, mod_w, mod_b, ln_g, ln_b, ff_w1, ff_w2, gla_w_in, gla_w_ga, gla_w_gb, gla_b_g, gla_gn_g, gla_w_o, conf_w_pw1, conf_b_pw1, conf_w_dw, conf_b_dw, conf_ln_g, conf_ln_b, conf_w_pw2, conf_b_pw2, sc_w_in, sc_w_conv, sc_w_out):
    assert x_prompt.shape == (BATCH, SEQ, D) and x_sample.shape == (DEC_BATCH, DEC_SEQ, D)
    x = (x_prompt.reshape(NCTX, D), x_sample.reshape(NLAT, D))

    c8 = jnp.concatenate([c_ctx[None, :], c, jnp.zeros((SUBLANE - N_CVEC, D), F32)], axis=0)
    mods = _adaln(c8, mod_w, mod_b)

    states = jnp.zeros((BATCH, state_gla.shape[1], 2) + STATE, F32)
    for l in range(DEPTH):
        kind, j = l % 3, l // 3
        if kind == 0:
            o_f, qd, kd, ket, v, r, dec, states = _gla_fwd(x, mods, l, j, gla_w_in, gla_w_ga, gla_w_gb,
                                                           gla_b_g, state_gla, states)
            x, states = _gla_bwd(x, o_f, qd, kd, ket, v, r, dec, state_gla, states, mods, l, j,
                                 gla_gn_g, gla_w_o, ln_g, ln_b)
        elif kind == 2:
            x = _short_conv(x, mods, l, j, sc_w_in, sc_w_conv, sc_w_out, ln_g, ln_b)
        conformer = None
        if l + 1 < DEPTH and (l + 1) % 3 == 1:
            conformer = ((l + 1) // 3, conf_w_pw1, conf_b_pw1, conf_w_dw, conf_b_dw, conf_ln_g, conf_ln_b,
                         conf_w_pw2, conf_b_pw2)
        x = _mlp(x, mods, l, ff_w1, ff_w2, ln_g, ln_b, split_out=(l == DEPTH - 1), conformer=conformer)

    y_prompt, y_sample = x
    return (y_prompt.reshape(BATCH, SEQ, D), y_sample.reshape(DEC_BATCH, DEC_SEQ, D), states)
```

```python
import dataclasses
import functools

import jax
import jax.numpy as jnp
from jax import lax
from jax.experimental import pallas as pl
from jax.experimental.pallas import tpu as pltpu

F32 = jnp.float32
BF16 = jnp.bfloat16

D = 1024
DEPTH = 4
BATCH, SEQ = 16, 256
DEC_BATCH, DEC_SEQ = 2, 2048
GRID_W = 64
N_MOD = 6
N_CVEC = 1 + DEC_BATCH
NCTX = BATCH * SEQ
NLAT = DEC_BATCH * DEC_SEQ
NTOK = NCTX + NLAT
H, DK, DV = 4, 128, 256
KW = H * DK
RANK = 16
CHUNK = 64
GATE_NORM = 16.0
CONF_W = 31
D_FF = 4 * D
LN_EPS = 1e-5
RMS_EPS = 1e-6
ALPHA = (2 * DEPTH) ** 0.25

TM = 256
NTILE = NTOK // TM
CTX_TILES = NCTX // TM
TILES_PER_LAT = DEC_SEQ // TM
TM_MLP = 512
TF_MLP = 1024
TC_MLP = 512
TN_MOD = 2048
SUBLANE = 8
VMEM_LIMIT = 58 * 1024 * 1024


def _cparams(*sem):
    return pltpu.CompilerParams(dimension_semantics=sem, vmem_limit_bytes=VMEM_LIMIT)


def _mod_rows(mod_ref, row0, ks):
    m = jnp.where(row0 < NCTX, 0, 1 + (row0 - NCTX) // DEC_SEQ)
    return [mod_ref[pl.ds(m, 1), k * D:(k + 1) * D] for k in ks]


def _mod_spec(l):
    return pl.BlockSpec((None, SUBLANE, N_MOD * D), lambda *_: (l, 0, 0))


def _stacked(block, j, single_buffer=False):
    nd = len(block)
    mode = dict(pipeline_mode=pl.Buffered(1)) if single_buffer else {}
    return pl.BlockSpec((None,) + block, lambda *_: (j,) + (0,) * nd, **mode)


def _x_specs(split, tile=lambda i: i):
    if not split:
        return [pl.BlockSpec((TM, D), lambda i: (tile(i), 0))]
    return [pl.BlockSpec((TM, D), lambda i: (jnp.clip(tile(i), 0, CTX_TILES - 1), 0)),
            pl.BlockSpec((TM, D), lambda i: (jnp.clip(tile(i) - CTX_TILES, 0, NTILE - CTX_TILES - 1), 0))]


def _x_args(x):
    return list(x) if isinstance(x, tuple) else [x]


def _read_x(x_refs, t):
    if len(x_refs) == 1:
        return x_refs[0][...]
    return jnp.where(t < CTX_TILES, x_refs[0][...], x_refs[1][...])


def _layer_norm(y, g, b):
    mu = jnp.mean(y, axis=-1, keepdims=True)
    yc = y - mu
    var = jnp.mean(yc * yc, axis=-1, keepdims=True)
    return yc * lax.rsqrt(var + LN_EPS) * g + b


def _silu(x):
    return x * jax.nn.sigmoid(x)


def _cast_rows(src_ref, dst_ref, step):
    n = src_ref.shape[0] // step

    def body(r, c):
        rows = pl.ds(pl.multiple_of(r * step, step), step)
        dst_ref[rows, :] = src_ref[rows, :].astype(BF16)
        return c

    lax.fori_loop(0, n, body, 0)


def _vec3(a):
    return a.reshape(-1, 1, a.shape[-1])


def _adaln_kernel(c_ref, w_ref, b_ref, o_ref):
    s = _silu(c_ref[...]).astype(BF16)
    o_ref[...] = jnp.dot(s, w_ref[...].astype(BF16), preferred_element_type=F32) + b_ref[...]


def _adaln(c8, mod_w, mod_b):
    return pl.pallas_call(
        _adaln_kernel,
        grid=(DEPTH, N_MOD * D // TN_MOD),
        in_specs=[
            pl.BlockSpec((SUBLANE, D), lambda l, n: (0, 0)),
            pl.BlockSpec((None, D, TN_MOD), lambda l, n: (l, 0, n)),
            pl.BlockSpec((None, 1, TN_MOD), lambda l, n: (l, 0, n)),
        ],
        out_specs=pl.BlockSpec((None, SUBLANE, TN_MOD), lambda l, n: (l, 0, n)),
        out_shape=jax.ShapeDtypeStruct((DEPTH, SUBLANE, N_MOD * D), F32),
        compiler_params=_cparams("parallel", "parallel"),
        name="adaln",
    )(c8, mod_w, _vec3(mod_b))


@dataclasses.dataclass(frozen=True)
class _MlpCfg:
    tm: int
    tf: int
    split_out: bool
    conformer: bool

    @property
    def nj(self):
        return D_FF // self.tf

    @property
    def nt(self):
        return NTOK // self.tm

    def tile(self, s):
        return jnp.clip(s - (self.nj - 1), 0, self.nt - 1)

    def done(self, s):
        return jnp.clip(s - self.nj, 0, self.nt - 1)


N_CONF_IN, N_CONF_SCRATCH = 11, 4


def _sqrelu(a):
    return jnp.square(jnp.maximum(a, 0.0)).astype(BF16)


def _mlp_kernel(*refs, cfg):
    it = iter(refs)
    take = lambda n: [next(it) for _ in range(n)]
    x_ref, xd_ref, mod_ref, w1_ref, w2_ref, g_ref, b_ref = take(7)
    conf_in = take(N_CONF_IN) if cfg.conformer else None
    o_refs = take(2 if cfg.split_out else 1)
    w1s, w2s, h_scr, acc_scr = take(4)
    y_scr = take(1)[0] if cfg.split_out else None
    conf_scr = take(N_CONF_SCRATCH) if cfg.conformer else None
    nj, nt, tm = cfg.nj, cfg.nt, cfg.tm

    s = pl.program_id(0)
    shift, scale = _mod_rows(mod_ref, cfg.tile(s) * tm, (3, 4))
    done = cfg.done(s)

    def finish(emit_matmuls=None):
        (gate,) = _mod_rows(mod_ref, done * tm, (5,))
        y = _layer_norm(ALPHA * xd_ref[...] + gate * acc_scr[...], g_ref[...], b_ref[...])
        if cfg.conformer:
            y = _conformer_tile(y, done, conf_in, conf_scr, emit_matmuls)
        if cfg.split_out:
            y_scr[...] = y
        else:
            o_refs[0][...] = y

    def route():
        if cfg.split_out:
            @pl.when(done < NCTX // tm)
            def _():
                o_refs[0][...] = y_scr[...]

            @pl.when(done >= NCTX // tm)
            def _():
                o_refs[1][...] = y_scr[...]

    @pl.when(s < nj)
    def _():
        w1s[s] = w1_ref[...].astype(BF16)
        w2s[s] = w2_ref[...].astype(BF16)
        if cfg.conformer:
            for src, dst in ((conf_in[1], conf_scr[0]), (conf_in[7], conf_scr[1])):
                rows = D // nj
                dst[pl.ds(pl.multiple_of(s * rows, rows), rows), :] = src[...].astype(BF16)

        @pl.when(s == 0)
        def _():
            h_scr[...] = (x_ref[...] * (1.0 + scale) + shift).astype(BF16)
            acc_scr[...] = jnp.zeros_like(acc_scr)

        a = _sqrelu(jnp.dot(h_scr[...], w1s[s], preferred_element_type=F32))
        acc_scr[...] += jnp.dot(a, w2s[s], preferred_element_type=F32)

    @pl.when(jnp.logical_and(s >= nj, s < nj + nt - 1))
    def _():
        h_scr[...] = (x_ref[...] * (1.0 + scale) + shift).astype(BF16)
        chunks = [(j, slice(c * TC_MLP, (c + 1) * TC_MLP))
                  for j in range(nj) for c in range(cfg.tf // TC_MLP)]
        queue = [("up", 0)]
        for k in range(len(chunks)):
            queue += ([("up", k + 1)] if k + 1 < len(chunks) else []) + [("down", k)]
        act, st = {}, dict(pos=0, acc=None)

        def emit_matmuls(n):
            for kind, k in queue[st["pos"]:st["pos"] + n]:
                j, cols = chunks[k]
                if kind == "up":
                    act[k] = _sqrelu(jnp.dot(h_scr[...], w1s[j, :, cols], preferred_element_type=F32))
                else:
                    part = jnp.dot(act.pop(k), w2s[j, cols, :], preferred_element_type=F32)
                    st["acc"] = part if st["acc"] is None else st["acc"] + part
            st["pos"] = min(st["pos"] + n, len(queue))

        emit_matmuls(2)
        finish(lambda: emit_matmuls(1))
        emit_matmuls(len(queue))
        acc_scr[...] = st["acc"]
        route()

    @pl.when(s == nj + nt - 1)
    def _():
        finish()
        route()


def _mlp(x, mods, l, w1, w2, ln_g, ln_b, split_out, conformer=None):
    cfg = _MlpCfg(tm=TM if conformer else TM_MLP, tf=TC_MLP if conformer else TF_MLP,
                  split_out=split_out, conformer=conformer is not None)
    tm, tf, nj = cfg.tm, cfg.tf, cfg.nj
    nc = NCTX // tm
    chunk = lambda s: jnp.minimum(s, nj - 1)
    in_specs = [
        pl.BlockSpec((tm, D), lambda s: (cfg.tile(s), 0)),
        pl.BlockSpec((tm, D), lambda s: (cfg.done(s), 0)),
        _mod_spec(l),
        pl.BlockSpec((None, D, tf), lambda s: (l, 0, chunk(s))),
        pl.BlockSpec((None, tf, D), lambda s: (l, chunk(s), 0)),
        _stacked((1, D), 2 * l + 1), _stacked((1, D), 2 * l + 1),
    ]
    args = [x, x, mods, w1, w2, _vec3(ln_g), _vec3(ln_b)]
    scratch = [pltpu.VMEM((nj, D, tf), BF16), pltpu.VMEM((nj, tf, D), BF16),
               pltpu.VMEM((tm, D), BF16), pltpu.VMEM((tm, D), F32)]
    if split_out:
        out_specs = [pl.BlockSpec((tm, D), lambda s: (jnp.minimum(cfg.done(s), nc - 1), 0)),
                     pl.BlockSpec((tm, D), lambda s: (jnp.maximum(cfg.done(s) - nc, 0), 0))]
        out_shape = [jax.ShapeDtypeStruct((NCTX, D), F32), jax.ShapeDtypeStruct((NLAT, D), F32)]
        scratch.append(pltpu.VMEM((tm, D), F32))
    else:
        out_specs = pl.BlockSpec((tm, D), lambda s: (cfg.done(s), 0))
        out_shape = jax.ShapeDtypeStruct((NTOK, D), F32)
    if conformer:
        j, cw1, cb1, wdw, bdw, cg, cb, cw2, cb2 = conformer
        slab = lambda cols: pl.BlockSpec((None, D // nj, cols), lambda s: (j, chunk(s), 0))
        in_specs += [_mod_spec(l + 1), slab(2 * D), _stacked((1, 2 * D), j), _stacked((CONF_W, D), j),
                     _stacked((1, D), j), _stacked((1, D), j), _stacked((1, D), j),
                     slab(D), _stacked((1, D), j),
                     _stacked((1, D), 2 * (l + 1)), _stacked((1, D), 2 * (l + 1))]
        args += [mods, cw1, _vec3(cb1), wdw, _vec3(bdw), _vec3(cg), _vec3(cb), cw2, _vec3(cb2),
                 _vec3(ln_g), _vec3(ln_b)]
        scratch += [pltpu.VMEM((D, 2 * D), BF16), pltpu.VMEM((D, D), BF16),
                    pltpu.VMEM((CONF_ROWS, CONF_LB), F32),
                    pltpu.VMEM((SUBLANE - 1, CONF_ROWS, CONF_LB), F32)]
        assert len(in_specs) == 7 + N_CONF_IN
    return pl.pallas_call(
        functools.partial(_mlp_kernel, cfg=cfg),
        grid=(nj + cfg.nt,),
        in_specs=in_specs,
        out_specs=out_specs,
        out_shape=out_shape,
        scratch_shapes=scratch,
        compiler_params=_cparams("arbitrary"),
        name="mlp_conformer" if conformer else "mlp",
    )(*args)


CONF_PAD = 16
CONF_LB = 256
CONF_ROWS = (TM // GRID_W) * (GRID_W + 2 * CONF_PAD)


def _conf_conv(upad_ref, shf_ref, w_taps, u, joined, between=None):
    nseg = TM // GRID_W
    stride = GRID_W + 2 * CONF_PAD
    zpad = jnp.zeros((CONF_PAD, CONF_LB), F32)
    for s in range(nseg):
        base, r0 = s * stride, s * GRID_W
        above = jnp.where(joined, u[r0 - CONF_PAD:r0, :], 0.0) if s > 0 else zpad
        below = jnp.where(joined, u[r0 + GRID_W:r0 + GRID_W + CONF_PAD, :], 0.0) if s < nseg - 1 else zpad
        upad_ref[base:base + CONF_PAD, :] = above
        upad_ref[base + CONF_PAD:base + CONF_PAD + GRID_W, :] = u[r0:r0 + GRID_W, :]
        upad_ref[base + CONF_PAD + GRID_W:base + stride, :] = below
    for b in range(1, SUBLANE):
        shf_ref[b - 1, 0:CONF_ROWS - SUBLANE, :] = upad_ref[b:b + CONF_ROWS - SUBLANE, :]
    out = []
    for s in range(nseg):
        if between is not None:
            between()
        r0 = s * stride + CONF_PAD
        acc = jnp.zeros((GRID_W, CONF_LB), F32)
        for k in range(CONF_W):
            a, b = divmod(k - CONF_W // 2, SUBLANE)
            rows = slice(r0 + SUBLANE * a, r0 + SUBLANE * a + GRID_W)
            acc = acc + w_taps[k] * (upad_ref[rows, :] if b == 0 else shf_ref[b - 1, rows, :])
        out.append(acc)
    return jnp.concatenate(out, axis=0)


def _conformer_tile(x, t, conf_in, conf_scr, between=None):
    mod_ref, _, b1_ref, wdw_ref, bdw_ref, cg_ref, cb_ref, _, b2_ref, g_ref, b_ref = conf_in
    w1s, w2s, upad, shf = conf_scr
    shift, scale, gate = _mod_rows(mod_ref, t * TM, (0, 1, 2))
    h = (x * (1.0 + scale) + shift).astype(BF16)

    def glu(lb):
        cols = slice(lb * CONF_LB, (lb + 1) * CONF_LB)
        gcols = slice(D + lb * CONF_LB, D + (lb + 1) * CONF_LB)
        a = jnp.dot(h, w1s[:, cols], preferred_element_type=F32) + b1_ref[:, cols]
        g = jnp.dot(h, w1s[:, gcols], preferred_element_type=F32) + b1_ref[:, gcols]
        return a * jax.nn.sigmoid(g)

    n_lb = D // CONF_LB
    conv = []
    u = glu(0)
    for lb in range(n_lb):
        u_next = glu(lb + 1) if lb + 1 < n_lb else None
        cols = slice(lb * CONF_LB, (lb + 1) * CONF_LB)
        taps = [wdw_ref[k:k + 1, cols] for k in range(CONF_W)]
        conv.append(_conf_conv(upad, shf, taps, u, t < CTX_TILES, between))
        u = u_next

    hook = between if between is not None else (lambda: None)
    hook()
    uc = jnp.concatenate(conv, axis=1) + bdw_ref[...]
    uc = _silu(_layer_norm(uc, cg_ref[...], cb_ref[...]))
    y = jnp.dot(uc.astype(BF16), w2s[...], preferred_element_type=F32) + b2_ref[...]
    hook()
    return _layer_norm(ALPHA * x + gate * y, g_ref[...], b_ref[...])


def _sconv_kernel(x_ref, xp_ref, xn_ref, mod_ref, win_ref, wc_ref, wout_ref, g_ref, b_ref, o_ref,
                  wins, wouts, cpad):
    i = pl.program_id(0)
    shift, scale, gate = _mod_rows(mod_ref, i * TM, (0, 1, 2))

    @pl.when(i == 0)
    def _():
        _cast_rows(win_ref, wins, 128)
        _cast_rows(wout_ref, wouts, 128)

    scale = 1.0 + scale
    w0, w1, w2 = wc_ref[0:1, :], wc_ref[1:2, :], wc_ref[2:3, :]

    def finish(x, h, y):
        bg = jnp.dot(h, wins[:, :D], preferred_element_type=F32)
        out = jnp.dot((bg * y).astype(BF16), wouts[...], preferred_element_type=F32)
        o_ref[...] = _layer_norm(ALPHA * x + gate * out, g_ref[...], b_ref[...])

    @pl.when(i < CTX_TILES)
    def _():
        x = x_ref[...]
        h = (x * scale + shift).astype(BF16)
        cu = jnp.dot(h, wins[:, D:], preferred_element_type=F32)
        cu = cu[:, :D] * cu[:, D:]
        zrow = jnp.zeros((SUBLANE, D), F32)
        cpad[0:SUBLANE, :] = zrow
        cpad[SUBLANE:SUBLANE + TM, :] = cu
        cpad[SUBLANE + TM:2 * SUBLANE + TM, :] = zrow
        finish(x, h, w0 * cpad[SUBLANE - 1:SUBLANE - 1 + TM, :] + w1 * cu
               + w2 * cpad[SUBLANE + 1:SUBLANE + 1 + TM, :])

    @pl.when(i >= CTX_TILES)
    def _():
        r = (i - CTX_TILES) % TILES_PER_LAT
        x = x_ref[...]
        h = (x * scale + shift).astype(BF16)
        hp = (xp_ref[...] * scale + shift).astype(BF16)
        hn = (xn_ref[...] * scale + shift).astype(BF16)
        hcat = jnp.concatenate([hp, h, hn], axis=0)
        cu = jnp.dot(hcat, wins[:, D:], preferred_element_type=F32)
        cu = cu[:, :D] * cu[:, D:]
        halo_up = jnp.where(r > 0, cu[0:GRID_W, :], 0.0)
        halo_dn = jnp.where(r < TILES_PER_LAT - 1, cu[GRID_W + TM:, :], 0.0)
        up = jnp.concatenate([halo_up, cu[GRID_W:TM, :]], axis=0)
        dn = jnp.concatenate([cu[2 * GRID_W:GRID_W + TM, :], halo_dn], axis=0)
        finish(x, h, w0 * up + w1 * cu[GRID_W:GRID_W + TM, :] + w2 * dn)


def _short_conv(x, mods, l, j, w_in, w_conv, w_out, ln_g, ln_b):
    halo_per_tile = TM // GRID_W
    n_halo = NTOK // GRID_W
    return pl.pallas_call(
        _sconv_kernel,
        grid=(NTILE,),
        in_specs=[
            pl.BlockSpec((TM, D), lambda i: (i, 0)),
            pl.BlockSpec((GRID_W, D), lambda i: (jnp.maximum(i * halo_per_tile - 1, 0), 0)),
            pl.BlockSpec((GRID_W, D), lambda i: (jnp.minimum((i + 1) * halo_per_tile, n_halo - 1), 0)),
            _mod_spec(l),
            _stacked((D, 3 * D), j, True),
            _stacked((3, D), j),
            _stacked((D, D), j, True),
            _stacked((1, D), 2 * l), _stacked((1, D), 2 * l),
        ],
        out_specs=pl.BlockSpec((TM, D), lambda i: (i, 0)),
        out_shape=jax.ShapeDtypeStruct((NTOK, D), F32),
        scratch_shapes=[pltpu.VMEM((D, 3 * D), BF16), pltpu.VMEM((D, D), BF16),
                        pltpu.VMEM((TM + 2 * SUBLANE, D), F32)],
        compiler_params=_cparams("arbitrary"),
        name="short_conv",
    )(x, x, x, mods, w_in, w_conv, w_out, _vec3(ln_g), _vec3(ln_b))


N_CH = TM // CHUNK


STATE = (H, DK, DV)


def _state_out_spec(tile, j, d):
    return pl.BlockSpec((None, None, None) + STATE,
                        lambda i: (jnp.minimum(tile(i), CTX_TILES - 1), j, d, 0, 0, 0))


def _emit_state(st_ref, s_out, t):
    @pl.when(t < CTX_TILES)
    def _():
        for hd in range(H):
            st_ref[hd] = s_out[hd]


def _lat_seq(t):
    return jnp.clip((t - CTX_TILES) // TILES_PER_LAT, 0, DEC_BATCH - 1)


def _chunk_tri(rev):
    row = lax.broadcasted_iota(jnp.int32, (TM, TM), 0)
    col = lax.broadcasted_iota(jnp.int32, (TM, TM), 1)
    same = (row // CHUNK) == (col // CHUNK)
    return same & ((col >= row) if rev else (col <= row))


def _scan_state_in(s_scr, s0_ref, t, rev):
    first_r = TILES_PER_LAT - 1 if rev else 0
    fresh = jnp.logical_and(t >= CTX_TILES, (t - CTX_TILES) % TILES_PER_LAT == first_r)
    s = jnp.where(fresh, s0_ref[...], s_scr[...])
    return jnp.where(t < CTX_TILES, 0.0, s)


def _decay_operands(q, k, bcum, rev):
    qd, kd, ke, dec = [], [], [], []
    for c in range(N_CH):
        rows = slice(c * CHUNK, (c + 1) * CHUNK)
        b = bcum[rows, :]
        last = b[0:1, :] if rev else b[CHUNK - 1:CHUNK, :]
        qd.append((q[rows, :] * jnp.exp(b)).astype(BF16))
        kd.append((k[rows, :] * jnp.exp(-b)).astype(BF16))
        ke.append(k[rows, :] * jnp.exp(last - b))
        dec.append(jnp.exp(last))
    return (jnp.concatenate(qd, axis=0), jnp.concatenate(kd, axis=0),
            jnp.concatenate(ke, axis=0).T.astype(BF16), dec)


def _scan_tile(heads, s_in, rev, hooks=(None, None)):
    mask = _chunk_tri(rev)
    kcol = lax.broadcasted_iota(jnp.int32, (DK, TM), 1) // CHUNK
    order = range(N_CH - 1, -1, -1) if rev else range(N_CH)
    sc, kv = [], []
    for qh, kh, keth, vh, _ in heads:
        sc.append(lax.dot_general(qh, kh, (((1,), (1,)), ((), ())), preferred_element_type=F32))
        kst = jnp.concatenate([jnp.where(kcol == c, keth, jnp.zeros_like(keth)) for c in range(N_CH)],
                              axis=0)
        kv.append(jnp.dot(kst, vh, preferred_element_type=F32))
    if hooks[0] is not None:
        hooks[0]()
    o_intra = [jnp.dot(jnp.where(mask, s, 0.0).astype(BF16), hd[3], preferred_element_type=F32)
               for s, hd in zip(sc, heads)]
    if hooks[1] is not None:
        hooks[1]()
    outs, s_out = [], []
    for hd, (qh, _, _, _, dec_rows) in enumerate(heads):
        s, o = s_in[hd], [None] * N_CH
        for c in order:
            rows = slice(c * CHUNK, (c + 1) * CHUNK)
            o[c] = o_intra[hd][rows, :] + jnp.dot(qh[rows, :], s.astype(BF16),
                                                  preferred_element_type=F32)
            dec_col = jnp.broadcast_to(dec_rows[c], (DK, DK)).T
            s = s * jnp.concatenate([dec_col, dec_col], axis=1) + kv[hd][c * DK:(c + 1) * DK, :]
        outs.append(o)
        s_out.append(s)
    return outs, s_out


def _gla_fwd_kernel(*refs, n_x):
    x_refs = refs[:n_x]
    (mod_ref, win_ref, wga_ref, wgb_ref, bg_ref, s0_ref, _,
     of_ref, qd_ref, kd_ref, ket_ref, v_ref, r_ref, dec_ref, st_ref,
     wins, tri_scr, s_scr) = refs[n_x:]
    i = pl.program_id(0)
    shift, scale = _mod_rows(mod_ref, i * TM, (0, 1))

    @pl.when(i == 0)
    def _():
        _cast_rows(win_ref, wins, 128)
        tri_scr[0] = jnp.where(_chunk_tri(False), 1.0, 0.0).astype(BF16)
        tri_scr[1] = jnp.where(_chunk_tri(True), 1.0, 0.0).astype(BF16)
        s_scr[...] = jnp.zeros_like(s_scr)

    h = (_read_x(x_refs, i) * (1.0 + scale) + shift).astype(BF16)

    g_parts = []
    for d in range(2):
        za = jnp.dot(h, wga_ref[d].astype(BF16), preferred_element_type=F32)
        z = jnp.dot(za.astype(BF16), wgb_ref[d].astype(BF16), preferred_element_type=F32) + bg_ref[d]
        g = (jnp.minimum(z, 0.0) - jnp.log1p(jnp.exp(-jnp.abs(z)))) * (1.0 / GATE_NORM)
        g_hi = g.astype(BF16)
        g_parts.append((g_hi, (g - g_hi.astype(F32)).astype(BF16)))

    proj = jnp.dot(h, wins[...], preferred_element_type=F32)
    q = proj[:, :KW] * (DK ** -0.5)
    k = proj[:, KW:2 * KW]
    v = proj[:, 2 * KW:2 * KW + D].astype(BF16)
    v_ref[...] = v
    r_ref[...] = proj[:, 2 * KW + D:].astype(BF16)

    bcum = [jnp.dot(tri_scr[d], g_hi, preferred_element_type=F32)
            + jnp.dot(tri_scr[d], g_lo, preferred_element_type=F32)
            for d, (g_hi, g_lo) in enumerate(g_parts)]

    qd_b, kd_b, ket_b, dec_b = _decay_operands(q, k, bcum[1], True)
    qd_ref[...] = qd_b
    kd_ref[...] = kd_b
    ket_ref[...] = ket_b
    dec_ref[...] = jnp.concatenate(dec_b + [jnp.zeros((SUBLANE - N_CH, KW), F32)], axis=0)

    qd, kd, ket, dec = _decay_operands(q, k, bcum[0], False)
    heads = []
    for hd in range(H):
        ks, vs = slice(hd * DK, (hd + 1) * DK), slice(hd * DV, (hd + 1) * DV)
        heads.append((qd[:, ks], kd[:, ks], ket[ks, :], v[:, vs], [e[:, ks] for e in dec]))
    outs, s_out = _scan_tile(heads, _scan_state_in(s_scr, s0_ref, i, False), False)
    for hd in range(H):
        for c in range(N_CH):
            of_ref[c * CHUNK:(c + 1) * CHUNK, hd * DV:(hd + 1) * DV] = outs[hd][c].astype(BF16)
        s_scr[hd] = s_out[hd]
    _emit_state(st_ref, s_out, i)


def _gla_fwd(x, mods, l, j, w_in, w_ga, w_gb, b_g, state_gla, new_states):
    xs = _x_args(x)
    tile = pl.BlockSpec((TM, D), lambda i: (i, 0))
    keys = pl.BlockSpec((TM, KW), lambda i: (i, 0))
    n_in = len(xs) + 7
    return pl.pallas_call(
        functools.partial(_gla_fwd_kernel, n_x=len(xs)),
        grid=(NTILE,),
        in_specs=_x_specs(len(xs) == 2) + [
            _mod_spec(l),
            _stacked((D, 2 * KW + 2 * D), j, True),
            _stacked((2, D, RANK), j),
            _stacked((2, RANK, KW), j),
            _stacked((2, 1, KW), j),
            pl.BlockSpec((None, None, None) + STATE, lambda i: (_lat_seq(i), j, 0, 0, 0, 0)),
            pl.BlockSpec(memory_space=pl.ANY),
        ],
        out_specs=[
            tile, keys, keys,
            pl.BlockSpec((KW, TM), lambda i: (0, i)),
            tile, tile,
            pl.BlockSpec((None, SUBLANE, KW), lambda i: (i, 0, 0)),
            _state_out_spec(lambda i: i, j, 0),
        ],
        out_shape=[
            jax.ShapeDtypeStruct((NTOK, D), BF16),
            jax.ShapeDtypeStruct((NTOK, KW), BF16),
            jax.ShapeDtypeStruct((NTOK, KW), BF16),
            jax.ShapeDtypeStruct((KW, NTOK), BF16),
            jax.ShapeDtypeStruct((NTOK, D), BF16),
            jax.ShapeDtypeStruct((NTOK, D), BF16),
            jax.ShapeDtypeStruct((NTILE, SUBLANE, KW), F32),
            jax.ShapeDtypeStruct(new_states.shape, F32),
        ],
        input_output_aliases={n_in - 1: 7},
        scratch_shapes=[pltpu.VMEM((D, 2 * KW + 2 * D), BF16), pltpu.VMEM((2, TM, TM), BF16),
                        pltpu.VMEM(STATE, F32)],
        compiler_params=_cparams("arbitrary"),
        name="gla_fwd",
    )(*xs, mods, w_in, w_ga, w_gb, b_g.reshape(b_g.shape[0], 2, 1, KW), state_gla, new_states)


def _gla_bwd_kernel(*refs, n_x):
    x_refs = refs[:n_x]
    (qd_ref, kd_ref, ket_ref, v_ref, dec_ref, s0_ref, of_ref, r_ref, mod_ref, gn_ref, wo_ref,
     g_ref, b_ref, _, o_ref, st_ref, wos, s_scr, ob_scr) = refs[n_x:]
    i = pl.program_id(0)
    t_scan = NTILE - 1 - i
    t_post = NTILE - i
    (gate,) = _mod_rows(mod_ref, t_post * TM, (2,))

    @pl.when(i == 0)
    def _():
        _cast_rows(wo_ref, wos, 128)
        s_scr[...] = jnp.zeros_like(s_scr)

    def scan(hooks=(None, None)):
        heads = []
        for hd in range(H):
            ks, vs = slice(hd * DK, (hd + 1) * DK), slice(hd * DV, (hd + 1) * DV)
            heads.append((qd_ref[:, ks], kd_ref[:, ks], ket_ref[ks, :], v_ref[:, vs],
                          [dec_ref[c:c + 1, ks] for c in range(N_CH)]))
        outs, s_out = _scan_tile(heads, _scan_state_in(s_scr, s0_ref, t_scan, True), True, hooks)
        for hd in range(H):
            s_scr[hd] = s_out[hd]
            for c in range(N_CH):
                ob_scr[i % 2, c * CHUNK:(c + 1) * CHUNK, hd * DV:(hd + 1) * DV] = outs[hd][c]
        _emit_state(st_ref, s_out, t_scan)

    post = {}

    def post_norm_gate():
        parts = []
        for hd in range(H):
            vs = slice(hd * DV, (hd + 1) * DV)
            oh = of_ref[:, vs].astype(F32) + ob_scr[(i + 1) % 2, :, vs]
            ms = jnp.mean(oh * oh, axis=-1, keepdims=True)
            parts.append(oh * lax.rsqrt(ms + RMS_EPS))
        on = jnp.concatenate(parts, axis=1) * gn_ref[...]
        post["a"] = (on * _silu(r_ref[...].astype(F32))).astype(BF16)

    def post_w_o():
        post["y"] = jnp.dot(post["a"], wos[...], preferred_element_type=F32)

    def post_residual():
        o_ref[...] = _layer_norm(ALPHA * _read_x(x_refs, t_post) + gate * post["y"], g_ref[...], b_ref[...])

    @pl.when(i == 0)
    def _():
        scan()

    @pl.when(jnp.logical_and(i > 0, i < NTILE))
    def _():
        post_norm_gate()
        scan((post_w_o, post_residual))

    @pl.when(i == NTILE)
    def _():
        post_norm_gate()
        post_w_o()
        post_residual()


def _gla_bwd(x, o_f, qd, kd, ket, v, r, dec, state_gla, new_states, mods, l, j, gn_g, w_o, ln_g, ln_b):
    xs = _x_args(x)
    stile = lambda i: jnp.maximum(NTILE - 1 - i, 0)
    ptile = lambda i: jnp.minimum(NTILE - i, NTILE - 1)
    scan_tile = pl.BlockSpec((TM, D), lambda i: (stile(i), 0))
    post_tile = pl.BlockSpec((TM, D), lambda i: (ptile(i), 0))
    keys = pl.BlockSpec((TM, KW), lambda i: (stile(i), 0))
    n_in = len(xs) + 14
    return pl.pallas_call(
        functools.partial(_gla_bwd_kernel, n_x=len(xs)),
        grid=(NTILE + 1,),
        in_specs=_x_specs(len(xs) == 2, ptile) + [
            keys, keys,
            pl.BlockSpec((KW, TM), lambda i: (0, stile(i))),
            scan_tile,
            pl.BlockSpec((None, SUBLANE, KW), lambda i: (stile(i), 0, 0)),
            pl.BlockSpec((None, None, None) + STATE, lambda i: (_lat_seq(stile(i)), j, 1, 0, 0, 0)),
            post_tile, post_tile, _mod_spec(l),
            _stacked((1, D), j), _stacked((D, D), j, True),
            _stacked((1, D), 2 * l), _stacked((1, D), 2 * l),
            pl.BlockSpec(memory_space=pl.ANY)],
        out_specs=[post_tile, _state_out_spec(stile, j, 1)],
        out_shape=[jax.ShapeDtypeStruct((NTOK, D), F32), jax.ShapeDtypeStruct(new_states.shape, F32)],
        input_output_aliases={n_in - 1: 1},
        scratch_shapes=[pltpu.VMEM((D, D), BF16), pltpu.VMEM(STATE, F32), pltpu.VMEM((2, TM, D), F32)],
        compiler_params=_cparams("arbitrary"),
        name="gla_bwd",
    )(*xs, qd, kd, ket, v, dec, state_gla, o_f, r, mods, _vec3(gn_g), w_o, _vec3(ln_g), _vec3(ln_b),
      new_states)


def kernel(x_prompt, x_sample, c, state_gla, c_ctx, mod_w, mod_b, ln_g, ln_b, ff_w1, ff_w2, gla_w_in, gla_w_ga, gla_w_gb, gla_b_g, gla_gn_g, gla_w_o, conf_w_pw1, conf_b_pw1, conf_w_dw, conf_b_dw, conf_ln_g, conf_ln_b, conf_w_pw2, conf_b_pw2, sc_w_in, sc_w_conv, sc_w_out):
    assert x_prompt.shape == (BATCH, SEQ, D) and x_sample.shape == (DEC_BATCH, DEC_SEQ, D)
    x = (x_prompt.reshape(NCTX, D), x_sample.reshape(NLAT, D))

    c8 = jnp.concatenate([c_ctx[None, :], c, jnp.zeros((SUBLANE - N_CVEC, D), F32)], axis=0)
    mods = _adaln(c8, mod_w, mod_b)

    states = jnp.zeros((BATCH, state_gla.shape[1], 2) + STATE, F32)
    for l in range(DEPTH):
        kind, j = l % 3, l // 3
        if kind == 0:
            o_f, qd, kd, ket, v, r, dec, states = _gla_fwd(x, mods, l, j, gla_w_in, gla_w_ga, gla_w_gb,
                                                           gla_b_g, state_gla, states)
            x, states = _gla_bwd(x, o_f, qd, kd, ket, v, r, dec, state_gla, states, mods, l, j,
                                 gla_gn_g, gla_w_o, ln_g, ln_b)
        elif kind == 2:
            x = _short_conv(x, mods, l, j, sc_w_in, sc_w_conv, sc_w_out, ln_g, ln_b)
        conformer = None
        if l + 1 < DEPTH and (l + 1) % 3 == 1:
            conformer = ((l + 1) // 3, conf_w_pw1, conf_b_pw1, conf_w_dw, conf_b_dw, conf_ln_g, conf_ln_b,
                         conf_w_pw2, conf_b_pw2)
        x = _mlp(x, mods, l, ff_w1, ff_w2, ln_g, ln_b, split_out=(l == DEPTH - 1), conformer=conformer)

    y_prompt, y_sample = x
    return (y_prompt.reshape(BATCH, SEQ, D), y_sample.reshape(DEC_BATCH, DEC_SEQ, D), states)
```

```python
import dataclasses
import functools

import jax
import jax.numpy as jnp
from jax import lax
from jax.experimental import pallas as pl
from jax.experimental.pallas import tpu as pltpu

F32 = jnp.float32
BF16 = jnp.bfloat16

D = 1024
DEPTH = 4
BATCH, SEQ = 16, 256
DEC_BATCH, DEC_SEQ = 2, 2048
GRID_W = 64
N_MOD = 6
N_CVEC = 1 + DEC_BATCH
NCTX = BATCH * SEQ
NLAT = DEC_BATCH * DEC_SEQ
NTOK = NCTX + NLAT
H, DK, DV = 4, 128, 256
KW = H * DK
RANK = 16
CHUNK = 64
GATE_NORM = 16.0
CONF_W = 31
D_FF = 4 * D
LN_EPS = 1e-5
RMS_EPS = 1e-6
ALPHA = (2 * DEPTH) ** 0.25

TM = 256
NTILE = NTOK // TM
CTX_TILES = NCTX // TM
TILES_PER_LAT = DEC_SEQ // TM
TM_MLP = 512
TF_MLP = 1024
TC_MLP = 512
TN_MOD = 2048
SUBLANE = 8
VMEM_LIMIT = 58 * 1024 * 1024


def _cparams(*sem):
    return pltpu.CompilerParams(dimension_semantics=sem, vmem_limit_bytes=VMEM_LIMIT)


def _mod_rows(mod_ref, row0, ks):
    m = jnp.where(row0 < NCTX, 0, 1 + (row0 - NCTX) // DEC_SEQ)
    return [mod_ref[pl.ds(m, 1), k * D:(k + 1) * D] for k in ks]


def _mod_spec(l):
    return pl.BlockSpec((None, SUBLANE, N_MOD * D), lambda *_: (l, 0, 0))


def _stacked(block, j, single_buffer=False):
    nd = len(block)
    mode = dict(pipeline_mode=pl.Buffered(1)) if single_buffer else {}
    return pl.BlockSpec((None,) + block, lambda *_: (j,) + (0,) * nd, **mode)


def _x_specs(split, block=lambda i: i, rows=TM):
    if not split:
        return [pl.BlockSpec((rows, D), lambda i: (block(i), 0))]
    nc, nl = NCTX // rows, NLAT // rows
    return [pl.BlockSpec((rows, D), lambda i: (jnp.clip(block(i), 0, nc - 1), 0)),
            pl.BlockSpec((rows, D), lambda i: (jnp.clip(block(i) - nc, 0, nl - 1), 0))]


def _x_args(x):
    return list(x) if isinstance(x, tuple) else [x]


def _read_x(x_refs, t, rows=slice(None)):
    if len(x_refs) == 1:
        return x_refs[0][rows, :]
    return jnp.where(t < CTX_TILES, x_refs[0][rows, :], x_refs[1][rows, :])


def _layer_norm(y, g, b):
    mu = jnp.mean(y, axis=-1, keepdims=True)
    yc = y - mu
    var = jnp.mean(yc * yc, axis=-1, keepdims=True)
    return yc * lax.rsqrt(var + LN_EPS) * g + b


def _silu(x):
    return x * jax.nn.sigmoid(x)


def _cast_rows(src_ref, dst_ref, step):
    n = src_ref.shape[0] // step

    def body(r, c):
        rows = pl.ds(pl.multiple_of(r * step, step), step)
        dst_ref[rows, :] = src_ref[rows, :].astype(BF16)
        return c

    lax.fori_loop(0, n, body, 0)


def _vec3(a):
    return a.reshape(-1, 1, a.shape[-1])


def _adaln_kernel(c_ref, w_ref, b_ref, o_ref):
    s = _silu(c_ref[...]).astype(BF16)
    o_ref[...] = jnp.dot(s, w_ref[...].astype(BF16), preferred_element_type=F32) + b_ref[...]


def _adaln(c8, mod_w, mod_b):
    return pl.pallas_call(
        _adaln_kernel,
        grid=(DEPTH, N_MOD * D // TN_MOD),
        in_specs=[
            pl.BlockSpec((SUBLANE, D), lambda l, n: (0, 0)),
            pl.BlockSpec((None, D, TN_MOD), lambda l, n: (l, 0, n)),
            pl.BlockSpec((None, 1, TN_MOD), lambda l, n: (l, 0, n)),
        ],
        out_specs=pl.BlockSpec((None, SUBLANE, TN_MOD), lambda l, n: (l, 0, n)),
        out_shape=jax.ShapeDtypeStruct((DEPTH, SUBLANE, N_MOD * D), F32),
        compiler_params=_cparams("parallel", "parallel"),
        name="adaln",
    )(c8, mod_w, _vec3(mod_b))


@dataclasses.dataclass(frozen=True)
class _MlpCfg:
    tm: int
    tf: int
    split_out: bool
    conformer: bool

    @property
    def nj(self):
        return D_FF // self.tf

    @property
    def nt(self):
        return NTOK // self.tm

    def tile(self, s):
        return jnp.clip(s - (self.nj - 1), 0, self.nt - 1)

    def done(self, s):
        return jnp.clip(s - self.nj, 0, self.nt - 1)


N_CONF_IN, N_CONF_SCRATCH = 11, 4


def _sqrelu(a):
    return jnp.square(jnp.maximum(a, 0.0)).astype(BF16)


def _mlp_kernel(*refs, cfg):
    it = iter(refs)
    take = lambda n: [next(it) for _ in range(n)]
    x_ref, xd_ref, mod_ref, w1_ref, w2_ref, g_ref, b_ref = take(7)
    conf_in = take(N_CONF_IN) if cfg.conformer else None
    o_refs = take(2 if cfg.split_out else 1)
    w1s, w2s, h_scr, acc_scr = take(4)
    y_scr = take(1)[0] if cfg.split_out else None
    conf_scr = take(N_CONF_SCRATCH) if cfg.conformer else None
    nj, nt, tm = cfg.nj, cfg.nt, cfg.tm

    s = pl.program_id(0)
    shift, scale = _mod_rows(mod_ref, cfg.tile(s) * tm, (3, 4))
    done = cfg.done(s)

    def finish(emit_matmuls=None):
        (gate,) = _mod_rows(mod_ref, done * tm, (5,))
        y = _layer_norm(ALPHA * xd_ref[...] + gate * acc_scr[...], g_ref[...], b_ref[...])
        if cfg.conformer:
            y = _conformer_tile(y, done, conf_in, conf_scr, emit_matmuls)
        if cfg.split_out:
            y_scr[...] = y
        else:
            o_refs[0][...] = y

    def route():
        if cfg.split_out:
            @pl.when(done < NCTX // tm)
            def _():
                o_refs[0][...] = y_scr[...]

            @pl.when(done >= NCTX // tm)
            def _():
                o_refs[1][...] = y_scr[...]

    @pl.when(s < nj)
    def _():
        w1s[s] = w1_ref[...].astype(BF16)
        w2s[s] = w2_ref[...].astype(BF16)
        if cfg.conformer:
            for src, dst in ((conf_in[1], conf_scr[0]), (conf_in[7], conf_scr[1])):
                rows = D // nj
                dst[pl.ds(pl.multiple_of(s * rows, rows), rows), :] = src[...].astype(BF16)

        @pl.when(s == 0)
        def _():
            h_scr[...] = (x_ref[...] * (1.0 + scale) + shift).astype(BF16)
            acc_scr[...] = jnp.zeros_like(acc_scr)

        a = _sqrelu(jnp.dot(h_scr[...], w1s[s], preferred_element_type=F32))
        acc_scr[...] += jnp.dot(a, w2s[s], preferred_element_type=F32)

    @pl.when(jnp.logical_and(s >= nj, s < nj + nt - 1))
    def _():
        h_scr[...] = (x_ref[...] * (1.0 + scale) + shift).astype(BF16)
        chunks = [(j, slice(c * TC_MLP, (c + 1) * TC_MLP))
                  for j in range(nj) for c in range(cfg.tf // TC_MLP)]
        queue = [("up", 0)]
        for k in range(len(chunks)):
            queue += ([("up", k + 1)] if k + 1 < len(chunks) else []) + [("down", k)]
        act, st = {}, dict(pos=0, acc=None)

        def emit_matmuls(n):
            for kind, k in queue[st["pos"]:st["pos"] + n]:
                j, cols = chunks[k]
                if kind == "up":
                    act[k] = _sqrelu(jnp.dot(h_scr[...], w1s[j, :, cols], preferred_element_type=F32))
                else:
                    part = jnp.dot(act.pop(k), w2s[j, cols, :], preferred_element_type=F32)
                    st["acc"] = part if st["acc"] is None else st["acc"] + part
            st["pos"] = min(st["pos"] + n, len(queue))

        emit_matmuls(2)
        finish(lambda: emit_matmuls(1))
        emit_matmuls(len(queue))
        acc_scr[...] = st["acc"]
        route()

    @pl.when(s == nj + nt - 1)
    def _():
        finish()
        route()


def _mlp(x, mods, l, w1, w2, ln_g, ln_b, split_out, conformer=None):
    cfg = _MlpCfg(tm=TM if conformer else TM_MLP, tf=TC_MLP if conformer else TF_MLP,
                  split_out=split_out, conformer=conformer is not None)
    tm, tf, nj = cfg.tm, cfg.tf, cfg.nj
    nc = NCTX // tm
    chunk = lambda s: jnp.minimum(s, nj - 1)
    in_specs = [
        pl.BlockSpec((tm, D), lambda s: (cfg.tile(s), 0)),
        pl.BlockSpec((tm, D), lambda s: (cfg.done(s), 0)),
        _mod_spec(l),
        pl.BlockSpec((None, D, tf), lambda s: (l, 0, chunk(s))),
        pl.BlockSpec((None, tf, D), lambda s: (l, chunk(s), 0)),
        _stacked((1, D), 2 * l + 1), _stacked((1, D), 2 * l + 1),
    ]
    args = [x, x, mods, w1, w2, _vec3(ln_g), _vec3(ln_b)]
    scratch = [pltpu.VMEM((nj, D, tf), BF16), pltpu.VMEM((nj, tf, D), BF16),
               pltpu.VMEM((tm, D), BF16), pltpu.VMEM((tm, D), F32)]
    if split_out:
        out_specs = [pl.BlockSpec((tm, D), lambda s: (jnp.minimum(cfg.done(s), nc - 1), 0)),
                     pl.BlockSpec((tm, D), lambda s: (jnp.maximum(cfg.done(s) - nc, 0), 0))]
        out_shape = [jax.ShapeDtypeStruct((NCTX, D), F32), jax.ShapeDtypeStruct((NLAT, D), F32)]
        scratch.append(pltpu.VMEM((tm, D), F32))
    else:
        out_specs = pl.BlockSpec((tm, D), lambda s: (cfg.done(s), 0))
        out_shape = jax.ShapeDtypeStruct((NTOK, D), F32)
    if conformer:
        j, cw1, cb1, wdw, bdw, cg, cb, cw2, cb2 = conformer
        slab = lambda cols: pl.BlockSpec((None, D // nj, cols), lambda s: (j, chunk(s), 0))
        in_specs += [_mod_spec(l + 1), slab(2 * D), _stacked((1, 2 * D), j), _stacked((CONF_W, D), j),
                     _stacked((1, D), j), _stacked((1, D), j), _stacked((1, D), j),
                     slab(D), _stacked((1, D), j),
                     _stacked((1, D), 2 * (l + 1)), _stacked((1, D), 2 * (l + 1))]
        args += [mods, cw1, _vec3(cb1), wdw, _vec3(bdw), _vec3(cg), _vec3(cb), cw2, _vec3(cb2),
                 _vec3(ln_g), _vec3(ln_b)]
        scratch += [pltpu.VMEM((D, 2 * D), BF16), pltpu.VMEM((D, D), BF16),
                    pltpu.VMEM((CONF_ROWS, CONF_LB), F32),
                    pltpu.VMEM((SUBLANE - 1, CONF_ROWS, CONF_LB), F32)]
        assert len(in_specs) == 7 + N_CONF_IN
    return pl.pallas_call(
        functools.partial(_mlp_kernel, cfg=cfg),
        grid=(nj + cfg.nt,),
        in_specs=in_specs,
        out_specs=out_specs,
        out_shape=out_shape,
        scratch_shapes=scratch,
        compiler_params=_cparams("arbitrary"),
        name="mlp_conformer" if conformer else "mlp",
    )(*args)


CONF_PAD = 16
CONF_LB = 256
CONF_ROWS = (TM // GRID_W) * (GRID_W + 2 * CONF_PAD)


def _conf_conv(upad_ref, shf_ref, w_taps, u, joined, between=None):
    nseg = TM // GRID_W
    stride = GRID_W + 2 * CONF_PAD
    zpad = jnp.zeros((CONF_PAD, CONF_LB), F32)
    for s in range(nseg):
        base, r0 = s * stride, s * GRID_W
        above = jnp.where(joined, u[r0 - CONF_PAD:r0, :], 0.0) if s > 0 else zpad
        below = jnp.where(joined, u[r0 + GRID_W:r0 + GRID_W + CONF_PAD, :], 0.0) if s < nseg - 1 else zpad
        upad_ref[base:base + CONF_PAD, :] = above
        upad_ref[base + CONF_PAD:base + CONF_PAD + GRID_W, :] = u[r0:r0 + GRID_W, :]
        upad_ref[base + CONF_PAD + GRID_W:base + stride, :] = below
    for b in range(1, SUBLANE):
        shf_ref[b - 1, 0:CONF_ROWS - SUBLANE, :] = upad_ref[b:b + CONF_ROWS - SUBLANE, :]
    out = []
    for s in range(nseg):
        if between is not None:
            between()
        r0 = s * stride + CONF_PAD
        acc = jnp.zeros((GRID_W, CONF_LB), F32)
        for k in range(CONF_W):
            a, b = divmod(k - CONF_W // 2, SUBLANE)
            rows = slice(r0 + SUBLANE * a, r0 + SUBLANE * a + GRID_W)
            acc = acc + w_taps[k] * (upad_ref[rows, :] if b == 0 else shf_ref[b - 1, rows, :])
        out.append(acc)
    return jnp.concatenate(out, axis=0)


def _conformer_tile(x, t, conf_in, conf_scr, between=None):
    mod_ref, _, b1_ref, wdw_ref, bdw_ref, cg_ref, cb_ref, _, b2_ref, g_ref, b_ref = conf_in
    w1s, w2s, upad, shf = conf_scr
    shift, scale, gate = _mod_rows(mod_ref, t * TM, (0, 1, 2))
    h = (x * (1.0 + scale) + shift).astype(BF16)

    def glu(lb):
        cols = slice(lb * CONF_LB, (lb + 1) * CONF_LB)
        gcols = slice(D + lb * CONF_LB, D + (lb + 1) * CONF_LB)
        a = jnp.dot(h, w1s[:, cols], preferred_element_type=F32) + b1_ref[:, cols]
        g = jnp.dot(h, w1s[:, gcols], preferred_element_type=F32) + b1_ref[:, gcols]
        return a * jax.nn.sigmoid(g)

    n_lb = D // CONF_LB
    conv = []
    u = glu(0)
    for lb in range(n_lb):
        u_next = glu(lb + 1) if lb + 1 < n_lb else None
        cols = slice(lb * CONF_LB, (lb + 1) * CONF_LB)
        taps = [wdw_ref[k:k + 1, cols] for k in range(CONF_W)]
        conv.append(_conf_conv(upad, shf, taps, u, t < CTX_TILES, between))
        u = u_next

    hook = between if between is not None else (lambda: None)
    hook()
    uc = jnp.concatenate(conv, axis=1) + bdw_ref[...]
    uc = _silu(_layer_norm(uc, cg_ref[...], cb_ref[...]))
    y = jnp.dot(uc.astype(BF16), w2s[...], preferred_element_type=F32) + b2_ref[...]
    hook()
    return _layer_norm(ALPHA * x + gate * y, g_ref[...], b_ref[...])


def _sconv_kernel(x_ref, xp_ref, xn_ref, mod_ref, win_ref, wc_ref, wout_ref, g_ref, b_ref, o_ref,
                  wins, wouts, cpad):
    i = pl.program_id(0)
    shift, scale, gate = _mod_rows(mod_ref, i * TM, (0, 1, 2))

    @pl.when(i == 0)
    def _():
        _cast_rows(win_ref, wins, 128)
        _cast_rows(wout_ref, wouts, 128)

    scale = 1.0 + scale
    w0, w1, w2 = wc_ref[0:1, :], wc_ref[1:2, :], wc_ref[2:3, :]

    def finish(x, h, y):
        bg = jnp.dot(h, wins[:, :D], preferred_element_type=F32)
        out = jnp.dot((bg * y).astype(BF16), wouts[...], preferred_element_type=F32)
        o_ref[...] = _layer_norm(ALPHA * x + gate * out, g_ref[...], b_ref[...])

    @pl.when(i < CTX_TILES)
    def _():
        x = x_ref[...]
        h = (x * scale + shift).astype(BF16)
        cu = jnp.dot(h, wins[:, D:], preferred_element_type=F32)
        cu = cu[:, :D] * cu[:, D:]
        zrow = jnp.zeros((SUBLANE, D), F32)
        cpad[0:SUBLANE, :] = zrow
        cpad[SUBLANE:SUBLANE + TM, :] = cu
        cpad[SUBLANE + TM:2 * SUBLANE + TM, :] = zrow
        finish(x, h, w0 * cpad[SUBLANE - 1:SUBLANE - 1 + TM, :] + w1 * cu
               + w2 * cpad[SUBLANE + 1:SUBLANE + 1 + TM, :])

    @pl.when(i >= CTX_TILES)
    def _():
        r = (i - CTX_TILES) % TILES_PER_LAT
        x = x_ref[...]
        h = (x * scale + shift).astype(BF16)
        hp = (xp_ref[...] * scale + shift).astype(BF16)
        hn = (xn_ref[...] * scale + shift).astype(BF16)
        hcat = jnp.concatenate([hp, h, hn], axis=0)
        cu = jnp.dot(hcat, wins[:, D:], preferred_element_type=F32)
        cu = cu[:, :D] * cu[:, D:]
        halo_up = jnp.where(r > 0, cu[0:GRID_W, :], 0.0)
        halo_dn = jnp.where(r < TILES_PER_LAT - 1, cu[GRID_W + TM:, :], 0.0)
        up = jnp.concatenate([halo_up, cu[GRID_W:TM, :]], axis=0)
        dn = jnp.concatenate([cu[2 * GRID_W:GRID_W + TM, :], halo_dn], axis=0)
        finish(x, h, w0 * up + w1 * cu[GRID_W:GRID_W + TM, :] + w2 * dn)


def _short_conv(x, mods, l, j, w_in, w_conv, w_out, ln_g, ln_b):
    halo_per_tile = TM // GRID_W
    n_halo = NTOK // GRID_W
    return pl.pallas_call(
        _sconv_kernel,
        grid=(NTILE,),
        in_specs=[
            pl.BlockSpec((TM, D), lambda i: (i, 0)),
            pl.BlockSpec((GRID_W, D), lambda i: (jnp.maximum(i * halo_per_tile - 1, 0), 0)),
            pl.BlockSpec((GRID_W, D), lambda i: (jnp.minimum((i + 1) * halo_per_tile, n_halo - 1), 0)),
            _mod_spec(l),
            _stacked((D, 3 * D), j, True),
            _stacked((3, D), j),
            _stacked((D, D), j, True),
            _stacked((1, D), 2 * l), _stacked((1, D), 2 * l),
        ],
        out_specs=pl.BlockSpec((TM, D), lambda i: (i, 0)),
        out_shape=jax.ShapeDtypeStruct((NTOK, D), F32),
        scratch_shapes=[pltpu.VMEM((D, 3 * D), BF16), pltpu.VMEM((D, D), BF16),
                        pltpu.VMEM((TM + 2 * SUBLANE, D), F32)],
        compiler_params=_cparams("arbitrary"),
        name="short_conv",
    )(x, x, x, mods, w_in, w_conv, w_out, _vec3(ln_g), _vec3(ln_b))


N_CH = TM // CHUNK
GLA_FWD_TILES = 1
GLA_BWD_TILES = 2
STATE = (H, DK, DV)


def _state_out_spec(block, j, d, n):
    return pl.BlockSpec((n, None, None) + STATE,
                        lambda i: (jnp.minimum(block(i), CTX_TILES // n - 1), j, d, 0, 0, 0))


def _emit_states(st_ref, finals, t0):
    @pl.when(t0 < CTX_TILES)
    def _():
        for sub, final in enumerate(finals):
            for hd in range(H):
                st_ref[sub, hd] = final[hd]


def _lat_seq(t):
    return jnp.clip((t - CTX_TILES) // TILES_PER_LAT, 0, DEC_BATCH - 1)


def _chunk_tri(rev):
    row = lax.broadcasted_iota(jnp.int32, (TM, TM), 0)
    col = lax.broadcasted_iota(jnp.int32, (TM, TM), 1)
    same = (row // CHUNK) == (col // CHUNK)
    return same & ((col >= row) if rev else (col <= row))


def _scan_states_in(s_scr, s0_ref, tiles, rev):
    first_r = TILES_PER_LAT - 1 if rev else 0

    def enter(prev, t, hd):
        fresh = jnp.logical_and(t >= CTX_TILES, (t - CTX_TILES) % TILES_PER_LAT == first_r)
        return jnp.where(t < CTX_TILES, 0.0, jnp.where(fresh, s0_ref[hd], prev))

    s_in = [enter(s_scr[hd], tiles[0], hd) for hd in range(H)]
    for p, t in enumerate(tiles[1:], start=1):
        s_in += [functools.partial(lambda done, p, t, hd: enter(done[(p - 1) * H + hd], t, hd),
                                   p=p, t=t, hd=hd) for hd in range(H)]
    return s_in


def _decay_operands(q, k, bcum, rev):
    qd, kd, ke, dec = [], [], [], []
    for c in range(N_CH):
        rows = slice(c * CHUNK, (c + 1) * CHUNK)
        b = bcum[rows, :]
        last = b[0:1, :] if rev else b[CHUNK - 1:CHUNK, :]
        qd.append((q[rows, :] * jnp.exp(b)).astype(BF16))
        kd.append((k[rows, :] * jnp.exp(-b)).astype(BF16))
        ke.append(k[rows, :] * jnp.exp(last - b))
        dec.append(jnp.exp(last))
    return (jnp.concatenate(qd, axis=0), jnp.concatenate(kd, axis=0),
            jnp.concatenate(ke, axis=0).T.astype(BF16), dec)


def _scan_tile(heads, s_in, rev):
    mask = _chunk_tri(rev)
    kcol = lax.broadcasted_iota(jnp.int32, (DK, TM), 1) // CHUNK
    order = range(N_CH - 1, -1, -1) if rev else range(N_CH)
    sc, kv = [], []
    for qh, kh, keth, vh, _ in heads:
        sc.append(lax.dot_general(qh, kh, (((1,), (1,)), ((), ())), preferred_element_type=F32))
        kst = jnp.concatenate([jnp.where(kcol == c, keth, jnp.zeros_like(keth)) for c in range(N_CH)],
                              axis=0)
        kv.append(jnp.dot(kst, vh, preferred_element_type=F32))
    o_intra = [jnp.dot(jnp.where(mask, s, 0.0).astype(BF16), hd[3], preferred_element_type=F32)
               for s, hd in zip(sc, heads)]
    outs, s_out = [], []
    for hd, (qh, _, _, _, dec_rows) in enumerate(heads):
        s, o = (s_in[hd](s_out) if callable(s_in[hd]) else s_in[hd]), [None] * N_CH
        for c in order:
            rows = slice(c * CHUNK, (c + 1) * CHUNK)
            o[c] = o_intra[hd][rows, :] + jnp.dot(qh[rows, :], s.astype(BF16),
                                                  preferred_element_type=F32)
            dec_col = jnp.broadcast_to(dec_rows[c], (DK, DK)).T
            s = s * jnp.concatenate([dec_col, dec_col], axis=1) + kv[hd][c * DK:(c + 1) * DK, :]
        outs.append(o)
        s_out.append(s)
    return outs, s_out


def _gla_fwd_kernel(*refs, n_x):
    x_refs = refs[:n_x]
    (mod_ref, win_ref, wga_ref, wgb_ref, bg_ref, s0_ref, _,
     of_ref, qd_ref, kd_ref, ket_ref, v_ref, r_ref, dec_ref, st_ref,
     wins, tri_scr, s_scr) = refs[n_x:]
    i = pl.program_id(0)

    @pl.when(i == 0)
    def _():
        _cast_rows(win_ref, wins, 128)
        tri_scr[0] = jnp.where(_chunk_tri(False), 1.0, 0.0).astype(BF16)
        tri_scr[1] = jnp.where(_chunk_tri(True), 1.0, 0.0).astype(BF16)
        s_scr[...] = jnp.zeros_like(s_scr)

    subs = range(GLA_FWD_TILES)
    tiles = [i * GLA_FWD_TILES + sub for sub in subs]
    rows = [slice(sub * TM, (sub + 1) * TM) for sub in subs]

    hs, gs = [], []
    for sub in subs:
        shift, scale = _mod_rows(mod_ref, tiles[sub] * TM, (0, 1))
        h = (_read_x(x_refs, tiles[sub], rows[sub]) * (1.0 + scale) + shift).astype(BF16)
        g_parts = []
        for d in range(2):
            za = jnp.dot(h, wga_ref[d].astype(BF16), preferred_element_type=F32)
            z = jnp.dot(za.astype(BF16), wgb_ref[d].astype(BF16), preferred_element_type=F32) + bg_ref[d]
            g = (jnp.minimum(z, 0.0) - jnp.log1p(jnp.exp(-jnp.abs(z)))) * (1.0 / GATE_NORM)
            g_hi = g.astype(BF16)
            g_parts.append((g_hi, (g - g_hi.astype(F32)).astype(BF16)))
        hs.append(h)
        gs.append(g_parts)

    qs, ks_, vs_ = [], [], []
    for sub in subs:
        proj = jnp.dot(hs[sub], wins[...], preferred_element_type=F32)
        qs.append(proj[:, :KW] * (DK ** -0.5))
        ks_.append(proj[:, KW:2 * KW])
        vs_.append(proj[:, 2 * KW:2 * KW + D].astype(BF16))
        v_ref[rows[sub], :] = vs_[sub]
        r_ref[rows[sub], :] = proj[:, 2 * KW + D:].astype(BF16)

    bcums = [[jnp.dot(tri_scr[d], g_hi, preferred_element_type=F32)
              + jnp.dot(tri_scr[d], g_lo, preferred_element_type=F32)
              for d, (g_hi, g_lo) in enumerate(gs[sub])] for sub in subs]

    heads = []
    for sub in subs:
        qd_b, kd_b, ket_b, dec_b = _decay_operands(qs[sub], ks_[sub], bcums[sub][1], True)
        qd_ref[rows[sub], :] = qd_b
        kd_ref[rows[sub], :] = kd_b
        ket_ref[:, rows[sub]] = ket_b
        dec_ref[sub] = jnp.concatenate(dec_b + [jnp.zeros((SUBLANE - N_CH, KW), F32)], axis=0)
        qd, kd, ket, dec = _decay_operands(qs[sub], ks_[sub], bcums[sub][0], False)
        for hd in range(H):
            kc, vc = slice(hd * DK, (hd + 1) * DK), slice(hd * DV, (hd + 1) * DV)
            heads.append((qd[:, kc], kd[:, kc], ket[kc, :], vs_[sub][:, vc], [e[:, kc] for e in dec]))

    outs, s_out = _scan_tile(heads, _scan_states_in(s_scr, s0_ref, tiles, False), False)
    for sub in subs:
        for hd in range(H):
            for c in range(N_CH):
                r0 = sub * TM + c * CHUNK
                of_ref[r0:r0 + CHUNK, hd * DV:(hd + 1) * DV] = outs[sub * H + hd][c].astype(BF16)
    for hd in range(H):
        s_scr[hd] = s_out[(GLA_FWD_TILES - 1) * H + hd]
    _emit_states(st_ref, [s_out[sub * H:(sub + 1) * H] for sub in subs], tiles[0])


def _gla_fwd(x, mods, l, j, w_in, w_ga, w_gb, b_g, state_gla, new_states):
    xs = _x_args(x)
    n, rows = GLA_FWD_TILES, GLA_FWD_TILES * TM
    tile = pl.BlockSpec((rows, D), lambda i: (i, 0))
    keys = pl.BlockSpec((rows, KW), lambda i: (i, 0))
    n_in = len(xs) + 7
    return pl.pallas_call(
        functools.partial(_gla_fwd_kernel, n_x=len(xs)),
        grid=(NTILE // n,),
        in_specs=_x_specs(len(xs) == 2, rows=rows) + [
            _mod_spec(l),
            _stacked((D, 2 * KW + 2 * D), j, True),
            _stacked((2, D, RANK), j),
            _stacked((2, RANK, KW), j),
            _stacked((2, 1, KW), j),
            pl.BlockSpec((None, None, None) + STATE, lambda i: (_lat_seq(i * n), j, 0, 0, 0, 0)),
            pl.BlockSpec(memory_space=pl.ANY),
        ],
        out_specs=[
            tile, keys, keys,
            pl.BlockSpec((KW, rows), lambda i: (0, i)),
            tile, tile,
            pl.BlockSpec((n, SUBLANE, KW), lambda i: (i, 0, 0)),
            _state_out_spec(lambda i: i, j, 0, n),
        ],
        out_shape=[
            jax.ShapeDtypeStruct((NTOK, D), BF16),
            jax.ShapeDtypeStruct((NTOK, KW), BF16),
            jax.ShapeDtypeStruct((NTOK, KW), BF16),
            jax.ShapeDtypeStruct((KW, NTOK), BF16),
            jax.ShapeDtypeStruct((NTOK, D), BF16),
            jax.ShapeDtypeStruct((NTOK, D), BF16),
            jax.ShapeDtypeStruct((NTILE, SUBLANE, KW), F32),
            jax.ShapeDtypeStruct(new_states.shape, F32),
        ],
        input_output_aliases={n_in - 1: 7},
        scratch_shapes=[pltpu.VMEM((D, 2 * KW + 2 * D), BF16), pltpu.VMEM((2, TM, TM), BF16),
                        pltpu.VMEM(STATE, F32)],
        compiler_params=_cparams("arbitrary"),
        name="gla_fwd",
    )(*xs, mods, w_in, w_ga, w_gb, b_g.reshape(b_g.shape[0], 2, 1, KW), state_gla, new_states)


def _gla_bwd_kernel(*refs, n_x):
    x_refs = refs[:n_x]
    (qd_ref, kd_ref, ket_ref, v_ref, dec_ref, s0_ref, of_ref, r_ref, mod_ref, gn_ref, wo_ref,
     g_ref, b_ref, _, o_ref, st_ref, wos, s_scr) = refs[n_x:]
    i = pl.program_id(0)
    n = GLA_BWD_TILES
    blk = NTILE // n - 1 - i

    @pl.when(i == 0)
    def _():
        _cast_rows(wo_ref, wos, 128)
        s_scr[...] = jnp.zeros_like(s_scr)

    order = list(range(n - 1, -1, -1))
    tiles = [blk * n + sub for sub in order]
    heads = []
    for sub in order:
        rows = slice(sub * TM, (sub + 1) * TM)
        for hd in range(H):
            kc, vc = slice(hd * DK, (hd + 1) * DK), slice(hd * DV, (hd + 1) * DV)
            heads.append((qd_ref[rows, kc], kd_ref[rows, kc], ket_ref[kc, rows], v_ref[rows, vc],
                          [dec_ref[sub, c:c + 1, kc] for c in range(N_CH)]))
    outs, s_out = _scan_tile(heads, _scan_states_in(s_scr, s0_ref, tiles, True), True)

    finals = [None] * n
    for p, sub in enumerate(order):
        rows = slice(sub * TM, (sub + 1) * TM)
        finals[sub] = s_out[p * H:(p + 1) * H]
        (gate,) = _mod_rows(mod_ref, tiles[p] * TM, (2,))
        parts = []
        for hd in range(H):
            oh = (of_ref[rows, hd * DV:(hd + 1) * DV].astype(F32)
                  + jnp.concatenate(outs[p * H + hd], axis=0))
            ms = jnp.mean(oh * oh, axis=-1, keepdims=True)
            parts.append(oh * lax.rsqrt(ms + RMS_EPS))
        on = jnp.concatenate(parts, axis=1) * gn_ref[...]
        y = jnp.dot((on * _silu(r_ref[rows, :].astype(F32))).astype(BF16), wos[...],
                    preferred_element_type=F32)
        o_ref[rows, :] = _layer_norm(ALPHA * _read_x(x_refs, tiles[p], rows) + gate * y,
                                     g_ref[...], b_ref[...])
    for hd in range(H):
        s_scr[hd] = s_out[(n - 1) * H + hd]
    _emit_states(st_ref, finals, blk * n)


def _gla_bwd(x, o_f, qd, kd, ket, v, r, dec, state_gla, new_states, mods, l, j, gn_g, w_o, ln_g, ln_b):
    xs = _x_args(x)
    n, rows = GLA_BWD_TILES, GLA_BWD_TILES * TM
    rblk = lambda i: NTILE // n - 1 - i
    tile = pl.BlockSpec((rows, D), lambda i: (rblk(i), 0))
    keys = pl.BlockSpec((rows, KW), lambda i: (rblk(i), 0))
    n_in = len(xs) + 14
    return pl.pallas_call(
        functools.partial(_gla_bwd_kernel, n_x=len(xs)),
        grid=(NTILE // n,),
        in_specs=_x_specs(len(xs) == 2, rblk, rows) + [
            keys, keys,
            pl.BlockSpec((KW, rows), lambda i: (0, rblk(i))),
            tile,
            pl.BlockSpec((n, SUBLANE, KW), lambda i: (rblk(i), 0, 0)),
            pl.BlockSpec((None, None, None) + STATE,
                         lambda i: (_lat_seq(rblk(i) * n), j, 1, 0, 0, 0)),
            tile, tile, _mod_spec(l),
            _stacked((1, D), j), _stacked((D, D), j, True),
            _stacked((1, D), 2 * l), _stacked((1, D), 2 * l),
            pl.BlockSpec(memory_space=pl.ANY)],
        out_specs=[tile, _state_out_spec(rblk, j, 1, n)],
        out_shape=[jax.ShapeDtypeStruct((NTOK, D), F32), jax.ShapeDtypeStruct(new_states.shape, F32)],
        input_output_aliases={n_in - 1: 1},
        scratch_shapes=[pltpu.VMEM((D, D), BF16), pltpu.VMEM(STATE, F32)],
        compiler_params=_cparams("arbitrary"),
        name="gla_bwd",
    )(*xs, qd, kd, ket, v, dec, state_gla, o_f, r, mods, _vec3(gn_g), w_o, _vec3(ln_g), _vec3(ln_b),
      new_states)


def kernel(x_prompt, x_sample, c, state_gla, c_ctx, mod_w, mod_b, ln_g, ln_b, ff_w1, ff_w2, gla_w_in, gla_w_ga, gla_w_gb, gla_b_g, gla_gn_g, gla_w_o, conf_w_pw1, conf_b_pw1, conf_w_dw, conf_b_dw, conf_ln_g, conf_ln_b, conf_w_pw2, conf_b_pw2, sc_w_in, sc_w_conv, sc_w_out):
    assert x_prompt.shape == (BATCH, SEQ, D) and x_sample.shape == (DEC_BATCH, DEC_SEQ, D)
    x = (x_prompt.reshape(NCTX, D), x_sample.reshape(NLAT, D))

    c8 = jnp.concatenate([c_ctx[None, :], c, jnp.zeros((SUBLANE - N_CVEC, D), F32)], axis=0)
    mods = _adaln(c8, mod_w, mod_b)

    states = jnp.zeros((BATCH, state_gla.shape[1], 2) + STATE, F32)
    for l in range(DEPTH):
        kind, j = l % 3, l // 3
        if kind == 0:
            o_f, qd, kd, ket, v, r, dec, states = _gla_fwd(x, mods, l, j, gla_w_in, gla_w_ga, gla_w_gb,
                                                           gla_b_g, state_gla, states)
            x, states = _gla_bwd(x, o_f, qd, kd, ket, v, r, dec, state_gla, states, mods, l, j,
                                 gla_gn_g, gla_w_o, ln_g, ln_b)
        elif kind == 2:
            x = _short_conv(x, mods, l, j, sc_w_in, sc_w_conv, sc_w_out, ln_g, ln_b)
        conformer = None
        if l + 1 < DEPTH and (l + 1) % 3 == 1:
            conformer = ((l + 1) // 3, conf_w_pw1, conf_b_pw1, conf_w_dw, conf_b_dw, conf_ln_g, conf_ln_b,
                         conf_w_pw2, conf_b_pw2)
        x = _mlp(x, mods, l, ff_w1, ff_w2, ln_g, ln_b, split_out=(l == DEPTH - 1), conformer=conformer)

    y_prompt, y_sample = x
    return (y_prompt.reshape(BATCH, SEQ, D), y_sample.reshape(DEC_BATCH, DEC_SEQ, D), states)
```

```python
import dataclasses
import functools

import jax
import jax.numpy as jnp
from jax import lax
from jax.experimental import pallas as pl
from jax.experimental.pallas import tpu as pltpu

F32 = jnp.float32
BF16 = jnp.bfloat16

D = 1024
DEPTH = 4
BATCH, SEQ = 16, 256
DEC_BATCH, DEC_SEQ = 2, 2048
GRID_W = 64
N_MOD = 6
N_CVEC = 1 + DEC_BATCH
NCTX = BATCH * SEQ
NLAT = DEC_BATCH * DEC_SEQ
NTOK = NCTX + NLAT
H, DK, DV = 4, 128, 256
KW = H * DK
RANK = 16
CHUNK = 64
GATE_NORM = 16.0
CONF_W = 31
D_FF = 4 * D
LN_EPS = 1e-5
RMS_EPS = 1e-6
ALPHA = (2 * DEPTH) ** 0.25

TM = 256
NTILE = NTOK // TM
CTX_TILES = NCTX // TM
TILES_PER_LAT = DEC_SEQ // TM
TM_MLP = 512
TF_MLP = 1024
TC_MLP = 512
TN_MOD = 2048
SUBLANE = 8
VMEM_LIMIT = 58 * 1024 * 1024


def _cparams(*sem):
    return pltpu.CompilerParams(dimension_semantics=sem, vmem_limit_bytes=VMEM_LIMIT)


def _mod_rows(mod_ref, row0, ks):
    m = jnp.where(row0 < NCTX, 0, 1 + (row0 - NCTX) // DEC_SEQ)
    return [mod_ref[pl.ds(m, 1), k * D:(k + 1) * D] for k in ks]


def _mod_spec(l):
    return pl.BlockSpec((None, SUBLANE, N_MOD * D), lambda *_: (l, 0, 0))


def _stacked(block, j, single_buffer=False):
    nd = len(block)
    mode = dict(pipeline_mode=pl.Buffered(1)) if single_buffer else {}
    return pl.BlockSpec((None,) + block, lambda *_: (j,) + (0,) * nd, **mode)


def _x_specs(split, block=lambda i: i, rows=TM):
    if not split:
        return [pl.BlockSpec((rows, D), lambda i: (block(i), 0))]
    nc, nl = NCTX // rows, NLAT // rows
    return [pl.BlockSpec((rows, D), lambda i: (jnp.clip(block(i), 0, nc - 1), 0)),
            pl.BlockSpec((rows, D), lambda i: (jnp.clip(block(i) - nc, 0, nl - 1), 0))]


def _x_args(x):
    return list(x) if isinstance(x, tuple) else [x]


def _read_x(x_refs, t, rows=slice(None)):
    if len(x_refs) == 1:
        return x_refs[0][rows, :]
    return jnp.where(t < CTX_TILES, x_refs[0][rows, :], x_refs[1][rows, :])


def _layer_norm(y, g, b):
    mu = jnp.mean(y, axis=-1, keepdims=True)
    yc = y - mu
    var = jnp.mean(yc * yc, axis=-1, keepdims=True)
    return yc * lax.rsqrt(var + LN_EPS) * g + b


def _silu(x):
    return x * jax.nn.sigmoid(x)


def _cast_rows(src_ref, dst_ref, step):
    n = src_ref.shape[0] // step

    def body(r, c):
        rows = pl.ds(pl.multiple_of(r * step, step), step)
        dst_ref[rows, :] = src_ref[rows, :].astype(BF16)
        return c

    lax.fori_loop(0, n, body, 0)


def _vec3(a):
    return a.reshape(-1, 1, a.shape[-1])


def _adaln_kernel(c_ref, w_ref, b_ref, o_ref):
    s = _silu(c_ref[...]).astype(BF16)
    o_ref[...] = jnp.dot(s, w_ref[...].astype(BF16), preferred_element_type=F32) + b_ref[...]


def _adaln(c8, mod_w, mod_b):
    return pl.pallas_call(
        _adaln_kernel,
        grid=(DEPTH, N_MOD * D // TN_MOD),
        in_specs=[
            pl.BlockSpec((SUBLANE, D), lambda l, n: (0, 0)),
            pl.BlockSpec((None, D, TN_MOD), lambda l, n: (l, 0, n)),
            pl.BlockSpec((None, 1, TN_MOD), lambda l, n: (l, 0, n)),
        ],
        out_specs=pl.BlockSpec((None, SUBLANE, TN_MOD), lambda l, n: (l, 0, n)),
        out_shape=jax.ShapeDtypeStruct((DEPTH, SUBLANE, N_MOD * D), F32),
        compiler_params=_cparams("parallel", "parallel"),
        name="adaln",
    )(c8, mod_w, _vec3(mod_b))


@dataclasses.dataclass(frozen=True)
class _MlpCfg:
    tm: int
    tf: int
    split_out: bool
    conformer: bool

    @property
    def nj(self):
        return D_FF // self.tf

    @property
    def nt(self):
        return NTOK // self.tm

    def tile(self, s):
        return jnp.clip(s - (self.nj - 1), 0, self.nt - 1)

    def done(self, s):
        return jnp.clip(s - self.nj, 0, self.nt - 1)


N_CONF_IN, N_CONF_SCRATCH = 11, 4


def _sqrelu(a):
    return jnp.square(jnp.maximum(a, 0.0)).astype(BF16)


def _mlp_kernel(*refs, cfg):
    it = iter(refs)
    take = lambda n: [next(it) for _ in range(n)]
    x_ref, xd_ref, mod_ref, w1_ref, w2_ref, g_ref, b_ref = take(7)
    conf_in = take(N_CONF_IN) if cfg.conformer else None
    o_refs = take(2 if cfg.split_out else 1)
    w1s, w2s, h_scr, acc_scr = take(4)
    y_scr = take(1)[0] if cfg.split_out else None
    conf_scr = take(N_CONF_SCRATCH) if cfg.conformer else None
    nj, nt, tm = cfg.nj, cfg.nt, cfg.tm

    s = pl.program_id(0)
    shift, scale = _mod_rows(mod_ref, cfg.tile(s) * tm, (3, 4))
    done = cfg.done(s)

    def finish(emit_matmuls=None):
        (gate,) = _mod_rows(mod_ref, done * tm, (5,))
        y = _layer_norm(ALPHA * xd_ref[...] + gate * acc_scr[...], g_ref[...], b_ref[...])
        if cfg.conformer:
            y = _conformer_tile(y, done, conf_in, conf_scr, emit_matmuls)
        if cfg.split_out:
            y_scr[...] = y
        else:
            o_refs[0][...] = y

    def route():
        if cfg.split_out:
            @pl.when(done < NCTX // tm)
            def _():
                o_refs[0][...] = y_scr[...]

            @pl.when(done >= NCTX // tm)
            def _():
                o_refs[1][...] = y_scr[...]

    @pl.when(s < nj)
    def _():
        w1s[s] = w1_ref[...].astype(BF16)
        w2s[s] = w2_ref[...].astype(BF16)
        if cfg.conformer:
            for src, dst in ((conf_in[1], conf_scr[0]), (conf_in[7], conf_scr[1])):
                rows = D // nj
                dst[pl.ds(pl.multiple_of(s * rows, rows), rows), :] = src[...].astype(BF16)

        @pl.when(s == 0)
        def _():
            h_scr[...] = (x_ref[...] * (1.0 + scale) + shift).astype(BF16)
            acc_scr[...] = jnp.zeros_like(acc_scr)

        a = _sqrelu(jnp.dot(h_scr[...], w1s[s], preferred_element_type=F32))
        acc_scr[...] += jnp.dot(a, w2s[s], preferred_element_type=F32)

    @pl.when(jnp.logical_and(s >= nj, s < nj + nt - 1))
    def _():
        h_scr[...] = (x_ref[...] * (1.0 + scale) + shift).astype(BF16)
        chunks = [(j, slice(c * TC_MLP, (c + 1) * TC_MLP))
                  for j in range(nj) for c in range(cfg.tf // TC_MLP)]
        queue = [("up", 0)]
        for k in range(len(chunks)):
            queue += ([("up", k + 1)] if k + 1 < len(chunks) else []) + [("down", k)]
        act, st = {}, dict(pos=0, acc=None)

        def emit_matmuls(n):
            for kind, k in queue[st["pos"]:st["pos"] + n]:
                j, cols = chunks[k]
                if kind == "up":
                    act[k] = _sqrelu(jnp.dot(h_scr[...], w1s[j, :, cols], preferred_element_type=F32))
                else:
                    part = jnp.dot(act.pop(k), w2s[j, cols, :], preferred_element_type=F32)
                    st["acc"] = part if st["acc"] is None else st["acc"] + part
            st["pos"] = min(st["pos"] + n, len(queue))

        emit_matmuls(2)
        finish(lambda: emit_matmuls(1))
        emit_matmuls(len(queue))
        acc_scr[...] = st["acc"]
        route()

    @pl.when(s == nj + nt - 1)
    def _():
        finish()
        route()


def _mlp(x, mods, l, w1, w2, ln_g, ln_b, split_out, conformer=None):
    cfg = _MlpCfg(tm=TM if conformer else TM_MLP, tf=TC_MLP if conformer else TF_MLP,
                  split_out=split_out, conformer=conformer is not None)
    tm, tf, nj = cfg.tm, cfg.tf, cfg.nj
    nc = NCTX // tm
    chunk = lambda s: jnp.minimum(s, nj - 1)
    in_specs = [
        pl.BlockSpec((tm, D), lambda s: (cfg.tile(s), 0)),
        pl.BlockSpec((tm, D), lambda s: (cfg.done(s), 0)),
        _mod_spec(l),
        pl.BlockSpec((None, D, tf), lambda s: (l, 0, chunk(s))),
        pl.BlockSpec((None, tf, D), lambda s: (l, chunk(s), 0)),
        _stacked((1, D), 2 * l + 1), _stacked((1, D), 2 * l + 1),
    ]
    args = [x, x, mods, w1, w2, _vec3(ln_g), _vec3(ln_b)]
    scratch = [pltpu.VMEM((nj, D, tf), BF16), pltpu.VMEM((nj, tf, D), BF16),
               pltpu.VMEM((tm, D), BF16), pltpu.VMEM((tm, D), F32)]
    if split_out:
        out_specs = [pl.BlockSpec((tm, D), lambda s: (jnp.minimum(cfg.done(s), nc - 1), 0)),
                     pl.BlockSpec((tm, D), lambda s: (jnp.maximum(cfg.done(s) - nc, 0), 0))]
        out_shape = [jax.ShapeDtypeStruct((NCTX, D), F32), jax.ShapeDtypeStruct((NLAT, D), F32)]
        scratch.append(pltpu.VMEM((tm, D), F32))
    else:
        out_specs = pl.BlockSpec((tm, D), lambda s: (cfg.done(s), 0))
        out_shape = jax.ShapeDtypeStruct((NTOK, D), F32)
    if conformer:
        j, cw1, cb1, wdw, bdw, cg, cb, cw2, cb2 = conformer
        slab = lambda cols: pl.BlockSpec((None, D // nj, cols), lambda s: (j, chunk(s), 0))
        in_specs += [_mod_spec(l + 1), slab(2 * D), _stacked((1, 2 * D), j), _stacked((CONF_W, D), j),
                     _stacked((1, D), j), _stacked((1, D), j), _stacked((1, D), j),
                     slab(D), _stacked((1, D), j),
                     _stacked((1, D), 2 * (l + 1)), _stacked((1, D), 2 * (l + 1))]
        args += [mods, cw1, _vec3(cb1), wdw, _vec3(bdw), _vec3(cg), _vec3(cb), cw2, _vec3(cb2),
                 _vec3(ln_g), _vec3(ln_b)]
        scratch += [pltpu.VMEM((D, 2 * D), BF16), pltpu.VMEM((D, D), BF16),
                    pltpu.VMEM((CONF_ROWS, CONF_LB), F32),
                    pltpu.VMEM((SUBLANE - 1, CONF_ROWS, CONF_LB), F32)]
        assert len(in_specs) == 7 + N_CONF_IN
    return pl.pallas_call(
        functools.partial(_mlp_kernel, cfg=cfg),
        grid=(nj + cfg.nt,),
        in_specs=in_specs,
        out_specs=out_specs,
        out_shape=out_shape,
        scratch_shapes=scratch,
        compiler_params=_cparams("arbitrary"),
        name="mlp_conformer" if conformer else "mlp",
    )(*args)


CONF_PAD = 16
CONF_LB = 256
CONF_ROWS = (TM // GRID_W) * (GRID_W + 2 * CONF_PAD)


def _conf_conv(upad_ref, shf_ref, w_taps, u, joined, between=None):
    nseg = TM // GRID_W
    stride = GRID_W + 2 * CONF_PAD
    zpad = jnp.zeros((CONF_PAD, CONF_LB), F32)
    for s in range(nseg):
        base, r0 = s * stride, s * GRID_W
        above = jnp.where(joined, u[r0 - CONF_PAD:r0, :], 0.0) if s > 0 else zpad
        below = jnp.where(joined, u[r0 + GRID_W:r0 + GRID_W + CONF_PAD, :], 0.0) if s < nseg - 1 else zpad
        upad_ref[base:base + CONF_PAD, :] = above
        upad_ref[base + CONF_PAD:base + CONF_PAD + GRID_W, :] = u[r0:r0 + GRID_W, :]
        upad_ref[base + CONF_PAD + GRID_W:base + stride, :] = below
    for b in range(1, SUBLANE):
        shf_ref[b - 1, 0:CONF_ROWS - SUBLANE, :] = upad_ref[b:b + CONF_ROWS - SUBLANE, :]
    out = []
    for s in range(nseg):
        if between is not None:
            between()
        r0 = s * stride + CONF_PAD
        acc = jnp.zeros((GRID_W, CONF_LB), F32)
        for k in range(CONF_W):
            a, b = divmod(k - CONF_W // 2, SUBLANE)
            rows = slice(r0 + SUBLANE * a, r0 + SUBLANE * a + GRID_W)
            acc = acc + w_taps[k] * (upad_ref[rows, :] if b == 0 else shf_ref[b - 1, rows, :])
        out.append(acc)
    return jnp.concatenate(out, axis=0)


def _conformer_tile(x, t, conf_in, conf_scr, between=None):
    mod_ref, _, b1_ref, wdw_ref, bdw_ref, cg_ref, cb_ref, _, b2_ref, g_ref, b_ref = conf_in
    w1s, w2s, upad, shf = conf_scr
    shift, scale, gate = _mod_rows(mod_ref, t * TM, (0, 1, 2))
    h = (x * (1.0 + scale) + shift).astype(BF16)

    def glu(lb):
        cols = slice(lb * CONF_LB, (lb + 1) * CONF_LB)
        gcols = slice(D + lb * CONF_LB, D + (lb + 1) * CONF_LB)
        a = jnp.dot(h, w1s[:, cols], preferred_element_type=F32) + b1_ref[:, cols]
        g = jnp.dot(h, w1s[:, gcols], preferred_element_type=F32) + b1_ref[:, gcols]
        return a * jax.nn.sigmoid(g)

    n_lb = D // CONF_LB
    conv = []
    u = glu(0)
    for lb in range(n_lb):
        u_next = glu(lb + 1) if lb + 1 < n_lb else None
        cols = slice(lb * CONF_LB, (lb + 1) * CONF_LB)
        taps = [wdw_ref[k:k + 1, cols] for k in range(CONF_W)]
        conv.append(_conf_conv(upad, shf, taps, u, t < CTX_TILES, between))
        u = u_next

    hook = between if between is not None else (lambda: None)
    hook()
    uc = jnp.concatenate(conv, axis=1) + bdw_ref[...]
    uc = _silu(_layer_norm(uc, cg_ref[...], cb_ref[...]))
    y = jnp.dot(uc.astype(BF16), w2s[...], preferred_element_type=F32) + b2_ref[...]
    hook()
    return _layer_norm(ALPHA * x + gate * y, g_ref[...], b_ref[...])


SC_TM = 2 * TM

def _sconv_kernel(x_ref, xp_ref, xn_ref, mod_ref, win_ref, wc_ref, wout_ref, g_ref, b_ref, o_ref,
                  wins, wouts, cpad):
    i = pl.program_id(0)
    shift, scale, gate = _mod_rows(mod_ref, i * SC_TM, (0, 1, 2))

    @pl.when(i == 0)
    def _():
        _cast_rows(win_ref, wins, 128)
        _cast_rows(wout_ref, wouts, 128)

    scale = 1.0 + scale
    w0, w1, w2 = wc_ref[0:1, :], wc_ref[1:2, :], wc_ref[2:3, :]

    def finish(x, h, y):
        bg = jnp.dot(h, wins[:, :D], preferred_element_type=F32)
        out = jnp.dot((bg * y).astype(BF16), wouts[...], preferred_element_type=F32)
        o_ref[...] = _layer_norm(ALPHA * x + gate * out, g_ref[...], b_ref[...])

    @pl.when(i < NCTX // SC_TM)
    def _():
        x = x_ref[...]
        h = (x * scale + shift).astype(BF16)
        cu = jnp.dot(h, wins[:, D:], preferred_element_type=F32)
        cu = cu[:, :D] * cu[:, D:]
        zrow = jnp.zeros((SUBLANE, D), F32)
        cpad[0:SUBLANE, :] = zrow
        cpad[SUBLANE:SUBLANE + SC_TM, :] = cu
        cpad[SUBLANE + SC_TM:2 * SUBLANE + SC_TM, :] = zrow
        pos = lax.broadcasted_iota(jnp.int32, (SC_TM, D), 0) % SEQ
        before = jnp.where(pos == 0, 0.0, cpad[SUBLANE - 1:SUBLANE - 1 + SC_TM, :])
        after = jnp.where(pos == SEQ - 1, 0.0, cpad[SUBLANE + 1:SUBLANE + 1 + SC_TM, :])
        finish(x, h, w0 * before + w1 * cu + w2 * after)

    @pl.when(i >= NCTX // SC_TM)
    def _():
        blocks_per_seq = DEC_SEQ // SC_TM
        r = (i - NCTX // SC_TM) % blocks_per_seq
        x = x_ref[...]
        h = (x * scale + shift).astype(BF16)
        hp = (xp_ref[...] * scale + shift).astype(BF16)
        hn = (xn_ref[...] * scale + shift).astype(BF16)
        hcat = jnp.concatenate([hp, h, hn], axis=0)
        cu = jnp.dot(hcat, wins[:, D:], preferred_element_type=F32)
        cu = cu[:, :D] * cu[:, D:]
        halo_up = jnp.where(r > 0, cu[0:GRID_W, :], 0.0)
        halo_dn = jnp.where(r < blocks_per_seq - 1, cu[GRID_W + SC_TM:, :], 0.0)
        up = jnp.concatenate([halo_up, cu[GRID_W:SC_TM, :]], axis=0)
        dn = jnp.concatenate([cu[2 * GRID_W:GRID_W + SC_TM, :], halo_dn], axis=0)
        finish(x, h, w0 * up + w1 * cu[GRID_W:GRID_W + SC_TM, :] + w2 * dn)


def _short_conv(x, mods, l, j, w_in, w_conv, w_out, ln_g, ln_b):
    halo_per_tile = SC_TM // GRID_W
    n_halo = NTOK // GRID_W
    return pl.pallas_call(
        _sconv_kernel,
        grid=(NTOK // SC_TM,),
        in_specs=[
            pl.BlockSpec((SC_TM, D), lambda i: (i, 0)),
            pl.BlockSpec((GRID_W, D), lambda i: (jnp.maximum(i * halo_per_tile - 1, 0), 0)),
            pl.BlockSpec((GRID_W, D), lambda i: (jnp.minimum((i + 1) * halo_per_tile, n_halo - 1), 0)),
            _mod_spec(l),
            _stacked((D, 3 * D), j, True),
            _stacked((3, D), j),
            _stacked((D, D), j, True),
            _stacked((1, D), 2 * l), _stacked((1, D), 2 * l),
        ],
        out_specs=pl.BlockSpec((SC_TM, D), lambda i: (i, 0)),
        out_shape=jax.ShapeDtypeStruct((NTOK, D), F32),
        scratch_shapes=[pltpu.VMEM((D, 3 * D), BF16), pltpu.VMEM((D, D), BF16),
                        pltpu.VMEM((SC_TM + 2 * SUBLANE, D), F32)],
        compiler_params=_cparams("arbitrary"),
        name="short_conv",
    )(x, x, x, mods, w_in, w_conv, w_out, _vec3(ln_g), _vec3(ln_b))


N_CH = TM // CHUNK
GLA_FWD_TILES = 1
GLA_BWD_TILES = 2
STATE = (H, DK, DV)


def _state_out_spec(block, j, d, n):
    return pl.BlockSpec((n, None, None) + STATE,
                        lambda i: (jnp.minimum(block(i), CTX_TILES // n - 1), j, d, 0, 0, 0))


def _emit_states(st_ref, finals, t0):
    @pl.when(t0 < CTX_TILES)
    def _():
        for sub, final in enumerate(finals):
            for hd in range(H):
                st_ref[sub, hd] = final[hd]


def _lat_seq(t):
    return jnp.clip((t - CTX_TILES) // TILES_PER_LAT, 0, DEC_BATCH - 1)


def _chunk_tri(rev):
    row = lax.broadcasted_iota(jnp.int32, (TM, TM), 0)
    col = lax.broadcasted_iota(jnp.int32, (TM, TM), 1)
    same = (row // CHUNK) == (col // CHUNK)
    return same & ((col >= row) if rev else (col <= row))


def _scan_states_in(s_scr, s0_ref, tiles, rev):
    first_r = TILES_PER_LAT - 1 if rev else 0

    def enter(prev, t, hd):
        fresh = jnp.logical_and(t >= CTX_TILES, (t - CTX_TILES) % TILES_PER_LAT == first_r)
        return jnp.where(t < CTX_TILES, 0.0, jnp.where(fresh, s0_ref[hd], prev))

    s_in = [enter(s_scr[hd], tiles[0], hd) for hd in range(H)]
    for p, t in enumerate(tiles[1:], start=1):
        s_in += [functools.partial(lambda done, p, t, hd: enter(done[(p - 1) * H + hd], t, hd),
                                   p=p, t=t, hd=hd) for hd in range(H)]
    return s_in


def _decay_operands(q, k, bcum, rev):
    qd, kd, ke, dec = [], [], [], []
    for c in range(N_CH):
        rows = slice(c * CHUNK, (c + 1) * CHUNK)
        b = bcum[rows, :]
        last = b[0:1, :] if rev else b[CHUNK - 1:CHUNK, :]
        qd.append((q[rows, :] * jnp.exp(b)).astype(BF16))
        kd.append((k[rows, :] * jnp.exp(-b)).astype(BF16))
        ke.append(k[rows, :] * jnp.exp(last - b))
        dec.append(jnp.exp(last))
    return (jnp.concatenate(qd, axis=0), jnp.concatenate(kd, axis=0),
            jnp.concatenate(ke, axis=0).T.astype(BF16), dec)


def _scan_tile(heads, s_in, rev):
    mask = _chunk_tri(rev)
    kcol = lax.broadcasted_iota(jnp.int32, (DK, TM), 1) // CHUNK
    order = range(N_CH - 1, -1, -1) if rev else range(N_CH)
    sc, kv = [], []
    for qh, kh, keth, vh, _ in heads:
        sc.append(lax.dot_general(qh, kh, (((1,), (1,)), ((), ())), preferred_element_type=F32))
        kst = jnp.concatenate([jnp.where(kcol == c, keth, jnp.zeros_like(keth)) for c in range(N_CH)],
                              axis=0)
        kv.append(jnp.dot(kst, vh, preferred_element_type=F32))
    o_intra = [jnp.dot(jnp.where(mask, s, 0.0).astype(BF16), hd[3], preferred_element_type=F32)
               for s, hd in zip(sc, heads)]
    outs, s_out = [], []
    for hd, (qh, _, _, _, dec_rows) in enumerate(heads):
        s, o = (s_in[hd](s_out) if callable(s_in[hd]) else s_in[hd]), [None] * N_CH
        for c in order:
            rows = slice(c * CHUNK, (c + 1) * CHUNK)
            o[c] = o_intra[hd][rows, :] + jnp.dot(qh[rows, :], s.astype(BF16),
                                                  preferred_element_type=F32)
            dec_col = jnp.broadcast_to(dec_rows[c], (DK, DK)).T
            s = s * jnp.concatenate([dec_col, dec_col], axis=1) + kv[hd][c * DK:(c + 1) * DK, :]
        outs.append(o)
        s_out.append(s)
    return outs, s_out


def _gla_fwd_kernel(*refs, n_x):
    x_refs = refs[:n_x]
    (mod_ref, win_ref, wga_ref, wgb_ref, bg_ref, s0_ref, _,
     of_ref, qd_ref, kd_ref, ket_ref, v_ref, r_ref, dec_ref, st_ref,
     wins, tri_scr, s_scr) = refs[n_x:]
    i = pl.program_id(0)

    @pl.when(i == 0)
    def _():
        _cast_rows(win_ref, wins, 128)
        tri_scr[0] = jnp.where(_chunk_tri(False), 1.0, 0.0).astype(BF16)
        tri_scr[1] = jnp.where(_chunk_tri(True), 1.0, 0.0).astype(BF16)
        s_scr[...] = jnp.zeros_like(s_scr)

    subs = range(GLA_FWD_TILES)
    tiles = [i * GLA_FWD_TILES + sub for sub in subs]
    rows = [slice(sub * TM, (sub + 1) * TM) for sub in subs]

    hs, gs = [], []
    for sub in subs:
        shift, scale = _mod_rows(mod_ref, tiles[sub] * TM, (0, 1))
        h = (_read_x(x_refs, tiles[sub], rows[sub]) * (1.0 + scale) + shift).astype(BF16)
        g_parts = []
        for d in range(2):
            za = jnp.dot(h, wga_ref[d].astype(BF16), preferred_element_type=F32)
            z = jnp.dot(za.astype(BF16), wgb_ref[d].astype(BF16), preferred_element_type=F32) + bg_ref[d]
            g = (jnp.minimum(z, 0.0) - jnp.log1p(jnp.exp(-jnp.abs(z)))) * (1.0 / GATE_NORM)
            g_hi = g.astype(BF16)
            g_parts.append((g_hi, (g - g_hi.astype(F32)).astype(BF16)))
        hs.append(h)
        gs.append(g_parts)

    qs, ks_, vs_ = [], [], []
    for sub in subs:
        proj = jnp.dot(hs[sub], wins[...], preferred_element_type=F32)
        qs.append(proj[:, :KW] * (DK ** -0.5))
        ks_.append(proj[:, KW:2 * KW])
        vs_.append(proj[:, 2 * KW:2 * KW + D].astype(BF16))
        v_ref[rows[sub], :] = vs_[sub]
        r_ref[rows[sub], :] = proj[:, 2 * KW + D:].astype(BF16)

    bcums = [[jnp.dot(tri_scr[d], g_hi, preferred_element_type=F32)
              + jnp.dot(tri_scr[d], g_lo, preferred_element_type=F32)
              for d, (g_hi, g_lo) in enumerate(gs[sub])] for sub in subs]

    heads = []
    for sub in subs:
        qd_b, kd_b, ket_b, dec_b = _decay_operands(qs[sub], ks_[sub], bcums[sub][1], True)
        qd_ref[rows[sub], :] = qd_b
        kd_ref[rows[sub], :] = kd_b
        ket_ref[:, rows[sub]] = ket_b
        dec_ref[sub] = jnp.concatenate(dec_b + [jnp.zeros((SUBLANE - N_CH, KW), F32)], axis=0)
        qd, kd, ket, dec = _decay_operands(qs[sub], ks_[sub], bcums[sub][0], False)
        for hd in range(H):
            kc, vc = slice(hd * DK, (hd + 1) * DK), slice(hd * DV, (hd + 1) * DV)
            heads.append((qd[:, kc], kd[:, kc], ket[kc, :], vs_[sub][:, vc], [e[:, kc] for e in dec]))

    outs, s_out = _scan_tile(heads, _scan_states_in(s_scr, s0_ref, tiles, False), False)
    for sub in subs:
        for hd in range(H):
            for c in range(N_CH):
                r0 = sub * TM + c * CHUNK
                of_ref[r0:r0 + CHUNK, hd * DV:(hd + 1) * DV] = outs[sub * H + hd][c].astype(BF16)
    for hd in range(H):
        s_scr[hd] = s_out[(GLA_FWD_TILES - 1) * H + hd]
    _emit_states(st_ref, [s_out[sub * H:(sub + 1) * H] for sub in subs], tiles[0])


def _gla_fwd(x, mods, l, j, w_in, w_ga, w_gb, b_g, state_gla, new_states):
    xs = _x_args(x)
    n, rows = GLA_FWD_TILES, GLA_FWD_TILES * TM
    tile = pl.BlockSpec((rows, D), lambda i: (i, 0))
    keys = pl.BlockSpec((rows, KW), lambda i: (i, 0))
    n_in = len(xs) + 7
    return pl.pallas_call(
        functools.partial(_gla_fwd_kernel, n_x=len(xs)),
        grid=(NTILE // n,),
        in_specs=_x_specs(len(xs) == 2, rows=rows) + [
            _mod_spec(l),
            _stacked((D, 2 * KW + 2 * D), j, True),
            _stacked((2, D, RANK), j),
            _stacked((2, RANK, KW), j),
            _stacked((2, 1, KW), j),
            pl.BlockSpec((None, None, None) + STATE, lambda i: (_lat_seq(i * n), j, 0, 0, 0, 0)),
            pl.BlockSpec(memory_space=pl.ANY),
        ],
        out_specs=[
            tile, keys, keys,
            pl.BlockSpec((KW, rows), lambda i: (0, i)),
            tile, tile,
            pl.BlockSpec((n, SUBLANE, KW), lambda i: (i, 0, 0)),
            _state_out_spec(lambda i: i, j, 0, n),
        ],
        out_shape=[
            jax.ShapeDtypeStruct((NTOK, D), BF16),
            jax.ShapeDtypeStruct((NTOK, KW), BF16),
            jax.ShapeDtypeStruct((NTOK, KW), BF16),
            jax.ShapeDtypeStruct((KW, NTOK), BF16),
            jax.ShapeDtypeStruct((NTOK, D), BF16),
            jax.ShapeDtypeStruct((NTOK, D), BF16),
            jax.ShapeDtypeStruct((NTILE, SUBLANE, KW), F32),
            jax.ShapeDtypeStruct(new_states.shape, F32),
        ],
        input_output_aliases={n_in - 1: 7},
        scratch_shapes=[pltpu.VMEM((D, 2 * KW + 2 * D), BF16), pltpu.VMEM((2, TM, TM), BF16),
                        pltpu.VMEM(STATE, F32)],
        compiler_params=_cparams("arbitrary"),
        name="gla_fwd",
    )(*xs, mods, w_in, w_ga, w_gb, b_g.reshape(b_g.shape[0], 2, 1, KW), state_gla, new_states)


def _gla_bwd_kernel(*refs, n_x):
    x_refs = refs[:n_x]
    (qd_ref, kd_ref, ket_ref, v_ref, dec_ref, s0_ref, of_ref, r_ref, mod_ref, gn_ref, wo_ref,
     g_ref, b_ref, _, o_ref, st_ref, wos, s_scr) = refs[n_x:]
    i = pl.program_id(0)
    n = GLA_BWD_TILES
    blk = NTILE // n - 1 - i

    @pl.when(i == 0)
    def _():
        _cast_rows(wo_ref, wos, 128)
        s_scr[...] = jnp.zeros_like(s_scr)

    order = list(range(n - 1, -1, -1))
    tiles = [blk * n + sub for sub in order]
    heads = []
    for sub in order:
        rows = slice(sub * TM, (sub + 1) * TM)
        for hd in range(H):
            kc, vc = slice(hd * DK, (hd + 1) * DK), slice(hd * DV, (hd + 1) * DV)
            heads.append((qd_ref[rows, kc], kd_ref[rows, kc], ket_ref[kc, rows], v_ref[rows, vc],
                          [dec_ref[sub, c:c + 1, kc] for c in range(N_CH)]))
    outs, s_out = _scan_tile(heads, _scan_states_in(s_scr, s0_ref, tiles, True), True)

    finals = [None] * n
    for p, sub in enumerate(order):
        rows = slice(sub * TM, (sub + 1) * TM)
        finals[sub] = s_out[p * H:(p + 1) * H]
        (gate,) = _mod_rows(mod_ref, tiles[p] * TM, (2,))
        parts = []
        for hd in range(H):
            oh = (of_ref[rows, hd * DV:(hd + 1) * DV].astype(F32)
                  + jnp.concatenate(outs[p * H + hd], axis=0))
            ms = jnp.mean(oh * oh, axis=-1, keepdims=True)
            parts.append(oh * lax.rsqrt(ms + RMS_EPS))
        on = jnp.concatenate(parts, axis=1) * gn_ref[...]
        y = jnp.dot((on * _silu(r_ref[rows, :].astype(F32))).astype(BF16), wos[...],
                    preferred_element_type=F32)
        o_ref[rows, :] = _layer_norm(ALPHA * _read_x(x_refs, tiles[p], rows) + gate * y,
                                     g_ref[...], b_ref[...])
    for hd in range(H):
        s_scr[hd] = s_out[(n - 1) * H + hd]
    _emit_states(st_ref, finals, blk * n)


def _gla_bwd(x, o_f, qd, kd, ket, v, r, dec, state_gla, new_states, mods, l, j, gn_g, w_o, ln_g, ln_b):
    xs = _x_args(x)
    n, rows = GLA_BWD_TILES, GLA_BWD_TILES * TM
    rblk = lambda i: NTILE // n - 1 - i
    tile = pl.BlockSpec((rows, D), lambda i: (rblk(i), 0))
    keys = pl.BlockSpec((rows, KW), lambda i: (rblk(i), 0))
    n_in = len(xs) + 14
    return pl.pallas_call(
        functools.partial(_gla_bwd_kernel, n_x=len(xs)),
        grid=(NTILE // n,),
        in_specs=_x_specs(len(xs) == 2, rblk, rows) + [
            keys, keys,
            pl.BlockSpec((KW, rows), lambda i: (0, rblk(i))),
            tile,
            pl.BlockSpec((n, SUBLANE, KW), lambda i: (rblk(i), 0, 0)),
            pl.BlockSpec((None, None, None) + STATE,
                         lambda i: (_lat_seq(rblk(i) * n), j, 1, 0, 0, 0)),
            tile, tile, _mod_spec(l),
            _stacked((1, D), j), _stacked((D, D), j, True),
            _stacked((1, D), 2 * l), _stacked((1, D), 2 * l),
            pl.BlockSpec(memory_space=pl.ANY)],
        out_specs=[tile, _state_out_spec(rblk, j, 1, n)],
        out_shape=[jax.ShapeDtypeStruct((NTOK, D), F32), jax.ShapeDtypeStruct(new_states.shape, F32)],
        input_output_aliases={n_in - 1: 1},
        scratch_shapes=[pltpu.VMEM((D, D), BF16), pltpu.VMEM(STATE, F32)],
        compiler_params=_cparams("arbitrary"),
        name="gla_bwd",
    )(*xs, qd, kd, ket, v, dec, state_gla, o_f, r, mods, _vec3(gn_g), w_o, _vec3(ln_g), _vec3(ln_b),
      new_states)


def kernel(x_prompt, x_sample, c, state_gla, c_ctx, mod_w, mod_b, ln_g, ln_b, ff_w1, ff_w2, gla_w_in, gla_w_ga, gla_w_gb, gla_b_g, gla_gn_g, gla_w_o, conf_w_pw1, conf_b_pw1, conf_w_dw, conf_b_dw, conf_ln_g, conf_ln_b, conf_w_pw2, conf_b_pw2, sc_w_in, sc_w_conv, sc_w_out):
    assert x_prompt.shape == (BATCH, SEQ, D) and x_sample.shape == (DEC_BATCH, DEC_SEQ, D)
    x = (x_prompt.reshape(NCTX, D), x_sample.reshape(NLAT, D))

    c8 = jnp.concatenate([c_ctx[None, :], c, jnp.zeros((SUBLANE - N_CVEC, D), F32)], axis=0)
    mods = _adaln(c8, mod_w, mod_b)

    states = jnp.zeros((BATCH, state_gla.shape[1], 2) + STATE, F32)
    for l in range(DEPTH):
        kind, j = l % 3, l // 3
        if kind == 0:
            o_f, qd, kd, ket, v, r, dec, states = _gla_fwd(x, mods, l, j, gla_w_in, gla_w_ga, gla_w_gb,
                                                           gla_b_g, state_gla, states)
            x, states = _gla_bwd(x, o_f, qd, kd, ket, v, r, dec, state_gla, states, mods, l, j,
                                 gla_gn_g, gla_w_o, ln_g, ln_b)
        elif kind == 2:
            x = _short_conv(x, mods, l, j, sc_w_in, sc_w_conv, sc_w_out, ln_g, ln_b)
        conformer = None
        if l + 1 < DEPTH and (l + 1) % 3 == 1:
            conformer = ((l + 1) // 3, conf_w_pw1, conf_b_pw1, conf_w_dw, conf_b_dw, conf_ln_g, conf_ln_b,
                         conf_w_pw2, conf_b_pw2)
        x = _mlp(x, mods, l, ff_w1, ff_w2, ln_g, ln_b, split_out=(l == DEPTH - 1), conformer=conformer)

    y_prompt, y_sample = x
    return (y_prompt.reshape(BATCH, SEQ, D), y_sample.reshape(DEC_BATCH, DEC_SEQ, D), states)
```

```python
import dataclasses
import functools

import jax
import jax.numpy as jnp
from jax import lax
from jax.experimental import pallas as pl
from jax.experimental.pallas import tpu as pltpu

F32 = jnp.float32
BF16 = jnp.bfloat16

D = 1024
DEPTH = 4
BATCH, SEQ = 16, 256
DEC_BATCH, DEC_SEQ = 2, 2048
GRID_W = 64
N_MOD = 6
N_CVEC = 1 + DEC_BATCH
NCTX = BATCH * SEQ
NLAT = DEC_BATCH * DEC_SEQ
NTOK = NCTX + NLAT
H, DK, DV = 4, 128, 256
KW = H * DK
RANK = 16
CHUNK = 64
GATE_NORM = 16.0
CONF_W = 31
D_FF = 4 * D
LN_EPS = 1e-5
RMS_EPS = 1e-6
ALPHA = (2 * DEPTH) ** 0.25

TM = 256
NTILE = NTOK // TM
CTX_TILES = NCTX // TM
TILES_PER_LAT = DEC_SEQ // TM
TM_MLP = 512
TF_MLP = 1024
TC_MLP = 512
TN_MOD = 2048
SUBLANE = 8
VMEM_LIMIT = 58 * 1024 * 1024


def _cparams(*sem):
    return pltpu.CompilerParams(dimension_semantics=sem, vmem_limit_bytes=VMEM_LIMIT)


def _mod_rows(mod_ref, row0, ks):
    m = jnp.where(row0 < NCTX, 0, 1 + (row0 - NCTX) // DEC_SEQ)
    return [mod_ref[pl.ds(m, 1), k * D:(k + 1) * D] for k in ks]


def _mod_spec(l):
    return pl.BlockSpec((None, SUBLANE, N_MOD * D), lambda *_: (l, 0, 0))


def _stacked(block, j, single_buffer=False):
    nd = len(block)
    mode = dict(pipeline_mode=pl.Buffered(1)) if single_buffer else {}
    return pl.BlockSpec((None,) + block, lambda *_: (j,) + (0,) * nd, **mode)


def _x_specs(split, block=lambda i: i, rows=TM):
    if not split:
        return [pl.BlockSpec((rows, D), lambda i: (block(i), 0))]
    nc, nl = NCTX // rows, NLAT // rows
    return [pl.BlockSpec((rows, D), lambda i: (jnp.clip(block(i), 0, nc - 1), 0)),
            pl.BlockSpec((rows, D), lambda i: (jnp.clip(block(i) - nc, 0, nl - 1), 0))]


def _x_args(x):
    return list(x) if isinstance(x, tuple) else [x]


def _read_x(x_refs, t, rows=slice(None)):
    if len(x_refs) == 1:
        return x_refs[0][rows, :]
    return jnp.where(t < CTX_TILES, x_refs[0][rows, :], x_refs[1][rows, :])


def _layer_norm(y, g, b):
    mu = jnp.mean(y, axis=-1, keepdims=True)
    yc = y - mu
    var = jnp.mean(yc * yc, axis=-1, keepdims=True)
    return yc * lax.rsqrt(var + LN_EPS) * g + b


def _silu(x):
    return x * jax.nn.sigmoid(x)


def _cast_rows(src_ref, dst_ref, step):
    n = src_ref.shape[0] // step

    def body(r, c):
        rows = pl.ds(pl.multiple_of(r * step, step), step)
        dst_ref[rows, :] = src_ref[rows, :].astype(BF16)
        return c

    lax.fori_loop(0, n, body, 0)


def _vec3(a):
    return a.reshape(-1, 1, a.shape[-1])


def _adaln_kernel(c_ref, w_ref, b_ref, o_ref):
    s = _silu(c_ref[...]).astype(BF16)
    o_ref[...] = jnp.dot(s, w_ref[...].astype(BF16), preferred_element_type=F32) + b_ref[...]


def _adaln(c8, mod_w, mod_b):
    return pl.pallas_call(
        _adaln_kernel,
        grid=(DEPTH, N_MOD * D // TN_MOD),
        in_specs=[
            pl.BlockSpec((SUBLANE, D), lambda l, n: (0, 0)),
            pl.BlockSpec((None, D, TN_MOD), lambda l, n: (l, 0, n)),
            pl.BlockSpec((None, 1, TN_MOD), lambda l, n: (l, 0, n)),
        ],
        out_specs=pl.BlockSpec((None, SUBLANE, TN_MOD), lambda l, n: (l, 0, n)),
        out_shape=jax.ShapeDtypeStruct((DEPTH, SUBLANE, N_MOD * D), F32),
        compiler_params=_cparams("parallel", "parallel"),
        name="adaln",
    )(c8, mod_w, _vec3(mod_b))


@dataclasses.dataclass(frozen=True)
class _MlpCfg:
    tm: int
    tf: int
    split_out: bool
    conformer: bool

    @property
    def nj(self):
        return D_FF // self.tf

    @property
    def nt(self):
        return NTOK // self.tm

    def tile(self, s):
        return jnp.clip(s - (self.nj - 1), 0, self.nt - 1)

    def done(self, s):
        return jnp.clip(s - self.nj, 0, self.nt - 1)


N_CONF_IN, N_CONF_SCRATCH = 11, 4


def _sqrelu(a):
    return jnp.square(jnp.maximum(a, 0.0)).astype(BF16)


def _mlp_kernel(*refs, cfg):
    it = iter(refs)
    take = lambda n: [next(it) for _ in range(n)]
    x_ref, xd_ref, mod_ref, w1_ref, w2_ref, g_ref, b_ref = take(7)
    conf_in = take(N_CONF_IN) if cfg.conformer else None
    o_refs = take(2 if cfg.split_out else 1)
    w1s, w2s, h_scr, acc_scr = take(4)
    y_scr = take(1)[0] if cfg.split_out else None
    conf_scr = take(N_CONF_SCRATCH) if cfg.conformer else None
    nj, nt, tm = cfg.nj, cfg.nt, cfg.tm

    s = pl.program_id(0)
    shift, scale = _mod_rows(mod_ref, cfg.tile(s) * tm, (3, 4))
    done = cfg.done(s)

    def finish(emit_matmuls=None):
        (gate,) = _mod_rows(mod_ref, done * tm, (5,))
        y = _layer_norm(ALPHA * xd_ref[...] + gate * acc_scr[...], g_ref[...], b_ref[...])
        if cfg.conformer:
            y = _conformer_tile(y, done, conf_in, conf_scr, emit_matmuls)
        if cfg.split_out:
            y_scr[...] = y
        else:
            o_refs[0][...] = y

    def route():
        if cfg.split_out:
            @pl.when(done < NCTX // tm)
            def _():
                o_refs[0][...] = y_scr[...]

            @pl.when(done >= NCTX // tm)
            def _():
                o_refs[1][...] = y_scr[...]

    @pl.when(s < nj)
    def _():
        w1s[s] = w1_ref[...].astype(BF16)
        w2s[s] = w2_ref[...].astype(BF16)
        if cfg.conformer:
            for src, dst in ((conf_in[1], conf_scr[0]), (conf_in[7], conf_scr[1])):
                rows = D // nj
                dst[pl.ds(pl.multiple_of(s * rows, rows), rows), :] = src[...].astype(BF16)

        @pl.when(s == 0)
        def _():
            h_scr[...] = (x_ref[...] * (1.0 + scale) + shift).astype(BF16)
            acc_scr[...] = jnp.zeros_like(acc_scr)

        a = _sqrelu(jnp.dot(h_scr[...], w1s[s], preferred_element_type=F32))
        acc_scr[...] += jnp.dot(a, w2s[s], preferred_element_type=F32)

    @pl.when(jnp.logical_and(s >= nj, s < nj + nt - 1))
    def _():
        h_scr[...] = (x_ref[...] * (1.0 + scale) + shift).astype(BF16)
        chunks = [(j, slice(c * TC_MLP, (c + 1) * TC_MLP))
                  for j in range(nj) for c in range(cfg.tf // TC_MLP)]
        queue = [("up", 0)]
        for k in range(len(chunks)):
            queue += ([("up", k + 1)] if k + 1 < len(chunks) else []) + [("down", k)]
        act, st = {}, dict(pos=0, acc=None)

        def emit_matmuls(n):
            for kind, k in queue[st["pos"]:st["pos"] + n]:
                j, cols = chunks[k]
                if kind == "up":
                    act[k] = _sqrelu(jnp.dot(h_scr[...], w1s[j, :, cols], preferred_element_type=F32))
                else:
                    part = jnp.dot(act.pop(k), w2s[j, cols, :], preferred_element_type=F32)
                    st["acc"] = part if st["acc"] is None else st["acc"] + part
            st["pos"] = min(st["pos"] + n, len(queue))

        emit_matmuls(2)
        finish(lambda: emit_matmuls(1))
        emit_matmuls(len(queue))
        acc_scr[...] = st["acc"]
        route()

    @pl.when(s == nj + nt - 1)
    def _():
        finish()
        route()


def _mlp(x, mods, l, w1, w2, ln_g, ln_b, split_out, conformer=None):
    cfg = _MlpCfg(tm=TM if conformer else TM_MLP, tf=TC_MLP if conformer else TF_MLP,
                  split_out=split_out, conformer=conformer is not None)
    tm, tf, nj = cfg.tm, cfg.tf, cfg.nj
    nc = NCTX // tm
    chunk = lambda s: jnp.minimum(s, nj - 1)
    in_specs = [
        pl.BlockSpec((tm, D), lambda s: (cfg.tile(s), 0)),
        pl.BlockSpec((tm, D), lambda s: (cfg.done(s), 0)),
        _mod_spec(l),
        pl.BlockSpec((None, D, tf), lambda s: (l, 0, chunk(s))),
        pl.BlockSpec((None, tf, D), lambda s: (l, chunk(s), 0)),
        _stacked((1, D), 2 * l + 1), _stacked((1, D), 2 * l + 1),
    ]
    args = [x, x, mods, w1, w2, _vec3(ln_g), _vec3(ln_b)]
    scratch = [pltpu.VMEM((nj, D, tf), BF16), pltpu.VMEM((nj, tf, D), BF16),
               pltpu.VMEM((tm, D), BF16), pltpu.VMEM((tm, D), F32)]
    if split_out:
        out_specs = [pl.BlockSpec((tm, D), lambda s: (jnp.minimum(cfg.done(s), nc - 1), 0)),
                     pl.BlockSpec((tm, D), lambda s: (jnp.maximum(cfg.done(s) - nc, 0), 0))]
        out_shape = [jax.ShapeDtypeStruct((NCTX, D), F32), jax.ShapeDtypeStruct((NLAT, D), F32)]
        scratch.append(pltpu.VMEM((tm, D), F32))
    else:
        out_specs = pl.BlockSpec((tm, D), lambda s: (cfg.done(s), 0))
        out_shape = jax.ShapeDtypeStruct((NTOK, D), F32)
    if conformer:
        j, cw1, cb1, wdw, bdw, cg, cb, cw2, cb2 = conformer
        slab = lambda cols: pl.BlockSpec((None, D // nj, cols), lambda s: (j, chunk(s), 0))
        in_specs += [_mod_spec(l + 1), slab(2 * D), _stacked((1, 2 * D), j), _stacked((CONF_W, D), j),
                     _stacked((1, D), j), _stacked((1, D), j), _stacked((1, D), j),
                     slab(D), _stacked((1, D), j),
                     _stacked((1, D), 2 * (l + 1)), _stacked((1, D), 2 * (l + 1))]
        args += [mods, cw1, _vec3(cb1), wdw, _vec3(bdw), _vec3(cg), _vec3(cb), cw2, _vec3(cb2),
                 _vec3(ln_g), _vec3(ln_b)]
        scratch += [pltpu.VMEM((D, 2 * D), BF16), pltpu.VMEM((D, D), BF16),
                    pltpu.VMEM((CONF_ROWS, CONF_LB), F32),
                    pltpu.VMEM((SUBLANE - 1, CONF_ROWS, CONF_LB), F32)]
        assert len(in_specs) == 7 + N_CONF_IN
    return pl.pallas_call(
        functools.partial(_mlp_kernel, cfg=cfg),
        grid=(nj + cfg.nt,),
        in_specs=in_specs,
        out_specs=out_specs,
        out_shape=out_shape,
        scratch_shapes=scratch,
        compiler_params=_cparams("arbitrary"),
        name="mlp_conformer" if conformer else "mlp",
    )(*args)


CONF_PAD = 16
CONF_LB = 256
CONF_ROWS = (TM // GRID_W) * (GRID_W + 2 * CONF_PAD)


def _conf_conv(upad_ref, shf_ref, w_taps, u, joined, between=None):
    nseg = TM // GRID_W
    stride = GRID_W + 2 * CONF_PAD
    zpad = jnp.zeros((CONF_PAD, CONF_LB), F32)
    for s in range(nseg):
        base, r0 = s * stride, s * GRID_W
        above = jnp.where(joined, u[r0 - CONF_PAD:r0, :], 0.0) if s > 0 else zpad
        below = jnp.where(joined, u[r0 + GRID_W:r0 + GRID_W + CONF_PAD, :], 0.0) if s < nseg - 1 else zpad
        upad_ref[base:base + CONF_PAD, :] = above
        upad_ref[base + CONF_PAD:base + CONF_PAD + GRID_W, :] = u[r0:r0 + GRID_W, :]
        upad_ref[base + CONF_PAD + GRID_W:base + stride, :] = below
    for b in range(1, SUBLANE):
        shf_ref[b - 1, 0:CONF_ROWS - SUBLANE, :] = upad_ref[b:b + CONF_ROWS - SUBLANE, :]
    out = []
    for s in range(nseg):
        if between is not None:
            between()
        r0 = s * stride + CONF_PAD
        acc = jnp.zeros((GRID_W, CONF_LB), F32)
        for k in range(CONF_W):
            a, b = divmod(k - CONF_W // 2, SUBLANE)
            rows = slice(r0 + SUBLANE * a, r0 + SUBLANE * a + GRID_W)
            acc = acc + w_taps[k] * (upad_ref[rows, :] if b == 0 else shf_ref[b - 1, rows, :])
        out.append(acc)
    return jnp.concatenate(out, axis=0)


def _conformer_tile(x, t, conf_in, conf_scr, between=None):
    mod_ref, _, b1_ref, wdw_ref, bdw_ref, cg_ref, cb_ref, _, b2_ref, g_ref, b_ref = conf_in
    w1s, w2s, upad, shf = conf_scr
    shift, scale, gate = _mod_rows(mod_ref, t * TM, (0, 1, 2))
    h = (x * (1.0 + scale) + shift).astype(BF16)

    def glu(lb):
        cols = slice(lb * CONF_LB, (lb + 1) * CONF_LB)
        gcols = slice(D + lb * CONF_LB, D + (lb + 1) * CONF_LB)
        a = jnp.dot(h, w1s[:, cols], preferred_element_type=F32) + b1_ref[:, cols]
        g = jnp.dot(h, w1s[:, gcols], preferred_element_type=F32) + b1_ref[:, gcols]
        return a * jax.nn.sigmoid(g)

    n_lb = D // CONF_LB
    conv = []
    u = glu(0)
    for lb in range(n_lb):
        u_next = glu(lb + 1) if lb + 1 < n_lb else None
        cols = slice(lb * CONF_LB, (lb + 1) * CONF_LB)
        taps = [wdw_ref[k:k + 1, cols] for k in range(CONF_W)]
        conv.append(_conf_conv(upad, shf, taps, u, t < CTX_TILES, between))
        u = u_next

    hook = between if between is not None else (lambda: None)
    hook()
    uc = jnp.concatenate(conv, axis=1) + bdw_ref[...]
    uc = _silu(_layer_norm(uc, cg_ref[...], cb_ref[...]))
    y = jnp.dot(uc.astype(BF16), w2s[...], preferred_element_type=F32) + b2_ref[...]
    hook()
    return _layer_norm(ALPHA * x + gate * y, g_ref[...], b_ref[...])


SC_TM = 2 * TM

def _sconv_kernel(x_ref, xp_ref, xn_ref, mod_ref, win_ref, wc_ref, wout_ref, g_ref, b_ref, o_ref,
                  wins, wouts, cpad):
    i = pl.program_id(0)
    shift, scale, gate = _mod_rows(mod_ref, i * SC_TM, (0, 1, 2))

    @pl.when(i == 0)
    def _():
        _cast_rows(win_ref, wins, 128)
        _cast_rows(wout_ref, wouts, 128)

    scale = 1.0 + scale
    w0, w1, w2 = wc_ref[0:1, :], wc_ref[1:2, :], wc_ref[2:3, :]

    def finish(x, h, y):
        bg = jnp.dot(h, wins[:, :D], preferred_element_type=F32)
        out = jnp.dot((bg * y).astype(BF16), wouts[...], preferred_element_type=F32)
        o_ref[...] = _layer_norm(ALPHA * x + gate * out, g_ref[...], b_ref[...])

    @pl.when(i < NCTX // SC_TM)
    def _():
        x = x_ref[...]
        h = (x * scale + shift).astype(BF16)
        cu = jnp.dot(h, wins[:, D:], preferred_element_type=F32)
        cu = cu[:, :D] * cu[:, D:]
        zrow = jnp.zeros((SUBLANE, D), F32)
        cpad[0:SUBLANE, :] = zrow
        cpad[SUBLANE:SUBLANE + SC_TM, :] = cu
        cpad[SUBLANE + SC_TM:2 * SUBLANE + SC_TM, :] = zrow
        pos = lax.broadcasted_iota(jnp.int32, (SC_TM, D), 0) % SEQ
        before = jnp.where(pos == 0, 0.0, cpad[SUBLANE - 1:SUBLANE - 1 + SC_TM, :])
        after = jnp.where(pos == SEQ - 1, 0.0, cpad[SUBLANE + 1:SUBLANE + 1 + SC_TM, :])
        finish(x, h, w0 * before + w1 * cu + w2 * after)

    @pl.when(i >= NCTX // SC_TM)
    def _():
        blocks_per_seq = DEC_SEQ // SC_TM
        r = (i - NCTX // SC_TM) % blocks_per_seq
        x = x_ref[...]
        h = (x * scale + shift).astype(BF16)
        hp = (xp_ref[...] * scale + shift).astype(BF16)
        hn = (xn_ref[...] * scale + shift).astype(BF16)
        hcat = jnp.concatenate([hp, h, hn], axis=0)
        cu = jnp.dot(hcat, wins[:, D:], preferred_element_type=F32)
        cu = cu[:, :D] * cu[:, D:]
        halo_up = jnp.where(r > 0, cu[0:GRID_W, :], 0.0)
        halo_dn = jnp.where(r < blocks_per_seq - 1, cu[GRID_W + SC_TM:, :], 0.0)
        up = jnp.concatenate([halo_up, cu[GRID_W:SC_TM, :]], axis=0)
        dn = jnp.concatenate([cu[2 * GRID_W:GRID_W + SC_TM, :], halo_dn], axis=0)
        finish(x, h, w0 * up + w1 * cu[GRID_W:GRID_W + SC_TM, :] + w2 * dn)


def _short_conv(x, mods, l, j, w_in, w_conv, w_out, ln_g, ln_b):
    halo_per_tile = SC_TM // GRID_W
    n_halo = NTOK // GRID_W
    return pl.pallas_call(
        _sconv_kernel,
        grid=(NTOK // SC_TM,),
        in_specs=[
            pl.BlockSpec((SC_TM, D), lambda i: (i, 0)),
            pl.BlockSpec((GRID_W, D), lambda i: (jnp.maximum(i * halo_per_tile - 1, 0), 0)),
            pl.BlockSpec((GRID_W, D), lambda i: (jnp.minimum((i + 1) * halo_per_tile, n_halo - 1), 0)),
            _mod_spec(l),
            _stacked((D, 3 * D), j, True),
            _stacked((3, D), j),
            _stacked((D, D), j, True),
            _stacked((1, D), 2 * l), _stacked((1, D), 2 * l),
        ],
        out_specs=pl.BlockSpec((SC_TM, D), lambda i: (i, 0)),
        out_shape=jax.ShapeDtypeStruct((NTOK, D), F32),
        scratch_shapes=[pltpu.VMEM((D, 3 * D), BF16), pltpu.VMEM((D, D), BF16),
                        pltpu.VMEM((SC_TM + 2 * SUBLANE, D), F32)],
        compiler_params=_cparams("arbitrary"),
        name="short_conv",
    )(x, x, x, mods, w_in, w_conv, w_out, _vec3(ln_g), _vec3(ln_b))


N_CH = TM // CHUNK
GLA_FWD_TILES = 2
GLA_BWD_TILES = 2
STATE = (H, DK, DV)


def _state_out_spec(block, j, d, n):
    return pl.BlockSpec((n, None, None) + STATE,
                        lambda i: (jnp.minimum(block(i), CTX_TILES // n - 1), j, d, 0, 0, 0))


def _emit_states(st_ref, finals, t0):
    @pl.when(t0 < CTX_TILES)
    def _():
        for sub, final in enumerate(finals):
            for hd in range(H):
                st_ref[sub, hd] = final[hd]


def _lat_seq(t):
    return jnp.clip((t - CTX_TILES) // TILES_PER_LAT, 0, DEC_BATCH - 1)


def _chunk_tri(rev):
    row = lax.broadcasted_iota(jnp.int32, (TM, TM), 0)
    col = lax.broadcasted_iota(jnp.int32, (TM, TM), 1)
    same = (row // CHUNK) == (col // CHUNK)
    return same & ((col >= row) if rev else (col <= row))


def _scan_states_in(s_scr, s0_ref, tiles, rev):
    first_r = TILES_PER_LAT - 1 if rev else 0

    def enter(prev, t, hd):
        fresh = jnp.logical_and(t >= CTX_TILES, (t - CTX_TILES) % TILES_PER_LAT == first_r)
        return jnp.where(t < CTX_TILES, 0.0, jnp.where(fresh, s0_ref[hd], prev))

    s_in = [enter(s_scr[hd], tiles[0], hd) for hd in range(H)]
    for p, t in enumerate(tiles[1:], start=1):
        s_in += [functools.partial(lambda done, p, t, hd: enter(done[(p - 1) * H + hd], t, hd),
                                   p=p, t=t, hd=hd) for hd in range(H)]
    return s_in


def _decay_operands(q, k, bcum, rev):
    qd, kd, ke, dec = [], [], [], []
    for c in range(N_CH):
        rows = slice(c * CHUNK, (c + 1) * CHUNK)
        b = bcum[rows, :]
        last = b[0:1, :] if rev else b[CHUNK - 1:CHUNK, :]
        qd.append((q[rows, :] * jnp.exp(b)).astype(BF16))
        kd.append((k[rows, :] * jnp.exp(-b)).astype(BF16))
        ke.append(k[rows, :] * jnp.exp(last - b))
        dec.append(jnp.exp(last))
    return (jnp.concatenate(qd, axis=0), jnp.concatenate(kd, axis=0),
            jnp.concatenate(ke, axis=0).T.astype(BF16), dec)


def _scan_tile(heads, s_in, rev):
    mask = _chunk_tri(rev)
    kcol = lax.broadcasted_iota(jnp.int32, (DK, TM), 1) // CHUNK
    order = range(N_CH - 1, -1, -1) if rev else range(N_CH)
    sc, kv = [], []
    for qh, kh, keth, vh, _ in heads:
        sc.append(lax.dot_general(qh, kh, (((1,), (1,)), ((), ())), preferred_element_type=F32))
        kst = jnp.concatenate([jnp.where(kcol == c, keth, jnp.zeros_like(keth)) for c in range(N_CH)],
                              axis=0)
        kv.append(jnp.dot(kst, vh, preferred_element_type=F32))
    o_intra = [jnp.dot(jnp.where(mask, s, 0.0).astype(BF16), hd[3], preferred_element_type=F32)
               for s, hd in zip(sc, heads)]
    outs, s_out = [], []
    for hd, (qh, _, _, _, dec_rows) in enumerate(heads):
        s, o = (s_in[hd](s_out) if callable(s_in[hd]) else s_in[hd]), [None] * N_CH
        for c in order:
            rows = slice(c * CHUNK, (c + 1) * CHUNK)
            o[c] = o_intra[hd][rows, :] + jnp.dot(qh[rows, :], s.astype(BF16),
                                                  preferred_element_type=F32)
            dec_col = jnp.broadcast_to(dec_rows[c], (DK, DK)).T
            s = s * jnp.concatenate([dec_col, dec_col], axis=1) + kv[hd][c * DK:(c + 1) * DK, :]
        outs.append(o)
        s_out.append(s)
    return outs, s_out


def _gla_fwd_kernel(*refs, n_x):
    x_refs = refs[:n_x]
    (mod_ref, win_ref, wga_ref, wgb_ref, bg_ref, s0_ref, _,
     of_ref, qd_ref, kd_ref, ket_ref, v_ref, r_ref, dec_ref, st_ref,
     wins, tri_scr, s_scr) = refs[n_x:]
    i = pl.program_id(0)

    @pl.when(i == 0)
    def _():
        _cast_rows(win_ref, wins, 128)
        tri_scr[0] = jnp.where(_chunk_tri(False), 1.0, 0.0).astype(BF16)
        tri_scr[1] = jnp.where(_chunk_tri(True), 1.0, 0.0).astype(BF16)
        s_scr[...] = jnp.zeros_like(s_scr)

    subs = range(GLA_FWD_TILES)
    tiles = [i * GLA_FWD_TILES + sub for sub in subs]
    rows = [slice(sub * TM, (sub + 1) * TM) for sub in subs]

    carry, finals = s_scr, []
    for sub in subs:
        heads = []
        shift, scale = _mod_rows(mod_ref, tiles[sub] * TM, (0, 1))
        h = (_read_x(x_refs, tiles[sub], rows[sub]) * (1.0 + scale) + shift).astype(BF16)

        g_parts = []
        for d in range(2):
            za = jnp.dot(h, wga_ref[d].astype(BF16), preferred_element_type=F32)
            z = jnp.dot(za.astype(BF16), wgb_ref[d].astype(BF16), preferred_element_type=F32) + bg_ref[d]
            g = (jnp.minimum(z, 0.0) - jnp.log1p(jnp.exp(-jnp.abs(z)))) * (1.0 / GATE_NORM)
            g_hi = g.astype(BF16)
            g_parts.append((g_hi, (g - g_hi.astype(F32)).astype(BF16)))

        proj = jnp.dot(h, wins[...], preferred_element_type=F32)
        q = proj[:, :KW] * (DK ** -0.5)
        k = proj[:, KW:2 * KW]
        v = proj[:, 2 * KW:2 * KW + D].astype(BF16)
        v_ref[rows[sub], :] = v
        r_ref[rows[sub], :] = proj[:, 2 * KW + D:].astype(BF16)

        bcum = [jnp.dot(tri_scr[d], g_hi, preferred_element_type=F32)
                + jnp.dot(tri_scr[d], g_lo, preferred_element_type=F32)
                for d, (g_hi, g_lo) in enumerate(g_parts)]

        qd_b, kd_b, ket_b, dec_b = _decay_operands(q, k, bcum[1], True)
        qd_ref[rows[sub], :] = qd_b
        kd_ref[rows[sub], :] = kd_b
        ket_ref[:, rows[sub]] = ket_b
        dec_ref[sub] = jnp.concatenate(dec_b + [jnp.zeros((SUBLANE - N_CH, KW), F32)], axis=0)
        qd, kd, ket, dec = _decay_operands(q, k, bcum[0], False)
        for hd in range(H):
            kc, vc = slice(hd * DK, (hd + 1) * DK), slice(hd * DV, (hd + 1) * DV)
            heads.append((qd[:, kc], kd[:, kc], ket[kc, :], v[:, vc], [e[:, kc] for e in dec]))

        outs, carry = _scan_tile(heads, _scan_states_in(carry, s0_ref, [tiles[sub]], False), False)
        finals.append(carry)
        for hd in range(H):
            for c in range(N_CH):
                r0 = sub * TM + c * CHUNK
                of_ref[r0:r0 + CHUNK, hd * DV:(hd + 1) * DV] = outs[hd][c].astype(BF16)
    for hd in range(H):
        s_scr[hd] = carry[hd]
    _emit_states(st_ref, finals, tiles[0])


def _gla_fwd(x, mods, l, j, w_in, w_ga, w_gb, b_g, state_gla, new_states):
    xs = _x_args(x)
    n, rows = GLA_FWD_TILES, GLA_FWD_TILES * TM
    tile = pl.BlockSpec((rows, D), lambda i: (i, 0))
    keys = pl.BlockSpec((rows, KW), lambda i: (i, 0))
    n_in = len(xs) + 7
    return pl.pallas_call(
        functools.partial(_gla_fwd_kernel, n_x=len(xs)),
        grid=(NTILE // n,),
        in_specs=_x_specs(len(xs) == 2, rows=rows) + [
            _mod_spec(l),
            _stacked((D, 2 * KW + 2 * D), j, True),
            _stacked((2, D, RANK), j),
            _stacked((2, RANK, KW), j),
            _stacked((2, 1, KW), j),
            pl.BlockSpec((None, None, None) + STATE, lambda i: (_lat_seq(i * n), j, 0, 0, 0, 0)),
            pl.BlockSpec(memory_space=pl.ANY),
        ],
        out_specs=[
            tile, keys, keys,
            pl.BlockSpec((KW, rows), lambda i: (0, i)),
            tile, tile,
            pl.BlockSpec((n, SUBLANE, KW), lambda i: (i, 0, 0)),
            _state_out_spec(lambda i: i, j, 0, n),
        ],
        out_shape=[
            jax.ShapeDtypeStruct((NTOK, D), BF16),
            jax.ShapeDtypeStruct((NTOK, KW), BF16),
            jax.ShapeDtypeStruct((NTOK, KW), BF16),
            jax.ShapeDtypeStruct((KW, NTOK), BF16),
            jax.ShapeDtypeStruct((NTOK, D), BF16),
            jax.ShapeDtypeStruct((NTOK, D), BF16),
            jax.ShapeDtypeStruct((NTILE, SUBLANE, KW), F32),
            jax.ShapeDtypeStruct(new_states.shape, F32),
        ],
        input_output_aliases={n_in - 1: 7},
        scratch_shapes=[pltpu.VMEM((D, 2 * KW + 2 * D), BF16), pltpu.VMEM((2, TM, TM), BF16),
                        pltpu.VMEM(STATE, F32)],
        compiler_params=_cparams("arbitrary"),
        name="gla_fwd",
    )(*xs, mods, w_in, w_ga, w_gb, b_g.reshape(b_g.shape[0], 2, 1, KW), state_gla, new_states)


def _gla_bwd_kernel(*refs, n_x):
    x_refs = refs[:n_x]
    (qd_ref, kd_ref, ket_ref, v_ref, dec_ref, s0_ref, of_ref, r_ref, mod_ref, gn_ref, wo_ref,
     g_ref, b_ref, _, o_ref, st_ref, wos, s_scr) = refs[n_x:]
    i = pl.program_id(0)
    n = GLA_BWD_TILES
    blk = NTILE // n - 1 - i

    @pl.when(i == 0)
    def _():
        _cast_rows(wo_ref, wos, 128)
        s_scr[...] = jnp.zeros_like(s_scr)

    order = list(range(n - 1, -1, -1))
    tiles = [blk * n + sub for sub in order]
    heads = []
    for sub in order:
        rows = slice(sub * TM, (sub + 1) * TM)
        for hd in range(H):
            kc, vc = slice(hd * DK, (hd + 1) * DK), slice(hd * DV, (hd + 1) * DV)
            heads.append((qd_ref[rows, kc], kd_ref[rows, kc], ket_ref[kc, rows], v_ref[rows, vc],
                          [dec_ref[sub, c:c + 1, kc] for c in range(N_CH)]))
    outs, s_out = _scan_tile(heads, _scan_states_in(s_scr, s0_ref, tiles, True), True)

    finals = [None] * n
    for p, sub in enumerate(order):
        rows = slice(sub * TM, (sub + 1) * TM)
        finals[sub] = s_out[p * H:(p + 1) * H]
        (gate,) = _mod_rows(mod_ref, tiles[p] * TM, (2,))
        parts = []
        for hd in range(H):
            oh = (of_ref[rows, hd * DV:(hd + 1) * DV].astype(F32)
                  + jnp.concatenate(outs[p * H + hd], axis=0))
            ms = jnp.mean(oh * oh, axis=-1, keepdims=True)
            parts.append(oh * lax.rsqrt(ms + RMS_EPS))
        on = jnp.concatenate(parts, axis=1) * gn_ref[...]
        y = jnp.dot((on * _silu(r_ref[rows, :].astype(F32))).astype(BF16), wos[...],
                    preferred_element_type=F32)
        o_ref[rows, :] = _layer_norm(ALPHA * _read_x(x_refs, tiles[p], rows) + gate * y,
                                     g_ref[...], b_ref[...])
    for hd in range(H):
        s_scr[hd] = s_out[(n - 1) * H + hd]
    _emit_states(st_ref, finals, blk * n)


def _gla_bwd(x, o_f, qd, kd, ket, v, r, dec, state_gla, new_states, mods, l, j, gn_g, w_o, ln_g, ln_b):
    xs = _x_args(x)
    n, rows = GLA_BWD_TILES, GLA_BWD_TILES * TM
    rblk = lambda i: NTILE // n - 1 - i
    tile = pl.BlockSpec((rows, D), lambda i: (rblk(i), 0))
    keys = pl.BlockSpec((rows, KW), lambda i: (rblk(i), 0))
    n_in = len(xs) + 14
    return pl.pallas_call(
        functools.partial(_gla_bwd_kernel, n_x=len(xs)),
        grid=(NTILE // n,),
        in_specs=_x_specs(len(xs) == 2, rblk, rows) + [
            keys, keys,
            pl.BlockSpec((KW, rows), lambda i: (0, rblk(i))),
            tile,
            pl.BlockSpec((n, SUBLANE, KW), lambda i: (rblk(i), 0, 0)),
            pl.BlockSpec((None, None, None) + STATE,
                         lambda i: (_lat_seq(rblk(i) * n), j, 1, 0, 0, 0)),
            tile, tile, _mod_spec(l),
            _stacked((1, D), j), _stacked((D, D), j, True),
            _stacked((1, D), 2 * l), _stacked((1, D), 2 * l),
            pl.BlockSpec(memory_space=pl.ANY)],
        out_specs=[tile, _state_out_spec(rblk, j, 1, n)],
        out_shape=[jax.ShapeDtypeStruct((NTOK, D), F32), jax.ShapeDtypeStruct(new_states.shape, F32)],
        input_output_aliases={n_in - 1: 1},
        scratch_shapes=[pltpu.VMEM((D, D), BF16), pltpu.VMEM(STATE, F32)],
        compiler_params=_cparams("arbitrary"),
        name="gla_bwd",
    )(*xs, qd, kd, ket, v, dec, state_gla, o_f, r, mods, _vec3(gn_g), w_o, _vec3(ln_g), _vec3(ln_b),
      new_states)


def kernel(x_prompt, x_sample, c, state_gla, c_ctx, mod_w, mod_b, ln_g, ln_b, ff_w1, ff_w2, gla_w_in, gla_w_ga, gla_w_gb, gla_b_g, gla_gn_g, gla_w_o, conf_w_pw1, conf_b_pw1, conf_w_dw, conf_b_dw, conf_ln_g, conf_ln_b, conf_w_pw2, conf_b_pw2, sc_w_in, sc_w_conv, sc_w_out):
    assert x_prompt.shape == (BATCH, SEQ, D) and x_sample.shape == (DEC_BATCH, DEC_SEQ, D)
    x = (x_prompt.reshape(NCTX, D), x_sample.reshape(NLAT, D))

    c8 = jnp.concatenate([c_ctx[None, :], c, jnp.zeros((SUBLANE - N_CVEC, D), F32)], axis=0)
    mods = _adaln(c8, mod_w, mod_b)

    states = jnp.zeros((BATCH, state_gla.shape[1], 2) + STATE, F32)
    for l in range(DEPTH):
        kind, j = l % 3, l // 3
        if kind == 0:
            o_f, qd, kd, ket, v, r, dec, states = _gla_fwd(x, mods, l, j, gla_w_in, gla_w_ga, gla_w_gb,
                                                           gla_b_g, state_gla, states)
            x, states = _gla_bwd(x, o_f, qd, kd, ket, v, r, dec, state_gla, states, mods, l, j,
                                 gla_gn_g, gla_w_o, ln_g, ln_b)
        elif kind == 2:
            x = _short_conv(x, mods, l, j, sc_w_in, sc_w_conv, sc_w_out, ln_g, ln_b)
        conformer = None
        if l + 1 < DEPTH and (l + 1) % 3 == 1:
            conformer = ((l + 1) // 3, conf_w_pw1, conf_b_pw1, conf_w_dw, conf_b_dw, conf_ln_g, conf_ln_b,
                         conf_w_pw2, conf_b_pw2)
        x = _mlp(x, mods, l, ff_w1, ff_w2, ln_g, ln_b, split_out=(l == DEPTH - 1), conformer=conformer)

    y_prompt, y_sample = x
    return (y_prompt.reshape(BATCH, SEQ, D), y_sample.reshape(DEC_BATCH, DEC_SEQ, D), states)
```

```python
import dataclasses
import functools

import jax
import jax.numpy as jnp
from jax import lax
from jax.experimental import pallas as pl
from jax.experimental.pallas import tpu as pltpu

F32 = jnp.float32
BF16 = jnp.bfloat16

D = 1024
DEPTH = 4
BATCH, SEQ = 16, 256
DEC_BATCH, DEC_SEQ = 2, 2048
GRID_W = 64
N_MOD = 6
N_CVEC = 1 + DEC_BATCH
NCTX = BATCH * SEQ
NLAT = DEC_BATCH * DEC_SEQ
NTOK = NCTX + NLAT
H, DK, DV = 4, 128, 256
KW = H * DK
RANK = 16
CHUNK = 64
GATE_NORM = 16.0
CONF_W = 31
D_FF = 4 * D
LN_EPS = 1e-5
RMS_EPS = 1e-6
ALPHA = (2 * DEPTH) ** 0.25

TM = 256
NTILE = NTOK // TM
CTX_TILES = NCTX // TM
TILES_PER_LAT = DEC_SEQ // TM
TM_MLP = 512
TF_MLP = 1024
TC_MLP = 512
TN_MOD = 2048
SUBLANE = 8
VMEM_LIMIT = 58 * 1024 * 1024


def _cparams(*sem):
    return pltpu.CompilerParams(dimension_semantics=sem, vmem_limit_bytes=VMEM_LIMIT)


def _mod_rows(mod_ref, row0, ks):
    m = jnp.where(row0 < NCTX, 0, 1 + (row0 - NCTX) // DEC_SEQ)
    return [mod_ref[pl.ds(m, 1), k * D:(k + 1) * D] for k in ks]


def _mod_spec(l):
    return pl.BlockSpec((None, SUBLANE, N_MOD * D), lambda *_: (l, 0, 0))


def _stacked(block, j, single_buffer=False):
    nd = len(block)
    mode = dict(pipeline_mode=pl.Buffered(1)) if single_buffer else {}
    return pl.BlockSpec((None,) + block, lambda *_: (j,) + (0,) * nd, **mode)


def _x_specs(split, block=lambda i: i, rows=TM):
    if not split:
        return [pl.BlockSpec((rows, D), lambda i: (block(i), 0))]
    nc, nl = NCTX // rows, NLAT // rows
    return [pl.BlockSpec((rows, D), lambda i: (jnp.clip(block(i), 0, nc - 1), 0)),
            pl.BlockSpec((rows, D), lambda i: (jnp.clip(block(i) - nc, 0, nl - 1), 0))]


def _x_args(x):
    return list(x) if isinstance(x, tuple) else [x]


def _read_x(x_refs, t, rows=slice(None)):
    if len(x_refs) == 1:
        return x_refs[0][rows, :]
    return jnp.where(t < CTX_TILES, x_refs[0][rows, :], x_refs[1][rows, :])


def _layer_norm(y, g, b):
    mu = jnp.mean(y, axis=-1, keepdims=True)
    yc = y - mu
    var = jnp.mean(yc * yc, axis=-1, keepdims=True)
    return yc * lax.rsqrt(var + LN_EPS) * g + b


def _silu(x):
    return x * jax.nn.sigmoid(x)


def _cast_rows(src_ref, dst_ref, step):
    n = src_ref.shape[0] // step

    def body(r, c):
        rows = pl.ds(pl.multiple_of(r * step, step), step)
        dst_ref[rows, :] = src_ref[rows, :].astype(BF16)
        return c

    lax.fori_loop(0, n, body, 0)


def _vec3(a):
    return a.reshape(-1, 1, a.shape[-1])


def _adaln_kernel(c_ref, w_ref, b_ref, o_ref):
    s = _silu(c_ref[...]).astype(BF16)
    o_ref[...] = jnp.dot(s, w_ref[...].astype(BF16), preferred_element_type=F32) + b_ref[...]


def _adaln(c8, mod_w, mod_b):
    return pl.pallas_call(
        _adaln_kernel,
        grid=(DEPTH, N_MOD * D // TN_MOD),
        in_specs=[
            pl.BlockSpec((SUBLANE, D), lambda l, n: (0, 0)),
            pl.BlockSpec((None, D, TN_MOD), lambda l, n: (l, 0, n)),
            pl.BlockSpec((None, 1, TN_MOD), lambda l, n: (l, 0, n)),
        ],
        out_specs=pl.BlockSpec((None, SUBLANE, TN_MOD), lambda l, n: (l, 0, n)),
        out_shape=jax.ShapeDtypeStruct((DEPTH, SUBLANE, N_MOD * D), F32),
        compiler_params=_cparams("parallel", "parallel"),
        name="adaln",
    )(c8, mod_w, _vec3(mod_b))


@dataclasses.dataclass(frozen=True)
class _MlpCfg:
    tm: int
    tf: int
    split_out: bool
    conformer: bool

    @property
    def nj(self):
        return D_FF // self.tf

    @property
    def nt(self):
        return NTOK // self.tm

    def tile(self, s):
        return jnp.clip(s - (self.nj - 1), 0, self.nt - 1)

    def done(self, s):
        return jnp.clip(s - self.nj, 0, self.nt - 1)


N_CONF_IN, N_CONF_SCRATCH = 11, 4


def _sqrelu(a):
    return jnp.square(jnp.maximum(a, 0.0)).astype(BF16)


def _mlp_kernel(*refs, cfg):
    it = iter(refs)
    take = lambda n: [next(it) for _ in range(n)]
    x_ref, xd_ref, mod_ref, w1_ref, w2_ref, g_ref, b_ref = take(7)
    conf_in = take(N_CONF_IN) if cfg.conformer else None
    o_refs = take(2 if cfg.split_out else 1)
    w1s, w2s, h_scr, acc_scr = take(4)
    y_scr = take(1)[0] if cfg.split_out else None
    conf_scr = take(N_CONF_SCRATCH) if cfg.conformer else None
    nj, nt, tm = cfg.nj, cfg.nt, cfg.tm

    s = pl.program_id(0)
    shift, scale = _mod_rows(mod_ref, cfg.tile(s) * tm, (3, 4))
    done = cfg.done(s)

    def finish(emit_matmuls=None):
        (gate,) = _mod_rows(mod_ref, done * tm, (5,))
        y = _layer_norm(ALPHA * xd_ref[...] + gate * acc_scr[...], g_ref[...], b_ref[...])
        if cfg.conformer:
            y = _conformer_tile(y, done, conf_in, conf_scr, emit_matmuls)
        if cfg.split_out:
            y_scr[...] = y
        else:
            o_refs[0][...] = y

    def route():
        if cfg.split_out:
            @pl.when(done < NCTX // tm)
            def _():
                o_refs[0][...] = y_scr[...]

            @pl.when(done >= NCTX // tm)
            def _():
                o_refs[1][...] = y_scr[...]

    @pl.when(s < nj)
    def _():
        w1s[s] = w1_ref[...].astype(BF16)
        w2s[s] = w2_ref[...].astype(BF16)
        if cfg.conformer:
            for src, dst in ((conf_in[1], conf_scr[0]), (conf_in[7], conf_scr[1])):
                rows = D // nj
                dst[pl.ds(pl.multiple_of(s * rows, rows), rows), :] = src[...].astype(BF16)

        @pl.when(s == 0)
        def _():
            h_scr[...] = (x_ref[...] * (1.0 + scale) + shift).astype(BF16)
            acc_scr[...] = jnp.zeros_like(acc_scr)

        a = _sqrelu(jnp.dot(h_scr[...], w1s[s], preferred_element_type=F32))
        acc_scr[...] += jnp.dot(a, w2s[s], preferred_element_type=F32)

    @pl.when(jnp.logical_and(s >= nj, s < nj + nt - 1))
    def _():
        h_scr[...] = (x_ref[...] * (1.0 + scale) + shift).astype(BF16)
        chunks = [(j, slice(c * TC_MLP, (c + 1) * TC_MLP))
                  for j in range(nj) for c in range(cfg.tf // TC_MLP)]
        queue = [("up", 0)]
        for k in range(len(chunks)):
            queue += ([("up", k + 1)] if k + 1 < len(chunks) else []) + [("down", k)]
        act, st = {}, dict(pos=0, acc=None)

        def emit_matmuls(n):
            for kind, k in queue[st["pos"]:st["pos"] + n]:
                j, cols = chunks[k]
                if kind == "up":
                    act[k] = _sqrelu(jnp.dot(h_scr[...], w1s[j, :, cols], preferred_element_type=F32))
                else:
                    part = jnp.dot(act.pop(k), w2s[j, cols, :], preferred_element_type=F32)
                    st["acc"] = part if st["acc"] is None else st["acc"] + part
            st["pos"] = min(st["pos"] + n, len(queue))

        emit_matmuls(2)
        finish(lambda: emit_matmuls(1))
        emit_matmuls(len(queue))
        acc_scr[...] = st["acc"]
        route()

    @pl.when(s == nj + nt - 1)
    def _():
        finish()
        route()


def _mlp(x, mods, l, w1, w2, ln_g, ln_b, split_out, conformer=None):
    cfg = _MlpCfg(tm=TM if conformer else TM_MLP, tf=TC_MLP if conformer else TF_MLP,
                  split_out=split_out, conformer=conformer is not None)
    tm, tf, nj = cfg.tm, cfg.tf, cfg.nj
    nc = NCTX // tm
    chunk = lambda s: jnp.minimum(s, nj - 1)
    in_specs = [
        pl.BlockSpec((tm, D), lambda s: (cfg.tile(s), 0)),
        pl.BlockSpec((tm, D), lambda s: (cfg.done(s), 0)),
        _mod_spec(l),
        pl.BlockSpec((None, D, tf), lambda s: (l, 0, chunk(s))),
        pl.BlockSpec((None, tf, D), lambda s: (l, chunk(s), 0)),
        _stacked((1, D), 2 * l + 1), _stacked((1, D), 2 * l + 1),
    ]
    args = [x, x, mods, w1, w2, _vec3(ln_g), _vec3(ln_b)]
    scratch = [pltpu.VMEM((nj, D, tf), BF16), pltpu.VMEM((nj, tf, D), BF16),
               pltpu.VMEM((tm, D), BF16), pltpu.VMEM((tm, D), F32)]
    if split_out:
        out_specs = [pl.BlockSpec((tm, D), lambda s: (jnp.minimum(cfg.done(s), nc - 1), 0)),
                     pl.BlockSpec((tm, D), lambda s: (jnp.maximum(cfg.done(s) - nc, 0), 0))]
        out_shape = [jax.ShapeDtypeStruct((NCTX, D), F32), jax.ShapeDtypeStruct((NLAT, D), F32)]
        scratch.append(pltpu.VMEM((tm, D), F32))
    else:
        out_specs = pl.BlockSpec((tm, D), lambda s: (cfg.done(s), 0))
        out_shape = jax.ShapeDtypeStruct((NTOK, D), F32)
    if conformer:
        j, cw1, cb1, wdw, bdw, cg, cb, cw2, cb2 = conformer
        slab = lambda cols: pl.BlockSpec((None, D // nj, cols), lambda s: (j, chunk(s), 0))
        in_specs += [_mod_spec(l + 1), slab(2 * D), _stacked((1, 2 * D), j), _stacked((CONF_W, D), j),
                     _stacked((1, D), j), _stacked((1, D), j), _stacked((1, D), j),
                     slab(D), _stacked((1, D), j),
                     _stacked((1, D), 2 * (l + 1)), _stacked((1, D), 2 * (l + 1))]
        args += [mods, cw1, _vec3(cb1), wdw, _vec3(bdw), _vec3(cg), _vec3(cb), cw2, _vec3(cb2),
                 _vec3(ln_g), _vec3(ln_b)]
        scratch += [pltpu.VMEM((D, 2 * D), BF16), pltpu.VMEM((D, D), BF16),
                    pltpu.VMEM((CONF_ROWS, CONF_LB), F32),
                    pltpu.VMEM((SUBLANE - 1, CONF_ROWS, CONF_LB), F32)]
        assert len(in_specs) == 7 + N_CONF_IN
    return pl.pallas_call(
        functools.partial(_mlp_kernel, cfg=cfg),
        grid=(nj + cfg.nt,),
        in_specs=in_specs,
        out_specs=out_specs,
        out_shape=out_shape,
        scratch_shapes=scratch,
        compiler_params=_cparams("arbitrary"),
        name="mlp_conformer" if conformer else "mlp",
    )(*args)


CONF_PAD = 16
CONF_LB = 256
CONF_ROWS = (TM // GRID_W) * (GRID_W + 2 * CONF_PAD)


def _conf_conv(upad_ref, shf_ref, w_taps, u, joined, between=None):
    nseg = TM // GRID_W
    stride = GRID_W + 2 * CONF_PAD
    zpad = jnp.zeros((CONF_PAD, CONF_LB), F32)
    for s in range(nseg):
        base, r0 = s * stride, s * GRID_W
        above = jnp.where(joined, u[r0 - CONF_PAD:r0, :], 0.0) if s > 0 else zpad
        below = jnp.where(joined, u[r0 + GRID_W:r0 + GRID_W + CONF_PAD, :], 0.0) if s < nseg - 1 else zpad
        upad_ref[base:base + CONF_PAD, :] = above
        upad_ref[base + CONF_PAD:base + CONF_PAD + GRID_W, :] = u[r0:r0 + GRID_W, :]
        upad_ref[base + CONF_PAD + GRID_W:base + stride, :] = below
    for b in range(1, SUBLANE):
        shf_ref[b - 1, 0:CONF_ROWS - SUBLANE, :] = upad_ref[b:b + CONF_ROWS - SUBLANE, :]
    out = []
    for s in range(nseg):
        if between is not None:
            between()
        r0 = s * stride + CONF_PAD
        acc = jnp.zeros((GRID_W, CONF_LB), F32)
        for k in range(CONF_W):
            a, b = divmod(k - CONF_W // 2, SUBLANE)
            rows = slice(r0 + SUBLANE * a, r0 + SUBLANE * a + GRID_W)
            acc = acc + w_taps[k] * (upad_ref[rows, :] if b == 0 else shf_ref[b - 1, rows, :])
        out.append(acc)
    return jnp.concatenate(out, axis=0)


def _conformer_tile(x, t, conf_in, conf_scr, between=None):
    mod_ref, _, b1_ref, wdw_ref, bdw_ref, cg_ref, cb_ref, _, b2_ref, g_ref, b_ref = conf_in
    w1s, w2s, upad, shf = conf_scr
    shift, scale, gate = _mod_rows(mod_ref, t * TM, (0, 1, 2))
    h = (x * (1.0 + scale) + shift).astype(BF16)

    def glu(lb):
        cols = slice(lb * CONF_LB, (lb + 1) * CONF_LB)
        gcols = slice(D + lb * CONF_LB, D + (lb + 1) * CONF_LB)
        a = jnp.dot(h, w1s[:, cols], preferred_element_type=F32) + b1_ref[:, cols]
        g = jnp.dot(h, w1s[:, gcols], preferred_element_type=F32) + b1_ref[:, gcols]
        return a * jax.nn.sigmoid(g)

    n_lb = D // CONF_LB
    conv = []
    u = glu(0)
    for lb in range(n_lb):
        u_next = glu(lb + 1) if lb + 1 < n_lb else None
        cols = slice(lb * CONF_LB, (lb + 1) * CONF_LB)
        taps = [wdw_ref[k:k + 1, cols] for k in range(CONF_W)]
        conv.append(_conf_conv(upad, shf, taps, u, t < CTX_TILES, between))
        u = u_next

    hook = between if between is not None else (lambda: None)
    hook()
    uc = jnp.concatenate(conv, axis=1) + bdw_ref[...]
    uc = _silu(_layer_norm(uc, cg_ref[...], cb_ref[...]))
    y = jnp.dot(uc.astype(BF16), w2s[...], preferred_element_type=F32) + b2_ref[...]
    hook()
    return _layer_norm(ALPHA * x + gate * y, g_ref[...], b_ref[...])


SC_TM = 2 * TM

def _sconv_kernel(x_ref, xp_ref, xn_ref, mod_ref, win_ref, wc_ref, wout_ref, g_ref, b_ref, o_ref,
                  wins, wouts, cpad):
    i = pl.program_id(0)
    shift, scale, gate = _mod_rows(mod_ref, i * SC_TM, (0, 1, 2))

    @pl.when(i == 0)
    def _():
        _cast_rows(win_ref, wins, 128)
        _cast_rows(wout_ref, wouts, 128)

    scale = 1.0 + scale
    w0, w1, w2 = wc_ref[0:1, :], wc_ref[1:2, :], wc_ref[2:3, :]

    def finish(x, h, y):
        bg = jnp.dot(h, wins[:, :D], preferred_element_type=F32)
        out = jnp.dot((bg * y).astype(BF16), wouts[...], preferred_element_type=F32)
        o_ref[...] = _layer_norm(ALPHA * x + gate * out, g_ref[...], b_ref[...])

    @pl.when(i < NCTX // SC_TM)
    def _():
        x = x_ref[...]
        h = (x * scale + shift).astype(BF16)
        cu = jnp.dot(h, wins[:, D:], preferred_element_type=F32)
        cu = cu[:, :D] * cu[:, D:]
        zrow = jnp.zeros((SUBLANE, D), F32)
        cpad[0:SUBLANE, :] = zrow
        cpad[SUBLANE:SUBLANE + SC_TM, :] = cu
        cpad[SUBLANE + SC_TM:2 * SUBLANE + SC_TM, :] = zrow
        pos = lax.broadcasted_iota(jnp.int32, (SC_TM, D), 0) % SEQ
        before = jnp.where(pos == 0, 0.0, cpad[SUBLANE - 1:SUBLANE - 1 + SC_TM, :])
        after = jnp.where(pos == SEQ - 1, 0.0, cpad[SUBLANE + 1:SUBLANE + 1 + SC_TM, :])
        finish(x, h, w0 * before + w1 * cu + w2 * after)

    @pl.when(i >= NCTX // SC_TM)
    def _():
        blocks_per_seq = DEC_SEQ // SC_TM
        r = (i - NCTX // SC_TM) % blocks_per_seq
        x = x_ref[...]
        h = (x * scale + shift).astype(BF16)
        hp = (xp_ref[...] * scale + shift).astype(BF16)
        hn = (xn_ref[...] * scale + shift).astype(BF16)
        hcat = jnp.concatenate([hp, h, hn], axis=0)
        cu = jnp.dot(hcat, wins[:, D:], preferred_element_type=F32)
        cu = cu[:, :D] * cu[:, D:]
        halo_up = jnp.where(r > 0, cu[0:GRID_W, :], 0.0)
        halo_dn = jnp.where(r < blocks_per_seq - 1, cu[GRID_W + SC_TM:, :], 0.0)
        up = jnp.concatenate([halo_up, cu[GRID_W:SC_TM, :]], axis=0)
        dn = jnp.concatenate([cu[2 * GRID_W:GRID_W + SC_TM, :], halo_dn], axis=0)
        finish(x, h, w0 * up + w1 * cu[GRID_W:GRID_W + SC_TM, :] + w2 * dn)


def _short_conv(x, mods, l, j, w_in, w_conv, w_out, ln_g, ln_b):
    halo_per_tile = SC_TM // GRID_W
    n_halo = NTOK // GRID_W
    return pl.pallas_call(
        _sconv_kernel,
        grid=(NTOK // SC_TM,),
        in_specs=[
            pl.BlockSpec((SC_TM, D), lambda i: (i, 0)),
            pl.BlockSpec((GRID_W, D), lambda i: (jnp.maximum(i * halo_per_tile - 1, 0), 0)),
            pl.BlockSpec((GRID_W, D), lambda i: (jnp.minimum((i + 1) * halo_per_tile, n_halo - 1), 0)),
            _mod_spec(l),
            _stacked((D, 3 * D), j, True),
            _stacked((3, D), j),
            _stacked((D, D), j, True),
            _stacked((1, D), 2 * l), _stacked((1, D), 2 * l),
        ],
        out_specs=pl.BlockSpec((SC_TM, D), lambda i: (i, 0)),
        out_shape=jax.ShapeDtypeStruct((NTOK, D), F32),
        scratch_shapes=[pltpu.VMEM((D, 3 * D), BF16), pltpu.VMEM((D, D), BF16),
                        pltpu.VMEM((SC_TM + 2 * SUBLANE, D), F32)],
        compiler_params=_cparams("arbitrary"),
        name="short_conv",
    )(x, x, x, mods, w_in, w_conv, w_out, _vec3(ln_g), _vec3(ln_b))


N_CH = TM // CHUNK
GLA_FWD_TILES = 2
GLA_BWD_TILES = 2
STATE = (H, DK, DV)


def _state_out_spec(block, j, d, n):
    return pl.BlockSpec((n, None, None) + STATE,
                        lambda i: (jnp.minimum(block(i), CTX_TILES // n - 1), j, d, 0, 0, 0))


def _emit_states(st_ref, finals, t0, create=None):
    @pl.when(t0 < CTX_TILES)
    def _():
        if create is not None:
            st_ref[...] = jnp.zeros(st_ref.shape, F32)
        for sub, final in enumerate(finals):
            for hd in range(H):
                if create is None:
                    st_ref[sub, hd] = final[hd]
                else:
                    st_ref[sub, create, 0, hd] = final[hd]


def _lat_seq(t):
    return jnp.clip((t - CTX_TILES) // TILES_PER_LAT, 0, DEC_BATCH - 1)


def _chunk_tri(rev):
    row = lax.broadcasted_iota(jnp.int32, (TM, TM), 0)
    col = lax.broadcasted_iota(jnp.int32, (TM, TM), 1)
    same = (row // CHUNK) == (col // CHUNK)
    return same & ((col >= row) if rev else (col <= row))


def _scan_states_in(s_scr, s0_ref, tiles, rev):
    first_r = TILES_PER_LAT - 1 if rev else 0

    def enter(prev, t, hd):
        fresh = jnp.logical_and(t >= CTX_TILES, (t - CTX_TILES) % TILES_PER_LAT == first_r)
        return jnp.where(t < CTX_TILES, 0.0, jnp.where(fresh, s0_ref[hd], prev))

    s_in = [enter(s_scr[hd], tiles[0], hd) for hd in range(H)]
    for p, t in enumerate(tiles[1:], start=1):
        s_in += [functools.partial(lambda done, p, t, hd: enter(done[(p - 1) * H + hd], t, hd),
                                   p=p, t=t, hd=hd) for hd in range(H)]
    return s_in


def _decay_operands(q, k, bcum, rev):
    qd, kd, ke, dec = [], [], [], []
    for c in range(N_CH):
        rows = slice(c * CHUNK, (c + 1) * CHUNK)
        b = bcum[rows, :]
        last = b[0:1, :] if rev else b[CHUNK - 1:CHUNK, :]
        qd.append((q[rows, :] * jnp.exp(b)).astype(BF16))
        kd.append((k[rows, :] * jnp.exp(-b)).astype(BF16))
        ke.append(k[rows, :] * jnp.exp(last - b))
        dec.append(jnp.exp(last))
    return (jnp.concatenate(qd, axis=0), jnp.concatenate(kd, axis=0),
            jnp.concatenate(ke, axis=0).T.astype(BF16), dec)


def _scan_tile(heads, s_in, rev):
    mask = _chunk_tri(rev)
    kcol = lax.broadcasted_iota(jnp.int32, (DK, TM), 1) // CHUNK
    order = range(N_CH - 1, -1, -1) if rev else range(N_CH)
    sc, kv = [], []
    for qh, kh, keth, vh, _ in heads:
        sc.append(lax.dot_general(qh, kh, (((1,), (1,)), ((), ())), preferred_element_type=F32))
        kst = jnp.concatenate([jnp.where(kcol == c, keth, jnp.zeros_like(keth)) for c in range(N_CH)],
                              axis=0)
        kv.append(jnp.dot(kst, vh, preferred_element_type=F32))
    o_intra = [jnp.dot(jnp.where(mask, s, 0.0).astype(BF16), hd[3], preferred_element_type=F32)
               for s, hd in zip(sc, heads)]
    outs, s_out = [], []
    for hd, (qh, _, _, _, dec_rows) in enumerate(heads):
        s, o = (s_in[hd](s_out) if callable(s_in[hd]) else s_in[hd]), [None] * N_CH
        for c in order:
            rows = slice(c * CHUNK, (c + 1) * CHUNK)
            o[c] = o_intra[hd][rows, :] + jnp.dot(qh[rows, :], s.astype(BF16),
                                                  preferred_element_type=F32)
            dec_col = jnp.broadcast_to(dec_rows[c], (DK, DK)).T
            s = s * jnp.concatenate([dec_col, dec_col], axis=1) + kv[hd][c * DK:(c + 1) * DK, :]
        outs.append(o)
        s_out.append(s)
    return outs, s_out


def _gla_fwd_kernel(*refs, n_x, create):
    x_refs = refs[:n_x]
    mod_ref, win_ref, wga_ref, wgb_ref, bg_ref, s0_ref = refs[n_x:n_x + 6]
    (of_ref, qd_ref, kd_ref, ket_ref, v_ref, r_ref, dec_ref, st_ref,
     wins, tri_scr, s_scr) = refs[n_x + (6 if create is not None else 7):]
    i = pl.program_id(0)

    @pl.when(i == 0)
    def _():
        _cast_rows(win_ref, wins, 128)
        tri_scr[0] = jnp.where(_chunk_tri(False), 1.0, 0.0).astype(BF16)
        tri_scr[1] = jnp.where(_chunk_tri(True), 1.0, 0.0).astype(BF16)
        s_scr[...] = jnp.zeros_like(s_scr)

    subs = range(GLA_FWD_TILES)
    tiles = [i * GLA_FWD_TILES + sub for sub in subs]
    rows = [slice(sub * TM, (sub + 1) * TM) for sub in subs]

    carry, finals = s_scr, []
    for sub in subs:
        heads = []
        shift, scale = _mod_rows(mod_ref, tiles[sub] * TM, (0, 1))
        h = (_read_x(x_refs, tiles[sub], rows[sub]) * (1.0 + scale) + shift).astype(BF16)

        g_parts = []
        for d in range(2):
            za = jnp.dot(h, wga_ref[d].astype(BF16), preferred_element_type=F32)
            z = jnp.dot(za.astype(BF16), wgb_ref[d].astype(BF16), preferred_element_type=F32) + bg_ref[d]
            g = (jnp.minimum(z, 0.0) - jnp.log1p(jnp.exp(-jnp.abs(z)))) * (1.0 / GATE_NORM)
            g_hi = g.astype(BF16)
            g_parts.append((g_hi, (g - g_hi.astype(F32)).astype(BF16)))

        proj = jnp.dot(h, wins[...], preferred_element_type=F32)
        q = proj[:, :KW] * (DK ** -0.5)
        k = proj[:, KW:2 * KW]
        v = proj[:, 2 * KW:2 * KW + D].astype(BF16)
        v_ref[rows[sub], :] = v
        r_ref[rows[sub], :] = proj[:, 2 * KW + D:].astype(BF16)

        bcum = [jnp.dot(tri_scr[d], g_hi, preferred_element_type=F32)
                + jnp.dot(tri_scr[d], g_lo, preferred_element_type=F32)
                for d, (g_hi, g_lo) in enumerate(g_parts)]

        qd_b, kd_b, ket_b, dec_b = _decay_operands(q, k, bcum[1], True)
        qd_ref[rows[sub], :] = qd_b
        kd_ref[rows[sub], :] = kd_b
        ket_ref[:, rows[sub]] = ket_b
        dec_ref[sub] = jnp.concatenate(dec_b + [jnp.zeros((SUBLANE - N_CH, KW), F32)], axis=0)
        qd, kd, ket, dec = _decay_operands(q, k, bcum[0], False)
        for hd in range(H):
            kc, vc = slice(hd * DK, (hd + 1) * DK), slice(hd * DV, (hd + 1) * DV)
            heads.append((qd[:, kc], kd[:, kc], ket[kc, :], v[:, vc], [e[:, kc] for e in dec]))

        outs, carry = _scan_tile(heads, _scan_states_in(carry, s0_ref, [tiles[sub]], False), False)
        finals.append(carry)
        for hd in range(H):
            for c in range(N_CH):
                r0 = sub * TM + c * CHUNK
                of_ref[r0:r0 + CHUNK, hd * DV:(hd + 1) * DV] = outs[hd][c].astype(BF16)
    for hd in range(H):
        s_scr[hd] = carry[hd]
    _emit_states(st_ref, finals, tiles[0], create)


def _gla_fwd(x, mods, l, j, w_in, w_ga, w_gb, b_g, state_gla, new_states):
    xs = _x_args(x)
    n, rows = GLA_FWD_TILES, GLA_FWD_TILES * TM
    tile = pl.BlockSpec((rows, D), lambda i: (i, 0))
    keys = pl.BlockSpec((rows, KW), lambda i: (i, 0))
    n_in = len(xs) + 7
    create = new_states is None
    n_gla = state_gla.shape[1]
    if create:
        st_spec = pl.BlockSpec((n, n_gla, 2) + STATE,
                               lambda i: (jnp.minimum(i, CTX_TILES // n - 1), 0, 0, 0, 0, 0))
        st_in, st_args, alias = [], [], {}
    else:
        st_spec = _state_out_spec(lambda i: i, j, 0, n)
        st_in, st_args, alias = [pl.BlockSpec(memory_space=pl.ANY)], [new_states], {n_in - 1: 7}
    return pl.pallas_call(
        functools.partial(_gla_fwd_kernel, n_x=len(xs), create=j if create else None),
        grid=(NTILE // n,),
        in_specs=_x_specs(len(xs) == 2, rows=rows) + [
            _mod_spec(l),
            _stacked((D, 2 * KW + 2 * D), j, True),
            _stacked((2, D, RANK), j),
            _stacked((2, RANK, KW), j),
            _stacked((2, 1, KW), j),
            pl.BlockSpec((None, None, None) + STATE, lambda i: (_lat_seq(i * n), j, 0, 0, 0, 0)),
        ] + st_in,
        out_specs=[
            tile, keys, keys,
            pl.BlockSpec((KW, rows), lambda i: (0, i)),
            tile, tile,
            pl.BlockSpec((n, SUBLANE, KW), lambda i: (i, 0, 0)),
            st_spec,
        ],
        out_shape=[
            jax.ShapeDtypeStruct((NTOK, D), BF16),
            jax.ShapeDtypeStruct((NTOK, KW), BF16),
            jax.ShapeDtypeStruct((NTOK, KW), BF16),
            jax.ShapeDtypeStruct((KW, NTOK), BF16),
            jax.ShapeDtypeStruct((NTOK, D), BF16),
            jax.ShapeDtypeStruct((NTOK, D), BF16),
            jax.ShapeDtypeStruct((NTILE, SUBLANE, KW), F32),
            jax.ShapeDtypeStruct((BATCH, n_gla, 2) + STATE, F32),
        ],
        input_output_aliases=alias,
        scratch_shapes=[pltpu.VMEM((D, 2 * KW + 2 * D), BF16), pltpu.VMEM((2, TM, TM), BF16),
                        pltpu.VMEM(STATE, F32)],
        compiler_params=_cparams("arbitrary"),
        name="gla_fwd",
    )(*xs, mods, w_in, w_ga, w_gb, b_g.reshape(b_g.shape[0], 2, 1, KW), state_gla, *st_args)


def _gla_bwd_kernel(*refs, n_x):
    x_refs = refs[:n_x]
    (qd_ref, kd_ref, ket_ref, v_ref, dec_ref, s0_ref, of_ref, r_ref, mod_ref, gn_ref, wo_ref,
     g_ref, b_ref, _, o_ref, st_ref, wos, s_scr) = refs[n_x:]
    i = pl.program_id(0)
    n = GLA_BWD_TILES
    blk = NTILE // n - 1 - i

    @pl.when(i == 0)
    def _():
        _cast_rows(wo_ref, wos, 128)
        s_scr[...] = jnp.zeros_like(s_scr)

    order = list(range(n - 1, -1, -1))
    tiles = [blk * n + sub for sub in order]
    heads = []
    for sub in order:
        rows = slice(sub * TM, (sub + 1) * TM)
        for hd in range(H):
            kc, vc = slice(hd * DK, (hd + 1) * DK), slice(hd * DV, (hd + 1) * DV)
            heads.append((qd_ref[rows, kc], kd_ref[rows, kc], ket_ref[kc, rows], v_ref[rows, vc],
                          [dec_ref[sub, c:c + 1, kc] for c in range(N_CH)]))
    outs, s_out = _scan_tile(heads, _scan_states_in(s_scr, s0_ref, tiles, True), True)

    finals = [None] * n
    for p, sub in enumerate(order):
        rows = slice(sub * TM, (sub + 1) * TM)
        finals[sub] = s_out[p * H:(p + 1) * H]
        (gate,) = _mod_rows(mod_ref, tiles[p] * TM, (2,))
        parts = []
        for hd in range(H):
            oh = (of_ref[rows, hd * DV:(hd + 1) * DV].astype(F32)
                  + jnp.concatenate(outs[p * H + hd], axis=0))
            ms = jnp.mean(oh * oh, axis=-1, keepdims=True)
            parts.append(oh * lax.rsqrt(ms + RMS_EPS))
        on = jnp.concatenate(parts, axis=1) * gn_ref[...]
        y = jnp.dot((on * _silu(r_ref[rows, :].astype(F32))).astype(BF16), wos[...],
                    preferred_element_type=F32)
        o_ref[rows, :] = _layer_norm(ALPHA * _read_x(x_refs, tiles[p], rows) + gate * y,
                                     g_ref[...], b_ref[...])
    for hd in range(H):
        s_scr[hd] = s_out[(n - 1) * H + hd]
    _emit_states(st_ref, finals, blk * n)


def _gla_bwd(x, o_f, qd, kd, ket, v, r, dec, state_gla, new_states, mods, l, j, gn_g, w_o, ln_g, ln_b):
    xs = _x_args(x)
    n, rows = GLA_BWD_TILES, GLA_BWD_TILES * TM
    rblk = lambda i: NTILE // n - 1 - i
    tile = pl.BlockSpec((rows, D), lambda i: (rblk(i), 0))
    keys = pl.BlockSpec((rows, KW), lambda i: (rblk(i), 0))
    n_in = len(xs) + 14
    return pl.pallas_call(
        functools.partial(_gla_bwd_kernel, n_x=len(xs)),
        grid=(NTILE // n,),
        in_specs=_x_specs(len(xs) == 2, rblk, rows) + [
            keys, keys,
            pl.BlockSpec((KW, rows), lambda i: (0, rblk(i))),
            tile,
            pl.BlockSpec((n, SUBLANE, KW), lambda i: (rblk(i), 0, 0)),
            pl.BlockSpec((None, None, None) + STATE,
                         lambda i: (_lat_seq(rblk(i) * n), j, 1, 0, 0, 0)),
            tile, tile, _mod_spec(l),
            _stacked((1, D), j), _stacked((D, D), j, True),
            _stacked((1, D), 2 * l), _stacked((1, D), 2 * l),
            pl.BlockSpec(memory_space=pl.ANY)],
        out_specs=[tile, _state_out_spec(rblk, j, 1, n)],
        out_shape=[jax.ShapeDtypeStruct((NTOK, D), F32), jax.ShapeDtypeStruct(new_states.shape, F32)],
        input_output_aliases={n_in - 1: 1},
        scratch_shapes=[pltpu.VMEM((D, D), BF16), pltpu.VMEM(STATE, F32)],
        compiler_params=_cparams("arbitrary"),
        name="gla_bwd",
    )(*xs, qd, kd, ket, v, dec, state_gla, o_f, r, mods, _vec3(gn_g), w_o, _vec3(ln_g), _vec3(ln_b),
      new_states)


def kernel(x_prompt, x_sample, c, state_gla, c_ctx, mod_w, mod_b, ln_g, ln_b, ff_w1, ff_w2, gla_w_in, gla_w_ga, gla_w_gb, gla_b_g, gla_gn_g, gla_w_o, conf_w_pw1, conf_b_pw1, conf_w_dw, conf_b_dw, conf_ln_g, conf_ln_b, conf_w_pw2, conf_b_pw2, sc_w_in, sc_w_conv, sc_w_out):
    assert x_prompt.shape == (BATCH, SEQ, D) and x_sample.shape == (DEC_BATCH, DEC_SEQ, D)
    x = (x_prompt.reshape(NCTX, D), x_sample.reshape(NLAT, D))

    c8 = jnp.concatenate([c_ctx[None, :], c, jnp.zeros((SUBLANE - N_CVEC, D), F32)], axis=0)
    mods = _adaln(c8, mod_w, mod_b)

    states = None
    for l in range(DEPTH):
        kind, j = l % 3, l // 3
        if kind == 0:
            o_f, qd, kd, ket, v, r, dec, states = _gla_fwd(x, mods, l, j, gla_w_in, gla_w_ga, gla_w_gb,
                                                           gla_b_g, state_gla, states)
            x, states = _gla_bwd(x, o_f, qd, kd, ket, v, r, dec, state_gla, states, mods, l, j,
                                 gla_gn_g, gla_w_o, ln_g, ln_b)
        elif kind == 2:
            x = _short_conv(x, mods, l, j, sc_w_in, sc_w_conv, sc_w_out, ln_g, ln_b)
        conformer = None
        if l + 1 < DEPTH and (l + 1) % 3 == 1:
            conformer = ((l + 1) // 3, conf_w_pw1, conf_b_pw1, conf_w_dw, conf_b_dw, conf_ln_g, conf_ln_b,
                         conf_w_pw2, conf_b_pw2)
        x = _mlp(x, mods, l, ff_w1, ff_w2, ln_g, ln_b, split_out=(l == DEPTH - 1), conformer=conformer)

    y_prompt, y_sample = x
    return (y_prompt.reshape(BATCH, SEQ, D), y_sample.reshape(DEC_BATCH, DEC_SEQ, D), states)
```

```python
import dataclasses
import functools

import jax
import jax.numpy as jnp
from jax import lax
from jax.experimental import pallas as pl
from jax.experimental.pallas import tpu as pltpu

F32 = jnp.float32
BF16 = jnp.bfloat16

D = 1024
DEPTH = 4
BATCH, SEQ = 16, 256
DEC_BATCH, DEC_SEQ = 2, 2048
GRID_W = 64
N_MOD = 6
N_CVEC = 1 + DEC_BATCH
NCTX = BATCH * SEQ
NLAT = DEC_BATCH * DEC_SEQ
NTOK = NCTX + NLAT
H, DK, DV = 4, 128, 256
KW = H * DK
RANK = 16
CHUNK = 64
GATE_NORM = 16.0
CONF_W = 31
D_FF = 4 * D
LN_EPS = 1e-5
RMS_EPS = 1e-6
ALPHA = (2 * DEPTH) ** 0.25

TM = 256
NTILE = NTOK // TM
CTX_TILES = NCTX // TM
TILES_PER_LAT = DEC_SEQ // TM
TM_MLP = 512
TF_MLP = 1024
TC_MLP = 512
TN_MOD = 2048
SUBLANE = 8
VMEM_LIMIT = 58 * 1024 * 1024


def _cparams(*sem):
    return pltpu.CompilerParams(dimension_semantics=sem, vmem_limit_bytes=VMEM_LIMIT)


def _mod_rows(mod_ref, row0, ks):
    m = jnp.where(row0 < NCTX, 0, 1 + (row0 - NCTX) // DEC_SEQ)
    return [mod_ref[pl.ds(m, 1), k * D:(k + 1) * D] for k in ks]


def _mod_spec(l):
    return pl.BlockSpec((None, SUBLANE, N_MOD * D), lambda *_: (l, 0, 0))


def _stacked(block, j, single_buffer=False):
    nd = len(block)
    mode = dict(pipeline_mode=pl.Buffered(1)) if single_buffer else {}
    return pl.BlockSpec((None,) + block, lambda *_: (j,) + (0,) * nd, **mode)


def _x_specs(split, block=lambda i: i, rows=TM):
    if not split:
        return [pl.BlockSpec((rows, D), lambda i: (block(i), 0))]
    nc, nl = NCTX // rows, NLAT // rows
    return [pl.BlockSpec((rows, D), lambda i: (jnp.clip(block(i), 0, nc - 1), 0)),
            pl.BlockSpec((rows, D), lambda i: (jnp.clip(block(i) - nc, 0, nl - 1), 0))]


def _x_args(x):
    return list(x) if isinstance(x, tuple) else [x]


def _read_x(x_refs, t, rows=slice(None)):
    if len(x_refs) == 1:
        return x_refs[0][rows, :]
    return jnp.where(t < CTX_TILES, x_refs[0][rows, :], x_refs[1][rows, :])


def _layer_norm(y, g, b):
    mu = jnp.mean(y, axis=-1, keepdims=True)
    yc = y - mu
    var = jnp.mean(yc * yc, axis=-1, keepdims=True)
    return yc * lax.rsqrt(var + LN_EPS) * g + b


def _silu(x):
    return x * jax.nn.sigmoid(x)


def _cast_rows(src_ref, dst_ref, step):
    n = src_ref.shape[0] // step

    def body(r, c):
        rows = pl.ds(pl.multiple_of(r * step, step), step)
        dst_ref[rows, :] = src_ref[rows, :].astype(BF16)
        return c

    lax.fori_loop(0, n, body, 0)


def _vec3(a):
    return a.reshape(-1, 1, a.shape[-1])


def _ln_spec(l):
    return _stacked((2, D), l)


def _adaln_kernel(cctx_ref, c_ref, w_ref, b_ref, o_ref, c8_scr):
    c8_scr[0:1, :] = cctx_ref[...]
    c8_scr[1:N_CVEC, :] = c_ref[...]
    c8_scr[N_CVEC:, :] = jnp.zeros((SUBLANE - N_CVEC, D), F32)
    s = _silu(c8_scr[...]).astype(BF16)
    bias = b_ref[pl.ds(pl.program_id(0), 1), :]
    o_ref[...] = jnp.dot(s, w_ref[...].astype(BF16), preferred_element_type=F32) + bias


def _adaln(c_ctx, c, mod_w, mod_b):
    return pl.pallas_call(
        _adaln_kernel,
        grid=(DEPTH, N_MOD * D // TN_MOD),
        in_specs=[
            pl.BlockSpec((1, D), lambda l, n: (0, 0)),
            pl.BlockSpec((DEC_BATCH, D), lambda l, n: (0, 0)),
            pl.BlockSpec((None, D, TN_MOD), lambda l, n: (l, 0, n)),
            pl.BlockSpec((DEPTH, TN_MOD), lambda l, n: (0, n)),
        ],
        out_specs=pl.BlockSpec((None, SUBLANE, TN_MOD), lambda l, n: (l, 0, n)),
        out_shape=jax.ShapeDtypeStruct((DEPTH, SUBLANE, N_MOD * D), F32),
        scratch_shapes=[pltpu.VMEM((SUBLANE, D), F32)],
        compiler_params=_cparams("parallel", "parallel"),
        name="adaln",
    )(c_ctx.reshape(1, D), c, mod_w, mod_b)


@dataclasses.dataclass(frozen=True)
class _MlpCfg:
    tm: int
    tf: int
    split_out: bool
    conformer: bool

    @property
    def nj(self):
        return D_FF // self.tf

    @property
    def nt(self):
        return NTOK // self.tm

    def tile(self, s):
        return jnp.clip(s - (self.nj - 1), 0, self.nt - 1)

    def done(self, s):
        return jnp.clip(s - self.nj, 0, self.nt - 1)


N_CONF_IN, N_CONF_SCRATCH = 11, 4


def _sqrelu(a):
    return jnp.square(jnp.maximum(a, 0.0)).astype(BF16)


def _mlp_kernel(*refs, cfg):
    it = iter(refs)
    take = lambda n: [next(it) for _ in range(n)]
    x_ref, xd_ref, mod_ref, w1_ref, w2_ref, g_ref, b_ref = take(7)
    conf_in = take(N_CONF_IN) if cfg.conformer else None
    o_refs = take(2 if cfg.split_out else 1)
    w1s, w2s, h_scr, acc_scr = take(4)
    y_scr = take(1)[0] if cfg.split_out else None
    conf_scr = take(N_CONF_SCRATCH) if cfg.conformer else None
    nj, nt, tm = cfg.nj, cfg.nt, cfg.tm

    s = pl.program_id(0)
    shift, scale = _mod_rows(mod_ref, cfg.tile(s) * tm, (3, 4))
    done = cfg.done(s)

    def finish(emit_matmuls=None):
        (gate,) = _mod_rows(mod_ref, done * tm, (5,))
        y = _layer_norm(ALPHA * xd_ref[...] + gate * acc_scr[...], g_ref[1:2, :], b_ref[1:2, :])
        if cfg.conformer:
            y = _conformer_tile(y, done, conf_in, conf_scr, emit_matmuls)
        if cfg.split_out:
            y_scr[...] = y
        else:
            o_refs[0][...] = y

    def route():
        if cfg.split_out:
            @pl.when(done < NCTX // tm)
            def _():
                o_refs[0][...] = y_scr[...]

            @pl.when(done >= NCTX // tm)
            def _():
                o_refs[1][...] = y_scr[...]

    @pl.when(s < nj)
    def _():
        w1s[s] = w1_ref[...].astype(BF16)
        w2s[s] = w2_ref[...].astype(BF16)
        if cfg.conformer:
            for src, dst in ((conf_in[1], conf_scr[0]), (conf_in[7], conf_scr[1])):
                rows = D // nj
                dst[pl.ds(pl.multiple_of(s * rows, rows), rows), :] = src[...].astype(BF16)

        @pl.when(s == 0)
        def _():
            h_scr[...] = (x_ref[...] * (1.0 + scale) + shift).astype(BF16)
            acc_scr[...] = jnp.zeros_like(acc_scr)

        a = _sqrelu(jnp.dot(h_scr[...], w1s[s], preferred_element_type=F32))
        acc_scr[...] += jnp.dot(a, w2s[s], preferred_element_type=F32)

    @pl.when(jnp.logical_and(s >= nj, s < nj + nt - 1))
    def _():
        h_scr[...] = (x_ref[...] * (1.0 + scale) + shift).astype(BF16)
        chunks = [(j, slice(c * TC_MLP, (c + 1) * TC_MLP))
                  for j in range(nj) for c in range(cfg.tf // TC_MLP)]
        queue = [("up", 0)]
        for k in range(len(chunks)):
            queue += ([("up", k + 1)] if k + 1 < len(chunks) else []) + [("down", k)]
        act, st = {}, dict(pos=0, acc=None)

        def emit_matmuls(n):
            for kind, k in queue[st["pos"]:st["pos"] + n]:
                j, cols = chunks[k]
                if kind == "up":
                    act[k] = _sqrelu(jnp.dot(h_scr[...], w1s[j, :, cols], preferred_element_type=F32))
                else:
                    part = jnp.dot(act.pop(k), w2s[j, cols, :], preferred_element_type=F32)
                    st["acc"] = part if st["acc"] is None else st["acc"] + part
            st["pos"] = min(st["pos"] + n, len(queue))

        emit_matmuls(2)
        finish(lambda: emit_matmuls(1))
        emit_matmuls(len(queue))
        acc_scr[...] = st["acc"]
        route()

    @pl.when(s == nj + nt - 1)
    def _():
        finish()
        route()


def _mlp(x, mods, l, w1, w2, ln_g, ln_b, split_out, conformer=None):
    cfg = _MlpCfg(tm=TM if conformer else TM_MLP, tf=TC_MLP if conformer else TF_MLP,
                  split_out=split_out, conformer=conformer is not None)
    tm, tf, nj = cfg.tm, cfg.tf, cfg.nj
    nc = NCTX // tm
    chunk = lambda s: jnp.minimum(s, nj - 1)
    in_specs = [
        pl.BlockSpec((tm, D), lambda s: (cfg.tile(s), 0)),
        pl.BlockSpec((tm, D), lambda s: (cfg.done(s), 0)),
        _mod_spec(l),
        pl.BlockSpec((None, D, tf), lambda s: (l, 0, chunk(s))),
        pl.BlockSpec((None, tf, D), lambda s: (l, chunk(s), 0)),
        _ln_spec(l), _ln_spec(l),
    ]
    args = [x, x, mods, w1, w2, ln_g, ln_b]
    scratch = [pltpu.VMEM((nj, D, tf), BF16), pltpu.VMEM((nj, tf, D), BF16),
               pltpu.VMEM((tm, D), BF16), pltpu.VMEM((tm, D), F32)]
    if split_out:
        out_specs = [pl.BlockSpec((tm, D), lambda s: (jnp.minimum(cfg.done(s), nc - 1), 0)),
                     pl.BlockSpec((tm, D), lambda s: (jnp.maximum(cfg.done(s) - nc, 0), 0))]
        out_shape = [jax.ShapeDtypeStruct((NCTX, D), F32), jax.ShapeDtypeStruct((NLAT, D), F32)]
        scratch.append(pltpu.VMEM((tm, D), F32))
    else:
        out_specs = pl.BlockSpec((tm, D), lambda s: (cfg.done(s), 0))
        out_shape = jax.ShapeDtypeStruct((NTOK, D), F32)
    if conformer:
        j, cw1, cb1, wdw, bdw, cg, cb, cw2, cb2 = conformer
        slab = lambda cols: pl.BlockSpec((None, D // nj, cols), lambda s: (j, chunk(s), 0))
        in_specs += [_mod_spec(l + 1), slab(2 * D), _stacked((1, 2 * D), j),
                     pl.BlockSpec((CONF_W, 1, D), lambda s: (j, 0, 0)),
                     _stacked((1, D), j), _stacked((1, D), j), _stacked((1, D), j),
                     slab(D), _stacked((1, D), j),
                     _ln_spec(l + 1), _ln_spec(l + 1)]
        args += [mods, cw1, _vec3(cb1), _vec3(wdw), _vec3(bdw), _vec3(cg), _vec3(cb), cw2, _vec3(cb2),
                 ln_g, ln_b]
        scratch += [pltpu.VMEM((D, 2 * D), BF16), pltpu.VMEM((D, D), BF16),
                    pltpu.VMEM((CONF_ROWS, CONF_LB), F32),
                    pltpu.VMEM((SUBLANE - 1, CONF_ROWS, CONF_LB), F32)]
        assert len(in_specs) == 7 + N_CONF_IN
    return pl.pallas_call(
        functools.partial(_mlp_kernel, cfg=cfg),
        grid=(nj + cfg.nt,),
        in_specs=in_specs,
        out_specs=out_specs,
        out_shape=out_shape,
        scratch_shapes=scratch,
        compiler_params=_cparams("arbitrary"),
        name="mlp_conformer" if conformer else "mlp",
    )(*args)


CONF_PAD = 16
CONF_LB = 256
CONF_ROWS = (TM // GRID_W) * (GRID_W + 2 * CONF_PAD)


def _conf_conv(upad_ref, shf_ref, w_taps, u, joined, between=None):
    nseg = TM // GRID_W
    stride = GRID_W + 2 * CONF_PAD
    zpad = jnp.zeros((CONF_PAD, CONF_LB), F32)
    for s in range(nseg):
        base, r0 = s * stride, s * GRID_W
        above = jnp.where(joined, u[r0 - CONF_PAD:r0, :], 0.0) if s > 0 else zpad
        below = jnp.where(joined, u[r0 + GRID_W:r0 + GRID_W + CONF_PAD, :], 0.0) if s < nseg - 1 else zpad
        upad_ref[base:base + CONF_PAD, :] = above
        upad_ref[base + CONF_PAD:base + CONF_PAD + GRID_W, :] = u[r0:r0 + GRID_W, :]
        upad_ref[base + CONF_PAD + GRID_W:base + stride, :] = below
    for b in range(1, SUBLANE):
        shf_ref[b - 1, 0:CONF_ROWS - SUBLANE, :] = upad_ref[b:b + CONF_ROWS - SUBLANE, :]
    out = []
    for s in range(nseg):
        if between is not None:
            between()
        r0 = s * stride + CONF_PAD
        acc = jnp.zeros((GRID_W, CONF_LB), F32)
        for k in range(CONF_W):
            a, b = divmod(k - CONF_W // 2, SUBLANE)
            rows = slice(r0 + SUBLANE * a, r0 + SUBLANE * a + GRID_W)
            acc = acc + w_taps[k] * (upad_ref[rows, :] if b == 0 else shf_ref[b - 1, rows, :])
        out.append(acc)
    return jnp.concatenate(out, axis=0)


def _conformer_tile(x, t, conf_in, conf_scr, between=None):
    mod_ref, _, b1_ref, wdw_ref, bdw_ref, cg_ref, cb_ref, _, b2_ref, g_ref, b_ref = conf_in
    w1s, w2s, upad, shf = conf_scr
    shift, scale, gate = _mod_rows(mod_ref, t * TM, (0, 1, 2))
    h = (x * (1.0 + scale) + shift).astype(BF16)

    def glu(lb):
        cols = slice(lb * CONF_LB, (lb + 1) * CONF_LB)
        gcols = slice(D + lb * CONF_LB, D + (lb + 1) * CONF_LB)
        a = jnp.dot(h, w1s[:, cols], preferred_element_type=F32) + b1_ref[:, cols]
        g = jnp.dot(h, w1s[:, gcols], preferred_element_type=F32) + b1_ref[:, gcols]
        return a * jax.nn.sigmoid(g)

    n_lb = D // CONF_LB
    conv = []
    u = glu(0)
    for lb in range(n_lb):
        u_next = glu(lb + 1) if lb + 1 < n_lb else None
        cols = slice(lb * CONF_LB, (lb + 1) * CONF_LB)
        taps = [wdw_ref[k, :, cols] for k in range(CONF_W)]
        conv.append(_conf_conv(upad, shf, taps, u, t < CTX_TILES, between))
        u = u_next

    hook = between if between is not None else (lambda: None)
    hook()
    uc = jnp.concatenate(conv, axis=1) + bdw_ref[...]
    uc = _silu(_layer_norm(uc, cg_ref[...], cb_ref[...]))
    y = jnp.dot(uc.astype(BF16), w2s[...], preferred_element_type=F32) + b2_ref[...]
    hook()
    return _layer_norm(ALPHA * x + gate * y, g_ref[0:1, :], b_ref[0:1, :])


SC_TM = 2 * TM

def _sconv_kernel(x_ref, xp_ref, xn_ref, mod_ref, win_ref, wc_ref, wout_ref, g_ref, b_ref, o_ref,
                  wins, wouts, cpad):
    i = pl.program_id(0)
    shift, scale, gate = _mod_rows(mod_ref, i * SC_TM, (0, 1, 2))

    @pl.when(i == 0)
    def _():
        _cast_rows(win_ref, wins, 128)
        _cast_rows(wout_ref, wouts, 128)

    scale = 1.0 + scale
    w0, w1, w2 = wc_ref[0], wc_ref[1], wc_ref[2]

    def finish(x, h, y):
        bg = jnp.dot(h, wins[:, :D], preferred_element_type=F32)
        out = jnp.dot((bg * y).astype(BF16), wouts[...], preferred_element_type=F32)
        o_ref[...] = _layer_norm(ALPHA * x + gate * out, g_ref[0:1, :], b_ref[0:1, :])

    @pl.when(i < NCTX // SC_TM)
    def _():
        x = x_ref[...]
        h = (x * scale + shift).astype(BF16)
        cu = jnp.dot(h, wins[:, D:], preferred_element_type=F32)
        cu = cu[:, :D] * cu[:, D:]
        zrow = jnp.zeros((SUBLANE, D), F32)
        cpad[0:SUBLANE, :] = zrow
        cpad[SUBLANE:SUBLANE + SC_TM, :] = cu
        cpad[SUBLANE + SC_TM:2 * SUBLANE + SC_TM, :] = zrow
        pos = lax.broadcasted_iota(jnp.int32, (SC_TM, D), 0) % SEQ
        before = jnp.where(pos == 0, 0.0, cpad[SUBLANE - 1:SUBLANE - 1 + SC_TM, :])
        after = jnp.where(pos == SEQ - 1, 0.0, cpad[SUBLANE + 1:SUBLANE + 1 + SC_TM, :])
        finish(x, h, w0 * before + w1 * cu + w2 * after)

    @pl.when(i >= NCTX // SC_TM)
    def _():
        blocks_per_seq = DEC_SEQ // SC_TM
        r = (i - NCTX // SC_TM) % blocks_per_seq
        x = x_ref[...]
        h = (x * scale + shift).astype(BF16)
        hp = (xp_ref[...] * scale + shift).astype(BF16)
        hn = (xn_ref[...] * scale + shift).astype(BF16)
        hcat = jnp.concatenate([hp, h, hn], axis=0)
        cu = jnp.dot(hcat, wins[:, D:], preferred_element_type=F32)
        cu = cu[:, :D] * cu[:, D:]
        halo_up = jnp.where(r > 0, cu[0:GRID_W, :], 0.0)
        halo_dn = jnp.where(r < blocks_per_seq - 1, cu[GRID_W + SC_TM:, :], 0.0)
        up = jnp.concatenate([halo_up, cu[GRID_W:SC_TM, :]], axis=0)
        dn = jnp.concatenate([cu[2 * GRID_W:GRID_W + SC_TM, :], halo_dn], axis=0)
        finish(x, h, w0 * up + w1 * cu[GRID_W:GRID_W + SC_TM, :] + w2 * dn)


def _short_conv(x, mods, l, j, w_in, w_conv, w_out, ln_g, ln_b):
    halo_per_tile = SC_TM // GRID_W
    n_halo = NTOK // GRID_W
    return pl.pallas_call(
        _sconv_kernel,
        grid=(NTOK // SC_TM,),
        in_specs=[
            pl.BlockSpec((SC_TM, D), lambda i: (i, 0)),
            pl.BlockSpec((GRID_W, D), lambda i: (jnp.maximum(i * halo_per_tile - 1, 0), 0)),
            pl.BlockSpec((GRID_W, D), lambda i: (jnp.minimum((i + 1) * halo_per_tile, n_halo - 1), 0)),
            _mod_spec(l),
            _stacked((D, 3 * D), j, True),
            pl.BlockSpec((3, 1, D), lambda i: (j, 0, 0)),
            _stacked((D, D), j, True),
            _ln_spec(l), _ln_spec(l),
        ],
        out_specs=pl.BlockSpec((SC_TM, D), lambda i: (i, 0)),
        out_shape=jax.ShapeDtypeStruct((NTOK, D), F32),
        scratch_shapes=[pltpu.VMEM((D, 3 * D), BF16), pltpu.VMEM((D, D), BF16),
                        pltpu.VMEM((SC_TM + 2 * SUBLANE, D), F32)],
        compiler_params=_cparams("arbitrary"),
        name="short_conv",
    )(x, x, x, mods, w_in, _vec3(w_conv), w_out, ln_g, ln_b)


N_CH = TM // CHUNK
GLA_FWD_TILES = 2
GLA_BWD_TILES = 2
STATE = (H, DK, DV)


def _state_out_spec(block, j, d, n):
    return pl.BlockSpec((n, None, None) + STATE,
                        lambda i: (jnp.minimum(block(i), CTX_TILES // n - 1), j, d, 0, 0, 0))


def _emit_states(st_ref, finals, t0, create=None):
    @pl.when(t0 < CTX_TILES)
    def _():
        if create is not None:
            st_ref[...] = jnp.zeros(st_ref.shape, F32)
        for sub, final in enumerate(finals):
            for hd in range(H):
                if create is None:
                    st_ref[sub, hd] = final[hd]
                else:
                    st_ref[sub, create, 0, hd] = final[hd]


def _lat_seq(t):
    return jnp.clip((t - CTX_TILES) // TILES_PER_LAT, 0, DEC_BATCH - 1)


def _chunk_tri(rev):
    row = lax.broadcasted_iota(jnp.int32, (TM, TM), 0)
    col = lax.broadcasted_iota(jnp.int32, (TM, TM), 1)
    same = (row // CHUNK) == (col // CHUNK)
    return same & ((col >= row) if rev else (col <= row))


def _scan_states_in(s_scr, s0_ref, tiles, rev):
    first_r = TILES_PER_LAT - 1 if rev else 0

    def enter(prev, t, hd):
        fresh = jnp.logical_and(t >= CTX_TILES, (t - CTX_TILES) % TILES_PER_LAT == first_r)
        return jnp.where(t < CTX_TILES, 0.0, jnp.where(fresh, s0_ref[hd], prev))

    s_in = [enter(s_scr[hd], tiles[0], hd) for hd in range(H)]
    for p, t in enumerate(tiles[1:], start=1):
        s_in += [functools.partial(lambda done, p, t, hd: enter(done[(p - 1) * H + hd], t, hd),
                                   p=p, t=t, hd=hd) for hd in range(H)]
    return s_in


def _decay_operands(q, k, bcum, rev):
    qd, kd, ke, dec = [], [], [], []
    for c in range(N_CH):
        rows = slice(c * CHUNK, (c + 1) * CHUNK)
        b = bcum[rows, :]
        last = b[0:1, :] if rev else b[CHUNK - 1:CHUNK, :]
        qd.append((q[rows, :] * jnp.exp(b)).astype(BF16))
        kd.append((k[rows, :] * jnp.exp(-b)).astype(BF16))
        ke.append(k[rows, :] * jnp.exp(last - b))
        dec.append(jnp.exp(last))
    return (jnp.concatenate(qd, axis=0), jnp.concatenate(kd, axis=0),
            jnp.concatenate(ke, axis=0).T.astype(BF16), dec)


def _scan_tile(heads, s_in, rev):
    mask = _chunk_tri(rev)
    kcol = lax.broadcasted_iota(jnp.int32, (DK, TM), 1) // CHUNK
    order = range(N_CH - 1, -1, -1) if rev else range(N_CH)
    sc, kv = [], []
    for qh, kh, keth, vh, _ in heads:
        sc.append(lax.dot_general(qh, kh, (((1,), (1,)), ((), ())), preferred_element_type=F32))
        kst = jnp.concatenate([jnp.where(kcol == c, keth, jnp.zeros_like(keth)) for c in range(N_CH)],
                              axis=0)
        kv.append(jnp.dot(kst, vh, preferred_element_type=F32))
    o_intra = [jnp.dot(jnp.where(mask, s, 0.0).astype(BF16), hd[3], preferred_element_type=F32)
               for s, hd in zip(sc, heads)]
    outs, s_out = [], []
    for hd, (qh, _, _, _, dec_rows) in enumerate(heads):
        s, o = (s_in[hd](s_out) if callable(s_in[hd]) else s_in[hd]), [None] * N_CH
        for c in order:
            rows = slice(c * CHUNK, (c + 1) * CHUNK)
            o[c] = o_intra[hd][rows, :] + jnp.dot(qh[rows, :], s.astype(BF16),
                                                  preferred_element_type=F32)
            dec_col = jnp.broadcast_to(dec_rows[c], (DK, DK)).T
            s = s * jnp.concatenate([dec_col, dec_col], axis=1) + kv[hd][c * DK:(c + 1) * DK, :]
        outs.append(o)
        s_out.append(s)
    return outs, s_out


def _gla_fwd_kernel(*refs, n_x, create):
    x_refs = refs[:n_x]
    mod_ref, win_ref, wga_ref, wgb_ref, bg_ref, s0_ref = refs[n_x:n_x + 6]
    (of_ref, qd_ref, kd_ref, ket_ref, v_ref, r_ref, dec_ref, st_ref,
     wins, tri_scr, s_scr) = refs[n_x + (6 if create is not None else 7):]
    i = pl.program_id(0)

    @pl.when(i == 0)
    def _():
        _cast_rows(win_ref, wins, 128)
        tri_scr[0] = jnp.where(_chunk_tri(False), 1.0, 0.0).astype(BF16)
        tri_scr[1] = jnp.where(_chunk_tri(True), 1.0, 0.0).astype(BF16)
        s_scr[...] = jnp.zeros_like(s_scr)

    subs = range(GLA_FWD_TILES)
    tiles = [i * GLA_FWD_TILES + sub for sub in subs]
    rows = [slice(sub * TM, (sub + 1) * TM) for sub in subs]

    carry, finals = s_scr, []
    for sub in subs:
        heads = []
        shift, scale = _mod_rows(mod_ref, tiles[sub] * TM, (0, 1))
        h = (_read_x(x_refs, tiles[sub], rows[sub]) * (1.0 + scale) + shift).astype(BF16)

        g_parts = []
        for d in range(2):
            za = lax.dot_general(h, wga_ref[d].astype(BF16), (((1,), (1,)), ((), ())),
                                 preferred_element_type=F32)
            z = (jnp.dot(za.astype(BF16), wgb_ref[d].astype(BF16), preferred_element_type=F32)
                 + bg_ref[d:d + 1, :])
            g = (jnp.minimum(z, 0.0) - jnp.log1p(jnp.exp(-jnp.abs(z)))) * (1.0 / GATE_NORM)
            g_hi = g.astype(BF16)
            g_parts.append((g_hi, (g - g_hi.astype(F32)).astype(BF16)))

        proj = jnp.dot(h, wins[...], preferred_element_type=F32)
        q = proj[:, :KW] * (DK ** -0.5)
        k = proj[:, KW:2 * KW]
        v = proj[:, 2 * KW:2 * KW + D].astype(BF16)
        v_ref[rows[sub], :] = v
        r_ref[rows[sub], :] = proj[:, 2 * KW + D:].astype(BF16)

        bcum = [jnp.dot(tri_scr[d], g_hi, preferred_element_type=F32)
                + jnp.dot(tri_scr[d], g_lo, preferred_element_type=F32)
                for d, (g_hi, g_lo) in enumerate(g_parts)]

        qd_b, kd_b, ket_b, dec_b = _decay_operands(q, k, bcum[1], True)
        qd_ref[rows[sub], :] = qd_b
        kd_ref[rows[sub], :] = kd_b
        ket_ref[:, rows[sub]] = ket_b
        dec_ref[sub] = jnp.concatenate(dec_b + [jnp.zeros((SUBLANE - N_CH, KW), F32)], axis=0)
        qd, kd, ket, dec = _decay_operands(q, k, bcum[0], False)
        for hd in range(H):
            kc, vc = slice(hd * DK, (hd + 1) * DK), slice(hd * DV, (hd + 1) * DV)
            heads.append((qd[:, kc], kd[:, kc], ket[kc, :], v[:, vc], [e[:, kc] for e in dec]))

        outs, carry = _scan_tile(heads, _scan_states_in(carry, s0_ref, [tiles[sub]], False), False)
        finals.append(carry)
        for hd in range(H):
            for c in range(N_CH):
                r0 = sub * TM + c * CHUNK
                of_ref[r0:r0 + CHUNK, hd * DV:(hd + 1) * DV] = outs[hd][c].astype(BF16)
    for hd in range(H):
        s_scr[hd] = carry[hd]
    _emit_states(st_ref, finals, tiles[0], create)


def _gla_fwd(x, mods, l, j, w_in, w_ga, w_gb, b_g, state_gla, new_states):
    xs = _x_args(x)
    n, rows = GLA_FWD_TILES, GLA_FWD_TILES * TM
    tile = pl.BlockSpec((rows, D), lambda i: (i, 0))
    keys = pl.BlockSpec((rows, KW), lambda i: (i, 0))
    n_in = len(xs) + 7
    create = new_states is None
    n_gla = state_gla.shape[1]
    if create:
        st_spec = pl.BlockSpec((n, n_gla, 2) + STATE,
                               lambda i: (jnp.minimum(i, CTX_TILES // n - 1), 0, 0, 0, 0, 0))
        st_in, st_args, alias = [], [], {}
    else:
        st_spec = _state_out_spec(lambda i: i, j, 0, n)
        st_in, st_args, alias = [pl.BlockSpec(memory_space=pl.ANY)], [new_states], {n_in - 1: 7}
    return pl.pallas_call(
        functools.partial(_gla_fwd_kernel, n_x=len(xs), create=j if create else None),
        grid=(NTILE // n,),
        in_specs=_x_specs(len(xs) == 2, rows=rows) + [
            _mod_spec(l),
            _stacked((D, 2 * KW + 2 * D), j, True),
            _stacked((2, RANK, D), j),
            _stacked((2, RANK, KW), j),
            _stacked((2, KW), j),
            pl.BlockSpec((None, None, None) + STATE, lambda i: (_lat_seq(i * n), j, 0, 0, 0, 0)),
        ] + st_in,
        out_specs=[
            tile, keys, keys,
            pl.BlockSpec((KW, rows), lambda i: (0, i)),
            tile, tile,
            pl.BlockSpec((n, SUBLANE, KW), lambda i: (i, 0, 0)),
            st_spec,
        ],
        out_shape=[
            jax.ShapeDtypeStruct((NTOK, D), BF16),
            jax.ShapeDtypeStruct((NTOK, KW), BF16),
            jax.ShapeDtypeStruct((NTOK, KW), BF16),
            jax.ShapeDtypeStruct((KW, NTOK), BF16),
            jax.ShapeDtypeStruct((NTOK, D), BF16),
            jax.ShapeDtypeStruct((NTOK, D), BF16),
            jax.ShapeDtypeStruct((NTILE, SUBLANE, KW), F32),
            jax.ShapeDtypeStruct((BATCH, n_gla, 2) + STATE, F32),
        ],
        input_output_aliases=alias,
        scratch_shapes=[pltpu.VMEM((D, 2 * KW + 2 * D), BF16), pltpu.VMEM((2, TM, TM), BF16),
                        pltpu.VMEM(STATE, F32)],
        compiler_params=_cparams("arbitrary"),
        name="gla_fwd",
    )(*xs, mods, w_in, jnp.swapaxes(w_ga, 2, 3), w_gb, b_g, state_gla, *st_args)


def _gla_bwd_kernel(*refs, n_x, j):
    x_refs = refs[:n_x]
    (qd_ref, kd_ref, ket_ref, v_ref, dec_ref, s0_ref, of_ref, r_ref, mod_ref, gn_ref, wo_ref,
     g_ref, b_ref, _, o_ref, st_ref, wos, s_scr) = refs[n_x:]
    i = pl.program_id(0)
    n = GLA_BWD_TILES
    blk = NTILE // n - 1 - i

    @pl.when(i == 0)
    def _():
        _cast_rows(wo_ref, wos, 128)
        s_scr[...] = jnp.zeros_like(s_scr)

    order = list(range(n - 1, -1, -1))
    tiles = [blk * n + sub for sub in order]
    heads = []
    for sub in order:
        rows = slice(sub * TM, (sub + 1) * TM)
        for hd in range(H):
            kc, vc = slice(hd * DK, (hd + 1) * DK), slice(hd * DV, (hd + 1) * DV)
            heads.append((qd_ref[rows, kc], kd_ref[rows, kc], ket_ref[kc, rows], v_ref[rows, vc],
                          [dec_ref[sub, c:c + 1, kc] for c in range(N_CH)]))
    outs, s_out = _scan_tile(heads, _scan_states_in(s_scr, s0_ref, tiles, True), True)

    finals = [None] * n
    for p, sub in enumerate(order):
        rows = slice(sub * TM, (sub + 1) * TM)
        finals[sub] = s_out[p * H:(p + 1) * H]
        (gate,) = _mod_rows(mod_ref, tiles[p] * TM, (2,))
        parts = []
        for hd in range(H):
            oh = (of_ref[rows, hd * DV:(hd + 1) * DV].astype(F32)
                  + jnp.concatenate(outs[p * H + hd], axis=0))
            ms = jnp.mean(oh * oh, axis=-1, keepdims=True)
            parts.append(oh * lax.rsqrt(ms + RMS_EPS))
        on = jnp.concatenate(parts, axis=1) * gn_ref[j:j + 1, :]
        y = jnp.dot((on * _silu(r_ref[rows, :].astype(F32))).astype(BF16), wos[...],
                    preferred_element_type=F32)
        o_ref[rows, :] = _layer_norm(ALPHA * _read_x(x_refs, tiles[p], rows) + gate * y,
                                     g_ref[0:1, :], b_ref[0:1, :])
    for hd in range(H):
        s_scr[hd] = s_out[(n - 1) * H + hd]
    _emit_states(st_ref, finals, blk * n)


def _gla_bwd(x, o_f, qd, kd, ket, v, r, dec, state_gla, new_states, mods, l, j, gn_g, w_o, ln_g, ln_b):
    xs = _x_args(x)
    n, rows = GLA_BWD_TILES, GLA_BWD_TILES * TM
    rblk = lambda i: NTILE // n - 1 - i
    tile = pl.BlockSpec((rows, D), lambda i: (rblk(i), 0))
    keys = pl.BlockSpec((rows, KW), lambda i: (rblk(i), 0))
    n_in = len(xs) + 14
    return pl.pallas_call(
        functools.partial(_gla_bwd_kernel, n_x=len(xs), j=j),
        grid=(NTILE // n,),
        in_specs=_x_specs(len(xs) == 2, rblk, rows) + [
            keys, keys,
            pl.BlockSpec((KW, rows), lambda i: (0, rblk(i))),
            tile,
            pl.BlockSpec((n, SUBLANE, KW), lambda i: (rblk(i), 0, 0)),
            pl.BlockSpec((None, None, None) + STATE,
                         lambda i: (_lat_seq(rblk(i) * n), j, 1, 0, 0, 0)),
            tile, tile, _mod_spec(l),
            pl.BlockSpec(gn_g.shape, lambda i: (0, 0)), _stacked((D, D), j, True),
            _ln_spec(l), _ln_spec(l),
            pl.BlockSpec(memory_space=pl.ANY)],
        out_specs=[tile, _state_out_spec(rblk, j, 1, n)],
        out_shape=[jax.ShapeDtypeStruct((NTOK, D), F32), jax.ShapeDtypeStruct(new_states.shape, F32)],
        input_output_aliases={n_in - 1: 1},
        scratch_shapes=[pltpu.VMEM((D, D), BF16), pltpu.VMEM(STATE, F32)],
        compiler_params=_cparams("arbitrary"),
        name="gla_bwd",
    )(*xs, qd, kd, ket, v, dec, state_gla, o_f, r, mods, gn_g, w_o, ln_g, ln_b,
      new_states)


def kernel(x_prompt, x_sample, c, state_gla, c_ctx, mod_w, mod_b, ln_g, ln_b, ff_w1, ff_w2, gla_w_in, gla_w_ga, gla_w_gb, gla_b_g, gla_gn_g, gla_w_o, conf_w_pw1, conf_b_pw1, conf_w_dw, conf_b_dw, conf_ln_g, conf_ln_b, conf_w_pw2, conf_b_pw2, sc_w_in, sc_w_conv, sc_w_out):
    assert x_prompt.shape == (BATCH, SEQ, D) and x_sample.shape == (DEC_BATCH, DEC_SEQ, D)
    x = (x_prompt.reshape(NCTX, D), x_sample.reshape(NLAT, D))

    mods = _adaln(c_ctx, c, mod_w, mod_b)

    states = None
    for l in range(DEPTH):
        kind, j = l % 3, l // 3
        if kind == 0:
            o_f, qd, kd, ket, v, r, dec, states = _gla_fwd(x, mods, l, j, gla_w_in, gla_w_ga, gla_w_gb,
                                                           gla_b_g, state_gla, states)
            x, states = _gla_bwd(x, o_f, qd, kd, ket, v, r, dec, state_gla, states, mods, l, j,
                                 gla_gn_g, gla_w_o, ln_g, ln_b)
        elif kind == 2:
            x = _short_conv(x, mods, l, j, sc_w_in, sc_w_conv, sc_w_out, ln_g, ln_b)
        conformer = None
        if l + 1 < DEPTH and (l + 1) % 3 == 1:
            conformer = ((l + 1) // 3, conf_w_pw1, conf_b_pw1, conf_w_dw, conf_b_dw, conf_ln_g, conf_ln_b,
                         conf_w_pw2, conf_b_pw2)
        x = _mlp(x, mods, l, ff_w1, ff_w2, ln_g, ln_b, split_out=(l == DEPTH - 1), conformer=conformer)

    y_prompt, y_sample = x
    return (y_prompt.reshape(BATCH, SEQ, D), y_sample.reshape(DEC_BATCH, DEC_SEQ, D), states)
```

```python
import dataclasses
import functools

import jax
import jax.numpy as jnp
from jax import lax
from jax.experimental import pallas as pl
from jax.experimental.pallas import tpu as pltpu

F32 = jnp.float32
BF16 = jnp.bfloat16

D = 1024
DEPTH = 4
BATCH, SEQ = 16, 256
DEC_BATCH, DEC_SEQ = 2, 2048
GRID_W = 64
N_MOD = 6
N_CVEC = 1 + DEC_BATCH
NCTX = BATCH * SEQ
NLAT = DEC_BATCH * DEC_SEQ
NTOK = NCTX + NLAT
H, DK, DV = 4, 128, 256
KW = H * DK
RANK = 16
CHUNK = 64
GATE_NORM = 16.0
CONF_W = 31
D_FF = 4 * D
LN_EPS = 1e-5
RMS_EPS = 1e-6
ALPHA = (2 * DEPTH) ** 0.25

TM = 256
NTILE = NTOK // TM
CTX_TILES = NCTX // TM
TILES_PER_LAT = DEC_SEQ // TM
TM_MLP = 512
TF_MLP = 1024
TC_MLP = 512
TN_MOD = 2048
SUBLANE = 8
VMEM_LIMIT = 58 * 1024 * 1024


def _cparams(*sem):
    return pltpu.CompilerParams(dimension_semantics=sem, vmem_limit_bytes=VMEM_LIMIT)


def _mod_rows(mod_ref, row0, ks):
    m = jnp.where(row0 < NCTX, 0, 1 + (row0 - NCTX) // DEC_SEQ)
    return [mod_ref[pl.ds(m, 1), k * D:(k + 1) * D] for k in ks]


def _mod_spec(l):
    return pl.BlockSpec((None, SUBLANE, N_MOD * D), lambda *_: (l, 0, 0))


def _stacked(block, j, single_buffer=False):
    nd = len(block)
    mode = dict(pipeline_mode=pl.Buffered(1)) if single_buffer else {}
    return pl.BlockSpec((None,) + block, lambda *_: (j,) + (0,) * nd, **mode)


def _x_specs(split, block=lambda i: i, rows=TM):
    if not split:
        return [pl.BlockSpec((rows, D), lambda i: (block(i), 0))]
    nc, nl = NCTX // rows, NLAT // rows
    return [pl.BlockSpec((rows, D), lambda i: (jnp.clip(block(i), 0, nc - 1), 0)),
            pl.BlockSpec((rows, D), lambda i: (jnp.clip(block(i) - nc, 0, nl - 1), 0))]


def _x_args(x):
    return list(x) if isinstance(x, tuple) else [x]


def _read_x(x_refs, t, rows=slice(None)):
    if len(x_refs) == 1:
        return x_refs[0][rows, :]
    return jnp.where(t < CTX_TILES, x_refs[0][rows, :], x_refs[1][rows, :])


def _layer_norm(y, g, b):
    mu = jnp.mean(y, axis=-1, keepdims=True)
    yc = y - mu
    var = jnp.mean(yc * yc, axis=-1, keepdims=True)
    return yc * lax.rsqrt(var + LN_EPS) * g + b


def _silu(x):
    return x * jax.nn.sigmoid(x)


def _cast_rows(src_ref, dst_ref, step):
    n = src_ref.shape[0] // step

    def body(r, c):
        rows = pl.ds(pl.multiple_of(r * step, step), step)
        dst_ref[rows, :] = src_ref[rows, :].astype(BF16)
        return c

    lax.fori_loop(0, n, body, 0)


def _vec3(a):
    return a.reshape(-1, 1, a.shape[-1])


def _ln_spec(l):
    return _stacked((2, D), l)


def _adaln_kernel(cctx_ref, c_ref, w_ref, b_ref, o_ref, c8_scr):
    c8_scr[0:1, :] = cctx_ref[...]
    c8_scr[1:N_CVEC, :] = c_ref[...]
    c8_scr[N_CVEC:, :] = jnp.zeros((SUBLANE - N_CVEC, D), F32)
    s = _silu(c8_scr[...]).astype(BF16)
    bias = b_ref[pl.ds(pl.program_id(0), 1), :]
    o_ref[...] = jnp.dot(s, w_ref[...].astype(BF16), preferred_element_type=F32) + bias


def _adaln(c_ctx, c, mod_w, mod_b):
    return pl.pallas_call(
        _adaln_kernel,
        grid=(DEPTH, N_MOD * D // TN_MOD),
        in_specs=[
            pl.BlockSpec((1, D), lambda l, n: (0, 0)),
            pl.BlockSpec((DEC_BATCH, D), lambda l, n: (0, 0)),
            pl.BlockSpec((None, D, TN_MOD), lambda l, n: (l, 0, n)),
            pl.BlockSpec((DEPTH, TN_MOD), lambda l, n: (0, n)),
        ],
        out_specs=pl.BlockSpec((None, SUBLANE, TN_MOD), lambda l, n: (l, 0, n)),
        out_shape=jax.ShapeDtypeStruct((DEPTH, SUBLANE, N_MOD * D), F32),
        scratch_shapes=[pltpu.VMEM((SUBLANE, D), F32)],
        compiler_params=_cparams("parallel", "parallel"),
        name="adaln",
    )(c_ctx.reshape(1, D), c, mod_w, mod_b)


@dataclasses.dataclass(frozen=True)
class _MlpCfg:
    tm: int
    tf: int
    split_out: bool
    conformer: bool

    @property
    def nj(self):
        return D_FF // self.tf

    @property
    def nt(self):
        return NTOK // self.tm

    def tile(self, s):
        return jnp.clip(s - (self.nj - 1), 0, self.nt - 1)

    def done(self, s):
        return jnp.clip(s - self.nj, 0, self.nt - 1)


N_CONF_IN, N_CONF_SCRATCH = 11, 4


def _sqrelu(a):
    return jnp.square(jnp.maximum(a, 0.0)).astype(BF16)


def _mlp_kernel(*refs, cfg):
    it = iter(refs)
    take = lambda n: [next(it) for _ in range(n)]
    x_ref, xd_ref, mod_ref, w1_ref, w2_ref, g_ref, b_ref = take(7)
    conf_in = take(N_CONF_IN) if cfg.conformer else None
    o_refs = take(2 if cfg.split_out else 1)
    w1s, w2s, h_scr, acc_scr = take(4)
    y_scr = take(1)[0] if cfg.split_out else None
    conf_scr = take(N_CONF_SCRATCH) if cfg.conformer else None
    nj, nt, tm = cfg.nj, cfg.nt, cfg.tm

    s = pl.program_id(0)
    shift, scale = _mod_rows(mod_ref, cfg.tile(s) * tm, (3, 4))
    done = cfg.done(s)

    def finish(emit_matmuls=None):
        (gate,) = _mod_rows(mod_ref, done * tm, (5,))
        y = _layer_norm(ALPHA * xd_ref[...] + gate * acc_scr[...], g_ref[1:2, :], b_ref[1:2, :])
        if cfg.conformer:
            y = _conformer_tile(y, done, conf_in, conf_scr, emit_matmuls)
        if cfg.split_out:
            y_scr[...] = y
        else:
            o_refs[0][...] = y

    def route():
        if cfg.split_out:
            @pl.when(done < NCTX // tm)
            def _():
                o_refs[0][...] = y_scr[...]

            @pl.when(done >= NCTX // tm)
            def _():
                o_refs[1][...] = y_scr[...]

    @pl.when(s < nj)
    def _():
        w1s[s] = w1_ref[...].astype(BF16)
        w2s[s] = w2_ref[...].astype(BF16)
        if cfg.conformer:
            for src, dst in ((conf_in[1], conf_scr[0]), (conf_in[7], conf_scr[1])):
                rows = D // nj
                dst[pl.ds(pl.multiple_of(s * rows, rows), rows), :] = src[...].astype(BF16)

        @pl.when(s == 0)
        def _():
            h_scr[...] = (x_ref[...] * (1.0 + scale) + shift).astype(BF16)
            acc_scr[...] = jnp.zeros_like(acc_scr)

        a = _sqrelu(jnp.dot(h_scr[...], w1s[s], preferred_element_type=F32))
        acc_scr[...] += jnp.dot(a, w2s[s], preferred_element_type=F32)

    @pl.when(jnp.logical_and(s >= nj, s < nj + nt - 1))
    def _():
        h_scr[...] = (x_ref[...] * (1.0 + scale) + shift).astype(BF16)
        chunks = [(j, slice(c * TC_MLP, (c + 1) * TC_MLP))
                  for j in range(nj) for c in range(cfg.tf // TC_MLP)]
        queue = [("up", 0)]
        for k in range(len(chunks)):
            queue += ([("up", k + 1)] if k + 1 < len(chunks) else []) + [("down", k)]
        act, st = {}, dict(pos=0, acc=None)

        def emit_matmuls(n):
            for kind, k in queue[st["pos"]:st["pos"] + n]:
                j, cols = chunks[k]
                if kind == "up":
                    act[k] = _sqrelu(jnp.dot(h_scr[...], w1s[j, :, cols], preferred_element_type=F32))
                else:
                    part = jnp.dot(act.pop(k), w2s[j, cols, :], preferred_element_type=F32)
                    st["acc"] = part if st["acc"] is None else st["acc"] + part
            st["pos"] = min(st["pos"] + n, len(queue))

        emit_matmuls(2)
        finish(lambda: emit_matmuls(1))
        emit_matmuls(len(queue))
        acc_scr[...] = st["acc"]
        route()

    @pl.when(s == nj + nt - 1)
    def _():
        finish()
        route()


def _mlp(x, mods, l, w1, w2, ln_g, ln_b, split_out, conformer=None):
    cfg = _MlpCfg(tm=TM if conformer else TM_MLP, tf=TC_MLP if conformer else TF_MLP,
                  split_out=split_out, conformer=conformer is not None)
    tm, tf, nj = cfg.tm, cfg.tf, cfg.nj
    nc = NCTX // tm
    chunk = lambda s: jnp.minimum(s, nj - 1)
    in_specs = [
        pl.BlockSpec((tm, D), lambda s: (cfg.tile(s), 0)),
        pl.BlockSpec((tm, D), lambda s: (cfg.done(s), 0)),
        _mod_spec(l),
        pl.BlockSpec((None, D, tf), lambda s: (l, 0, chunk(s))),
        pl.BlockSpec((None, tf, D), lambda s: (l, chunk(s), 0)),
        _ln_spec(l), _ln_spec(l),
    ]
    args = [x, x, mods, w1, w2, ln_g, ln_b]
    scratch = [pltpu.VMEM((nj, D, tf), BF16), pltpu.VMEM((nj, tf, D), BF16),
               pltpu.VMEM((tm, D), BF16), pltpu.VMEM((tm, D), F32)]
    if split_out:
        out_specs = [pl.BlockSpec((tm, D), lambda s: (jnp.minimum(cfg.done(s), nc - 1), 0)),
                     pl.BlockSpec((tm, D), lambda s: (jnp.maximum(cfg.done(s) - nc, 0), 0))]
        out_shape = [jax.ShapeDtypeStruct((NCTX, D), F32), jax.ShapeDtypeStruct((NLAT, D), F32)]
        scratch.append(pltpu.VMEM((tm, D), F32))
    else:
        out_specs = pl.BlockSpec((tm, D), lambda s: (cfg.done(s), 0))
        out_shape = jax.ShapeDtypeStruct((NTOK, D), F32)
    if conformer:
        j, cw1, cb1, wdw, bdw, cg, cb, cw2, cb2 = conformer
        slab = lambda cols: pl.BlockSpec((None, D // nj, cols), lambda s: (j, chunk(s), 0))
        in_specs += [_mod_spec(l + 1), slab(2 * D), _stacked((1, 2 * D), j),
                     pl.BlockSpec((CONF_W, 1, D), lambda s: (j, 0, 0)),
                     _stacked((1, D), j), _stacked((1, D), j), _stacked((1, D), j),
                     slab(D), _stacked((1, D), j),
                     _ln_spec(l + 1), _ln_spec(l + 1)]
        args += [mods, cw1, _vec3(cb1), _vec3(wdw), _vec3(bdw), _vec3(cg), _vec3(cb), cw2, _vec3(cb2),
                 ln_g, ln_b]
        scratch += [pltpu.VMEM((D, 2 * D), BF16), pltpu.VMEM((D, D), BF16),
                    pltpu.VMEM((CONF_ROWS, CONF_LB), F32),
                    pltpu.VMEM((SUBLANE - 1, CONF_ROWS, CONF_LB), F32)]
        assert len(in_specs) == 7 + N_CONF_IN
    return pl.pallas_call(
        functools.partial(_mlp_kernel, cfg=cfg),
        grid=(nj + cfg.nt,),
        in_specs=in_specs,
        out_specs=out_specs,
        out_shape=out_shape,
        scratch_shapes=scratch,
        compiler_params=_cparams("arbitrary"),
        name="mlp_conformer" if conformer else "mlp",
    )(*args)


CONF_PAD = 16
CONF_LB = 256
CONF_ROWS = (TM // GRID_W) * (GRID_W + 2 * CONF_PAD)


def _conf_conv(upad_ref, shf_ref, w_taps, u, joined, between=None):
    nseg = TM // GRID_W
    stride = GRID_W + 2 * CONF_PAD
    zpad = jnp.zeros((CONF_PAD, CONF_LB), F32)
    for s in range(nseg):
        base, r0 = s * stride, s * GRID_W
        above = jnp.where(joined, u[r0 - CONF_PAD:r0, :], 0.0) if s > 0 else zpad
        below = jnp.where(joined, u[r0 + GRID_W:r0 + GRID_W + CONF_PAD, :], 0.0) if s < nseg - 1 else zpad
        upad_ref[base:base + CONF_PAD, :] = above
        upad_ref[base + CONF_PAD:base + CONF_PAD + GRID_W, :] = u[r0:r0 + GRID_W, :]
        upad_ref[base + CONF_PAD + GRID_W:base + stride, :] = below
    for b in range(1, SUBLANE):
        shf_ref[b - 1, 0:CONF_ROWS - SUBLANE, :] = upad_ref[b:b + CONF_ROWS - SUBLANE, :]
    out = []
    for s in range(nseg):
        if between is not None:
            between()
        r0 = s * stride + CONF_PAD
        acc = jnp.zeros((GRID_W, CONF_LB), F32)
        for k in range(CONF_W):
            a, b = divmod(k - CONF_W // 2, SUBLANE)
            rows = slice(r0 + SUBLANE * a, r0 + SUBLANE * a + GRID_W)
            acc = acc + w_taps[k] * (upad_ref[rows, :] if b == 0 else shf_ref[b - 1, rows, :])
        out.append(acc)
    return jnp.concatenate(out, axis=0)


def _conformer_tile(x, t, conf_in, conf_scr, between=None):
    mod_ref, _, b1_ref, wdw_ref, bdw_ref, cg_ref, cb_ref, _, b2_ref, g_ref, b_ref = conf_in
    w1s, w2s, upad, shf = conf_scr
    shift, scale, gate = _mod_rows(mod_ref, t * TM, (0, 1, 2))
    h = (x * (1.0 + scale) + shift).astype(BF16)

    def glu(lb):
        cols = slice(lb * CONF_LB, (lb + 1) * CONF_LB)
        gcols = slice(D + lb * CONF_LB, D + (lb + 1) * CONF_LB)
        a = jnp.dot(h, w1s[:, cols], preferred_element_type=F32) + b1_ref[:, cols]
        g = jnp.dot(h, w1s[:, gcols], preferred_element_type=F32) + b1_ref[:, gcols]
        return a * jax.nn.sigmoid(g)

    n_lb = D // CONF_LB
    conv = []
    u = glu(0)
    for lb in range(n_lb):
        u_next = glu(lb + 1) if lb + 1 < n_lb else None
        cols = slice(lb * CONF_LB, (lb + 1) * CONF_LB)
        taps = [wdw_ref[k, :, cols] for k in range(CONF_W)]
        conv.append(_conf_conv(upad, shf, taps, u, t < CTX_TILES, between))
        u = u_next

    hook = between if between is not None else (lambda: None)
    hook()
    uc = jnp.concatenate(conv, axis=1) + bdw_ref[...]
    uc = _silu(_layer_norm(uc, cg_ref[...], cb_ref[...]))
    y = jnp.dot(uc.astype(BF16), w2s[...], preferred_element_type=F32) + b2_ref[...]
    hook()
    return _layer_norm(ALPHA * x + gate * y, g_ref[0:1, :], b_ref[0:1, :])


SC_TM = 2 * TM

def _sconv_kernel(x_ref, xp_ref, xn_ref, mod_ref, win_ref, wc_ref, wout_ref, g_ref, b_ref, o_ref,
                  wins, wouts, cpad):
    i = pl.program_id(0)
    shift, scale, gate = _mod_rows(mod_ref, i * SC_TM, (0, 1, 2))

    @pl.when(i == 0)
    def _():
        _cast_rows(win_ref, wins, 128)
        _cast_rows(wout_ref, wouts, 128)

    scale = 1.0 + scale
    w0, w1, w2 = wc_ref[0], wc_ref[1], wc_ref[2]

    def finish(x, h, y):
        bg = jnp.dot(h, wins[:, :D], preferred_element_type=F32)
        out = jnp.dot((bg * y).astype(BF16), wouts[...], preferred_element_type=F32)
        o_ref[...] = _layer_norm(ALPHA * x + gate * out, g_ref[0:1, :], b_ref[0:1, :])

    @pl.when(i < NCTX // SC_TM)
    def _():
        x = x_ref[...]
        h = (x * scale + shift).astype(BF16)
        cu = jnp.dot(h, wins[:, D:], preferred_element_type=F32)
        cu = cu[:, :D] * cu[:, D:]
        zrow = jnp.zeros((SUBLANE, D), F32)
        cpad[0:SUBLANE, :] = zrow
        cpad[SUBLANE:SUBLANE + SC_TM, :] = cu
        cpad[SUBLANE + SC_TM:2 * SUBLANE + SC_TM, :] = zrow
        pos = lax.broadcasted_iota(jnp.int32, (SC_TM, D), 0) % SEQ
        before = jnp.where(pos == 0, 0.0, cpad[SUBLANE - 1:SUBLANE - 1 + SC_TM, :])
        after = jnp.where(pos == SEQ - 1, 0.0, cpad[SUBLANE + 1:SUBLANE + 1 + SC_TM, :])
        finish(x, h, w0 * before + w1 * cu + w2 * after)

    @pl.when(i >= NCTX // SC_TM)
    def _():
        blocks_per_seq = DEC_SEQ // SC_TM
        r = (i - NCTX // SC_TM) % blocks_per_seq
        x = x_ref[...]
        h = (x * scale + shift).astype(BF16)
        hp = (xp_ref[...] * scale + shift).astype(BF16)
        hn = (xn_ref[...] * scale + shift).astype(BF16)
        hcat = jnp.concatenate([hp, h, hn], axis=0)
        cu = jnp.dot(hcat, wins[:, D:], preferred_element_type=F32)
        cu = cu[:, :D] * cu[:, D:]
        halo_up = jnp.where(r > 0, cu[0:GRID_W, :], 0.0)
        halo_dn = jnp.where(r < blocks_per_seq - 1, cu[GRID_W + SC_TM:, :], 0.0)
        up = jnp.concatenate([halo_up, cu[GRID_W:SC_TM, :]], axis=0)
        dn = jnp.concatenate([cu[2 * GRID_W:GRID_W + SC_TM, :], halo_dn], axis=0)
        finish(x, h, w0 * up + w1 * cu[GRID_W:GRID_W + SC_TM, :] + w2 * dn)


def _short_conv(x, mods, l, j, w_in, w_conv, w_out, ln_g, ln_b):
    halo_per_tile = SC_TM // GRID_W
    n_halo = NTOK // GRID_W
    return pl.pallas_call(
        _sconv_kernel,
        grid=(NTOK // SC_TM,),
        in_specs=[
            pl.BlockSpec((SC_TM, D), lambda i: (i, 0)),
            pl.BlockSpec((GRID_W, D), lambda i: (jnp.maximum(i * halo_per_tile - 1, 0), 0)),
            pl.BlockSpec((GRID_W, D), lambda i: (jnp.minimum((i + 1) * halo_per_tile, n_halo - 1), 0)),
            _mod_spec(l),
            _stacked((D, 3 * D), j, True),
            pl.BlockSpec((3, 1, D), lambda i: (j, 0, 0)),
            _stacked((D, D), j, True),
            _ln_spec(l), _ln_spec(l),
        ],
        out_specs=pl.BlockSpec((SC_TM, D), lambda i: (i, 0)),
        out_shape=jax.ShapeDtypeStruct((NTOK, D), F32),
        scratch_shapes=[pltpu.VMEM((D, 3 * D), BF16), pltpu.VMEM((D, D), BF16),
                        pltpu.VMEM((SC_TM + 2 * SUBLANE, D), F32)],
        compiler_params=_cparams("arbitrary"),
        name="short_conv",
    )(x, x, x, mods, w_in, _vec3(w_conv), w_out, ln_g, ln_b)


N_CH = TM // CHUNK
GLA_FWD_TILES = 2
GLA_BWD_TILES = 2
STATE = (H, DK, DV)


def _state_out_spec(block, j, d, n):
    return pl.BlockSpec((n, None, None) + STATE,
                        lambda i: (jnp.minimum(block(i), CTX_TILES // n - 1), j, d, 0, 0, 0))


def _emit_states(st_ref, finals, t0, create=None):
    @pl.when(t0 < CTX_TILES)
    def _():
        if create is not None:
            st_ref[...] = jnp.zeros(st_ref.shape, F32)
        for sub, final in enumerate(finals):
            for hd in range(H):
                if create is None:
                    st_ref[sub, hd] = final[hd]
                else:
                    st_ref[sub, create, 0, hd] = final[hd]


def _lat_seq(t):
    return jnp.clip((t - CTX_TILES) // TILES_PER_LAT, 0, DEC_BATCH - 1)


def _chunk_tri(rev):
    row = lax.broadcasted_iota(jnp.int32, (TM, TM), 0)
    col = lax.broadcasted_iota(jnp.int32, (TM, TM), 1)
    same = (row // CHUNK) == (col // CHUNK)
    return same & ((col >= row) if rev else (col <= row))


def _scan_states_in(s_scr, s0_ref, tiles, rev):
    first_r = TILES_PER_LAT - 1 if rev else 0

    def enter(prev, t, hd):
        fresh = jnp.logical_and(t >= CTX_TILES, (t - CTX_TILES) % TILES_PER_LAT == first_r)
        return jnp.where(t < CTX_TILES, 0.0, jnp.where(fresh, s0_ref[hd], prev))

    s_in = [enter(s_scr[hd], tiles[0], hd) for hd in range(H)]
    for p, t in enumerate(tiles[1:], start=1):
        s_in += [functools.partial(lambda done, p, t, hd: enter(done[(p - 1) * H + hd], t, hd),
                                   p=p, t=t, hd=hd) for hd in range(H)]
    return s_in


def _decay_operands(q, k, bcum, rev):
    qd, kd, ke, dec = [], [], [], []
    for c in range(N_CH):
        rows = slice(c * CHUNK, (c + 1) * CHUNK)
        b = bcum[rows, :]
        last = b[0:1, :] if rev else b[CHUNK - 1:CHUNK, :]
        qd.append((q[rows, :] * jnp.exp(b)).astype(BF16))
        kd.append((k[rows, :] * jnp.exp(-b)).astype(BF16))
        ke.append(k[rows, :] * jnp.exp(last - b))
        dec.append(jnp.exp(last))
    return (jnp.concatenate(qd, axis=0), jnp.concatenate(kd, axis=0),
            jnp.concatenate(ke, axis=0).T.astype(BF16), dec)


def _scan_tile(heads, s_in, rev):
    mask = _chunk_tri(rev)
    kcol = lax.broadcasted_iota(jnp.int32, (DK, TM), 1) // CHUNK
    order = range(N_CH - 1, -1, -1) if rev else range(N_CH)
    sc, kv = [], []
    for qh, kh, keth, vh, _ in heads:
        sc.append(lax.dot_general(qh, kh, (((1,), (1,)), ((), ())), preferred_element_type=F32))
        kst = jnp.concatenate([jnp.where(kcol == c, keth, jnp.zeros_like(keth)) for c in range(N_CH)],
                              axis=0)
        kv.append(jnp.dot(kst, vh, preferred_element_type=F32))
    o_intra = [jnp.dot(jnp.where(mask, s, 0.0).astype(BF16), hd[3], preferred_element_type=F32)
               for s, hd in zip(sc, heads)]
    outs, s_out = [], []
    for hd, (qh, _, _, _, dec_rows) in enumerate(heads):
        s, o = (s_in[hd](s_out) if callable(s_in[hd]) else s_in[hd]), [None] * N_CH
        for c in order:
            rows = slice(c * CHUNK, (c + 1) * CHUNK)
            o[c] = o_intra[hd][rows, :] + jnp.dot(qh[rows, :], s.astype(BF16),
                                                  preferred_element_type=F32)
            dec_col = jnp.broadcast_to(dec_rows[c], (DK, DK)).T
            s = s * jnp.concatenate([dec_col, dec_col], axis=1) + kv[hd][c * DK:(c + 1) * DK, :]
        outs.append(o)
        s_out.append(s)
    return outs, s_out


def _gla_fwd_kernel(*refs, n_x, create):
    x_refs = refs[:n_x]
    mod_ref, win_ref, wga_ref, wgb_ref, bg_ref, s0_ref = refs[n_x:n_x + 6]
    rest = refs[n_x + (6 if create is not None else 7):]
    of_ref, qd_ref, kd_ref, ket_ref, v_ref, r_ref, dec_ref, st_ref = rest[:8]
    xall_ref = rest[8] if n_x == 2 else None
    wins, tri_scr, s_scr = rest[8 + (n_x == 2):]
    i = pl.program_id(0)

    @pl.when(i == 0)
    def _():
        _cast_rows(win_ref, wins, 128)
        tri_scr[0] = jnp.where(_chunk_tri(False), 1.0, 0.0).astype(BF16)
        tri_scr[1] = jnp.where(_chunk_tri(True), 1.0, 0.0).astype(BF16)
        s_scr[...] = jnp.zeros_like(s_scr)

    subs = range(GLA_FWD_TILES)
    tiles = [i * GLA_FWD_TILES + sub for sub in subs]
    rows = [slice(sub * TM, (sub + 1) * TM) for sub in subs]

    carry, finals = s_scr, []
    for sub in subs:
        heads = []
        shift, scale = _mod_rows(mod_ref, tiles[sub] * TM, (0, 1))
        xt = _read_x(x_refs, tiles[sub], rows[sub])
        if xall_ref is not None:
            xall_ref[rows[sub], :] = xt
        h = (xt * (1.0 + scale) + shift).astype(BF16)

        g_parts = []
        for d in range(2):
            za = lax.dot_general(h, wga_ref[d].astype(BF16), (((1,), (1,)), ((), ())),
                                 preferred_element_type=F32)
            z = (jnp.dot(za.astype(BF16), wgb_ref[d].astype(BF16), preferred_element_type=F32)
                 + bg_ref[d:d + 1, :])
            g = (jnp.minimum(z, 0.0) - jnp.log1p(jnp.exp(-jnp.abs(z)))) * (1.0 / GATE_NORM)
            g_hi = g.astype(BF16)
            g_parts.append((g_hi, (g - g_hi.astype(F32)).astype(BF16)))

        proj = jnp.dot(h, wins[...], preferred_element_type=F32)
        q = proj[:, :KW] * (DK ** -0.5)
        k = proj[:, KW:2 * KW]
        v = proj[:, 2 * KW:2 * KW + D].astype(BF16)
        v_ref[rows[sub], :] = v
        r_ref[rows[sub], :] = proj[:, 2 * KW + D:].astype(BF16)

        bcum = [jnp.dot(tri_scr[d], g_hi, preferred_element_type=F32)
                + jnp.dot(tri_scr[d], g_lo, preferred_element_type=F32)
                for d, (g_hi, g_lo) in enumerate(g_parts)]

        qd_b, kd_b, ket_b, dec_b = _decay_operands(q, k, bcum[1], True)
        qd_ref[rows[sub], :] = qd_b
        kd_ref[rows[sub], :] = kd_b
        ket_ref[:, rows[sub]] = ket_b
        dec_ref[sub] = jnp.concatenate(dec_b + [jnp.zeros((SUBLANE - N_CH, KW), F32)], axis=0)
        qd, kd, ket, dec = _decay_operands(q, k, bcum[0], False)
        for hd in range(H):
            kc, vc = slice(hd * DK, (hd + 1) * DK), slice(hd * DV, (hd + 1) * DV)
            heads.append((qd[:, kc], kd[:, kc], ket[kc, :], v[:, vc], [e[:, kc] for e in dec]))

        outs, carry = _scan_tile(heads, _scan_states_in(carry, s0_ref, [tiles[sub]], False), False)
        finals.append(carry)
        for hd in range(H):
            for c in range(N_CH):
                r0 = sub * TM + c * CHUNK
                of_ref[r0:r0 + CHUNK, hd * DV:(hd + 1) * DV] = outs[hd][c].astype(BF16)
    for hd in range(H):
        s_scr[hd] = carry[hd]
    _emit_states(st_ref, finals, tiles[0], create)


def _gla_fwd(x, mods, l, j, w_in, w_ga, w_gb, b_g, state_gla, new_states):
    xs = _x_args(x)
    n, rows = GLA_FWD_TILES, GLA_FWD_TILES * TM
    tile = pl.BlockSpec((rows, D), lambda i: (i, 0))
    keys = pl.BlockSpec((rows, KW), lambda i: (i, 0))
    n_in = len(xs) + 7
    create = new_states is None
    n_gla = state_gla.shape[1]
    if create:
        st_spec = pl.BlockSpec((n, n_gla, 2) + STATE,
                               lambda i: (jnp.minimum(i, CTX_TILES // n - 1), 0, 0, 0, 0, 0))
        st_in, st_args, alias = [], [], {}
    else:
        st_spec = _state_out_spec(lambda i: i, j, 0, n)
        st_in, st_args, alias = [pl.BlockSpec(memory_space=pl.ANY)], [new_states], {n_in - 1: 7}
    return pl.pallas_call(
        functools.partial(_gla_fwd_kernel, n_x=len(xs), create=j if create else None),
        grid=(NTILE // n,),
        in_specs=_x_specs(len(xs) == 2, rows=rows) + [
            _mod_spec(l),
            _stacked((D, 2 * KW + 2 * D), j, True),
            _stacked((2, RANK, D), j),
            _stacked((2, RANK, KW), j),
            _stacked((2, KW), j),
            pl.BlockSpec((None, None, None) + STATE, lambda i: (_lat_seq(i * n), j, 0, 0, 0, 0)),
        ] + st_in,
        out_specs=[
            tile, keys, keys,
            pl.BlockSpec((KW, rows), lambda i: (0, i)),
            tile, tile,
            pl.BlockSpec((n, SUBLANE, KW), lambda i: (i, 0, 0)),
            st_spec,
        ] + [pl.BlockSpec((rows, D), lambda i: (i, 0))] * (len(xs) == 2),
        out_shape=[
            jax.ShapeDtypeStruct((NTOK, D), BF16),
            jax.ShapeDtypeStruct((NTOK, KW), BF16),
            jax.ShapeDtypeStruct((NTOK, KW), BF16),
            jax.ShapeDtypeStruct((KW, NTOK), BF16),
            jax.ShapeDtypeStruct((NTOK, D), BF16),
            jax.ShapeDtypeStruct((NTOK, D), BF16),
            jax.ShapeDtypeStruct((NTILE, SUBLANE, KW), F32),
            jax.ShapeDtypeStruct((BATCH, n_gla, 2) + STATE, F32),
        ] + [jax.ShapeDtypeStruct((NTOK, D), F32)] * (len(xs) == 2),
        input_output_aliases=alias,
        scratch_shapes=[pltpu.VMEM((D, 2 * KW + 2 * D), BF16), pltpu.VMEM((2, TM, TM), BF16),
                        pltpu.VMEM(STATE, F32)],
        compiler_params=_cparams("arbitrary"),
        name="gla_fwd",
    )(*xs, mods, w_in, jnp.swapaxes(w_ga, 2, 3), w_gb, b_g, state_gla, *st_args)


def _gla_bwd_kernel(*refs, n_x, j):
    x_refs = refs[:n_x]
    (qd_ref, kd_ref, ket_ref, v_ref, dec_ref, s0_ref, of_ref, r_ref, mod_ref, gn_ref, wo_ref,
     g_ref, b_ref, _, o_ref, st_ref, wos, s_scr) = refs[n_x:]
    i = pl.program_id(0)
    n = GLA_BWD_TILES
    blk = NTILE // n - 1 - i

    @pl.when(i == 0)
    def _():
        _cast_rows(wo_ref, wos, 128)
        s_scr[...] = jnp.zeros_like(s_scr)

    order = list(range(n - 1, -1, -1))
    tiles = [blk * n + sub for sub in order]
    heads = []
    for sub in order:
        rows = slice(sub * TM, (sub + 1) * TM)
        for hd in range(H):
            kc, vc = slice(hd * DK, (hd + 1) * DK), slice(hd * DV, (hd + 1) * DV)
            heads.append((qd_ref[rows, kc], kd_ref[rows, kc], ket_ref[kc, rows], v_ref[rows, vc],
                          [dec_ref[sub, c:c + 1, kc] for c in range(N_CH)]))
    outs, s_out = _scan_tile(heads, _scan_states_in(s_scr, s0_ref, tiles, True), True)

    finals = [None] * n
    for p, sub in enumerate(order):
        rows = slice(sub * TM, (sub + 1) * TM)
        finals[sub] = s_out[p * H:(p + 1) * H]
        (gate,) = _mod_rows(mod_ref, tiles[p] * TM, (2,))
        parts = []
        for hd in range(H):
            oh = (of_ref[rows, hd * DV:(hd + 1) * DV].astype(F32)
                  + jnp.concatenate(outs[p * H + hd], axis=0))
            ms = jnp.mean(oh * oh, axis=-1, keepdims=True)
            parts.append(oh * lax.rsqrt(ms + RMS_EPS))
        on = jnp.concatenate(parts, axis=1) * gn_ref[j:j + 1, :]
        y = jnp.dot((on * _silu(r_ref[rows, :].astype(F32))).astype(BF16), wos[...],
                    preferred_element_type=F32)
        o_ref[rows, :] = _layer_norm(ALPHA * _read_x(x_refs, tiles[p], rows) + gate * y,
                                     g_ref[0:1, :], b_ref[0:1, :])
    for hd in range(H):
        s_scr[hd] = s_out[(n - 1) * H + hd]
    _emit_states(st_ref, finals, blk * n)


def _gla_bwd(x, o_f, qd, kd, ket, v, r, dec, state_gla, new_states, mods, l, j, gn_g, w_o, ln_g, ln_b):
    xs = _x_args(x)
    n, rows = GLA_BWD_TILES, GLA_BWD_TILES * TM
    rblk = lambda i: NTILE // n - 1 - i
    tile = pl.BlockSpec((rows, D), lambda i: (rblk(i), 0))
    keys = pl.BlockSpec((rows, KW), lambda i: (rblk(i), 0))
    n_in = len(xs) + 14
    return pl.pallas_call(
        functools.partial(_gla_bwd_kernel, n_x=len(xs), j=j),
        grid=(NTILE // n,),
        in_specs=_x_specs(len(xs) == 2, rblk, rows) + [
            keys, keys,
            pl.BlockSpec((KW, rows), lambda i: (0, rblk(i))),
            tile,
            pl.BlockSpec((n, SUBLANE, KW), lambda i: (rblk(i), 0, 0)),
            pl.BlockSpec((None, None, None) + STATE,
                         lambda i: (_lat_seq(rblk(i) * n), j, 1, 0, 0, 0)),
            tile, tile, _mod_spec(l),
            pl.BlockSpec(gn_g.shape, lambda i: (0, 0)), _stacked((D, D), j, True),
            _ln_spec(l), _ln_spec(l),
            pl.BlockSpec(memory_space=pl.ANY)],
        out_specs=[tile, _state_out_spec(rblk, j, 1, n)],
        out_shape=[jax.ShapeDtypeStruct((NTOK, D), F32), jax.ShapeDtypeStruct(new_states.shape, F32)],
        input_output_aliases={n_in - 1: 1},
        scratch_shapes=[pltpu.VMEM((D, D), BF16), pltpu.VMEM(STATE, F32)],
        compiler_params=_cparams("arbitrary"),
        name="gla_bwd",
    )(*xs, qd, kd, ket, v, dec, state_gla, o_f, r, mods, gn_g, w_o, ln_g, ln_b,
      new_states)


def kernel(x_prompt, x_sample, c, state_gla, c_ctx, mod_w, mod_b, ln_g, ln_b, ff_w1, ff_w2, gla_w_in, gla_w_ga, gla_w_gb, gla_b_g, gla_gn_g, gla_w_o, conf_w_pw1, conf_b_pw1, conf_w_dw, conf_b_dw, conf_ln_g, conf_ln_b, conf_w_pw2, conf_b_pw2, sc_w_in, sc_w_conv, sc_w_out):
    assert x_prompt.shape == (BATCH, SEQ, D) and x_sample.shape == (DEC_BATCH, DEC_SEQ, D)
    x = (x_prompt.reshape(NCTX, D), x_sample.reshape(NLAT, D))

    mods = _adaln(c_ctx, c, mod_w, mod_b)

    states = None
    for l in range(DEPTH):
        kind, j = l % 3, l // 3
        if kind == 0:
            o_f, qd, kd, ket, v, r, dec, states, *x_all = _gla_fwd(x, mods, l, j, gla_w_in, gla_w_ga,
                                                                   gla_w_gb, gla_b_g, state_gla, states)
            x = x_all[0] if x_all else x
            x, states = _gla_bwd(x, o_f, qd, kd, ket, v, r, dec, state_gla, states, mods, l, j,
                                 gla_gn_g, gla_w_o, ln_g, ln_b)
        elif kind == 2:
            x = _short_conv(x, mods, l, j, sc_w_in, sc_w_conv, sc_w_out, ln_g, ln_b)
        conformer = None
        if l + 1 < DEPTH and (l + 1) % 3 == 1:
            conformer = ((l + 1) // 3, conf_w_pw1, conf_b_pw1, conf_w_dw, conf_b_dw, conf_ln_g, conf_ln_b,
                         conf_w_pw2, conf_b_pw2)
        x = _mlp(x, mods, l, ff_w1, ff_w2, ln_g, ln_b, split_out=(l == DEPTH - 1), conformer=conformer)

    y_prompt, y_sample = x
    return (y_prompt.reshape(BATCH, SEQ, D), y_sample.reshape(DEC_BATCH, DEC_SEQ, D), states)
```

```python
import dataclasses
import functools

import jax
import jax.numpy as jnp
from jax import lax
from jax.experimental import pallas as pl
from jax.experimental.pallas import tpu as pltpu

F32 = jnp.float32
BF16 = jnp.bfloat16

D = 1024
DEPTH = 4
BATCH, SEQ = 16, 256
DEC_BATCH, DEC_SEQ = 2, 2048
GRID_W = 64
N_MOD = 6
N_CVEC = 1 + DEC_BATCH
NCTX = BATCH * SEQ
NLAT = DEC_BATCH * DEC_SEQ
NTOK = NCTX + NLAT
H, DK, DV = 4, 128, 256
KW = H * DK
RANK = 16
CHUNK = 64
GATE_NORM = 16.0
CONF_W = 31
D_FF = 4 * D
LN_EPS = 1e-5
RMS_EPS = 1e-6
ALPHA = (2 * DEPTH) ** 0.25

TM = 256
NTILE = NTOK // TM
CTX_TILES = NCTX // TM
TILES_PER_LAT = DEC_SEQ // TM
TM_MLP = 512
TF_MLP = 1024
TC_MLP = 512
TN_MOD = 2048
SUBLANE = 8
VMEM_LIMIT = 58 * 1024 * 1024


def _cparams(*sem):
    return pltpu.CompilerParams(dimension_semantics=sem, vmem_limit_bytes=VMEM_LIMIT)


def _mod_rows(mod_ref, row0, ks):
    m = jnp.where(row0 < NCTX, 0, 1 + (row0 - NCTX) // DEC_SEQ)
    return [mod_ref[pl.ds(m, 1), k * D:(k + 1) * D] for k in ks]


def _mod_spec(l):
    return pl.BlockSpec((None, SUBLANE, N_MOD * D), lambda *_: (l, 0, 0))


def _stacked(block, j, single_buffer=False):
    nd = len(block)
    mode = dict(pipeline_mode=pl.Buffered(1)) if single_buffer else {}
    return pl.BlockSpec((None,) + block, lambda *_: (j,) + (0,) * nd, **mode)


def _x_specs(split, block=lambda i: i, rows=TM):
    if not split:
        return [pl.BlockSpec((rows, D), lambda i: (block(i), 0))]
    nc, nl = NCTX // rows, NLAT // rows
    return [pl.BlockSpec((rows, D), lambda i: (jnp.clip(block(i), 0, nc - 1), 0)),
            pl.BlockSpec((rows, D), lambda i: (jnp.clip(block(i) - nc, 0, nl - 1), 0))]


def _x_args(x):
    return list(x) if isinstance(x, tuple) else [x]


def _read_x(x_refs, t, rows=slice(None)):
    if len(x_refs) == 1:
        return x_refs[0][rows, :]
    return jnp.where(t < CTX_TILES, x_refs[0][rows, :], x_refs[1][rows, :])


def _layer_norm(y, g, b):
    mu = jnp.mean(y, axis=-1, keepdims=True)
    yc = y - mu
    var = jnp.mean(yc * yc, axis=-1, keepdims=True)
    return yc * lax.rsqrt(var + LN_EPS) * g + b


def _silu(x):
    return x * jax.nn.sigmoid(x)


def _cast_rows(src_ref, dst_ref, step):
    n = src_ref.shape[0] // step

    def body(r, c):
        rows = pl.ds(pl.multiple_of(r * step, step), step)
        dst_ref[rows, :] = src_ref[rows, :].astype(BF16)
        return c

    lax.fori_loop(0, n, body, 0)


def _vec3(a):
    return a.reshape(-1, 1, a.shape[-1])


def _ln_spec(l):
    return _stacked((2, D), l)


def _adaln_kernel(cctx_ref, c_ref, w_ref, b_ref, o_ref, c8_scr):
    c8_scr[0:1, :] = cctx_ref[...]
    c8_scr[1:N_CVEC, :] = c_ref[...]
    c8_scr[N_CVEC:, :] = jnp.zeros((SUBLANE - N_CVEC, D), F32)
    s = _silu(c8_scr[...]).astype(BF16)
    bias = b_ref[pl.ds(pl.program_id(0), 1), :]
    o_ref[...] = jnp.dot(s, w_ref[...].astype(BF16), preferred_element_type=F32) + bias


def _adaln(c_ctx, c, mod_w, mod_b):
    return pl.pallas_call(
        _adaln_kernel,
        grid=(DEPTH, N_MOD * D // TN_MOD),
        in_specs=[
            pl.BlockSpec((1, D), lambda l, n: (0, 0)),
            pl.BlockSpec((DEC_BATCH, D), lambda l, n: (0, 0)),
            pl.BlockSpec((None, D, TN_MOD), lambda l, n: (l, 0, n)),
            pl.BlockSpec((DEPTH, TN_MOD), lambda l, n: (0, n)),
        ],
        out_specs=pl.BlockSpec((None, SUBLANE, TN_MOD), lambda l, n: (l, 0, n)),
        out_shape=jax.ShapeDtypeStruct((DEPTH, SUBLANE, N_MOD * D), F32),
        scratch_shapes=[pltpu.VMEM((SUBLANE, D), F32)],
        compiler_params=_cparams("parallel", "parallel"),
        name="adaln",
    )(c_ctx.reshape(1, D), c, mod_w, mod_b)


@dataclasses.dataclass(frozen=True)
class _MlpCfg:
    tm: int
    tf: int
    split_out: bool
    conformer: bool

    @property
    def nj(self):
        return D_FF // self.tf

    @property
    def nt(self):
        return NTOK // self.tm

    def tile(self, s):
        return jnp.clip(s - (self.nj - 1), 0, self.nt - 1)

    def done(self, s):
        return jnp.clip(s - self.nj, 0, self.nt - 1)


N_CONF_IN, N_CONF_SCRATCH = 11, 4


def _sqrelu(a):
    return jnp.square(jnp.maximum(a, 0.0)).astype(BF16)


def _mlp_kernel(*refs, cfg):
    it = iter(refs)
    take = lambda n: [next(it) for _ in range(n)]
    x_ref, xd_ref, mod_ref, w1_ref, w2_ref, g_ref, b_ref = take(7)
    conf_in = take(N_CONF_IN) if cfg.conformer else None
    o_refs = take(2 if cfg.split_out else 1)
    w1s, w2s, h_scr, acc_scr = take(4)
    y_scr = take(1)[0] if cfg.split_out else None
    conf_scr = take(N_CONF_SCRATCH) if cfg.conformer else None
    nj, nt, tm = cfg.nj, cfg.nt, cfg.tm

    s = pl.program_id(0)
    shift, scale = _mod_rows(mod_ref, cfg.tile(s) * tm, (3, 4))
    done = cfg.done(s)

    def finish(emit_matmuls=None):
        (gate,) = _mod_rows(mod_ref, done * tm, (5,))
        y = _layer_norm(ALPHA * xd_ref[...] + gate * acc_scr[...], g_ref[1:2, :], b_ref[1:2, :])
        if cfg.conformer:
            y = _conformer_tile(y, done, conf_in, conf_scr, emit_matmuls)
        if cfg.split_out:
            y_scr[...] = y
        else:
            o_refs[0][...] = y

    def route():
        if cfg.split_out:
            @pl.when(done < NCTX // tm)
            def _():
                o_refs[0][...] = y_scr[...]

            @pl.when(done >= NCTX // tm)
            def _():
                o_refs[1][...] = y_scr[...]

    @pl.when(s < nj)
    def _():
        w1s[s] = w1_ref[...].astype(BF16)
        w2s[s] = w2_ref[...].astype(BF16)
        if cfg.conformer:
            for src, dst in ((conf_in[1], conf_scr[0]), (conf_in[7], conf_scr[1])):
                rows = D // nj
                dst[pl.ds(pl.multiple_of(s * rows, rows), rows), :] = src[...].astype(BF16)

        @pl.when(s == 0)
        def _():
            h_scr[...] = (x_ref[...] * (1.0 + scale) + shift).astype(BF16)
            acc_scr[...] = jnp.zeros_like(acc_scr)

        a = _sqrelu(jnp.dot(h_scr[...], w1s[s], preferred_element_type=F32))
        acc_scr[...] += jnp.dot(a, w2s[s], preferred_element_type=F32)

    @pl.when(jnp.logical_and(s >= nj, s < nj + nt - 1))
    def _():
        h_scr[...] = (x_ref[...] * (1.0 + scale) + shift).astype(BF16)
        chunks = [(j, slice(c * TC_MLP, (c + 1) * TC_MLP))
                  for j in range(nj) for c in range(cfg.tf // TC_MLP)]
        queue = [("up", 0)]
        for k in range(len(chunks)):
            queue += ([("up", k + 1)] if k + 1 < len(chunks) else []) + [("down", k)]
        act, st = {}, dict(pos=0, acc=None)

        def emit_matmuls(n):
            for kind, k in queue[st["pos"]:st["pos"] + n]:
                j, cols = chunks[k]
                if kind == "up":
                    act[k] = _sqrelu(jnp.dot(h_scr[...], w1s[j, :, cols], preferred_element_type=F32))
                else:
                    part = jnp.dot(act.pop(k), w2s[j, cols, :], preferred_element_type=F32)
                    st["acc"] = part if st["acc"] is None else st["acc"] + part
            st["pos"] = min(st["pos"] + n, len(queue))

        emit_matmuls(2)
        finish(lambda: emit_matmuls(1))
        emit_matmuls(len(queue))
        acc_scr[...] = st["acc"]
        route()

    @pl.when(s == nj + nt - 1)
    def _():
        finish()
        route()


def _mlp(x, mods, l, w1, w2, ln_g, ln_b, split_out, conformer=None):
    cfg = _MlpCfg(tm=TM if conformer else TM_MLP, tf=TC_MLP if conformer else TF_MLP,
                  split_out=split_out, conformer=conformer is not None)
    tm, tf, nj = cfg.tm, cfg.tf, cfg.nj
    nc = NCTX // tm
    chunk = lambda s: jnp.minimum(s, nj - 1)
    in_specs = [
        pl.BlockSpec((tm, D), lambda s: (cfg.tile(s), 0)),
        pl.BlockSpec((tm, D), lambda s: (cfg.done(s), 0)),
        _mod_spec(l),
        pl.BlockSpec((None, D, tf), lambda s: (l, 0, chunk(s))),
        pl.BlockSpec((None, tf, D), lambda s: (l, chunk(s), 0)),
        _ln_spec(l), _ln_spec(l),
    ]
    args = [x, x, mods, w1, w2, ln_g, ln_b]
    scratch = [pltpu.VMEM((nj, D, tf), BF16), pltpu.VMEM((nj, tf, D), BF16),
               pltpu.VMEM((tm, D), BF16), pltpu.VMEM((tm, D), F32)]
    if split_out:
        out_specs = [pl.BlockSpec((tm, D), lambda s: (jnp.minimum(cfg.done(s), nc - 1), 0)),
                     pl.BlockSpec((tm, D), lambda s: (jnp.maximum(cfg.done(s) - nc, 0), 0))]
        out_shape = [jax.ShapeDtypeStruct((NCTX, D), F32), jax.ShapeDtypeStruct((NLAT, D), F32)]
        scratch.append(pltpu.VMEM((tm, D), F32))
    else:
        out_specs = pl.BlockSpec((tm, D), lambda s: (cfg.done(s), 0))
        out_shape = jax.ShapeDtypeStruct((NTOK, D), F32)
    if conformer:
        j, cw1, cb1, wdw, bdw, cg, cb, cw2, cb2 = conformer
        slab = lambda cols: pl.BlockSpec((None, D // nj, cols), lambda s: (j, chunk(s), 0))
        in_specs += [_mod_spec(l + 1), slab(2 * D), _stacked((1, 2 * D), j),
                     pl.BlockSpec((CONF_W, 1, D), lambda s: (j, 0, 0)),
                     _stacked((1, D), j), _stacked((1, D), j), _stacked((1, D), j),
                     slab(D), _stacked((1, D), j),
                     _ln_spec(l + 1), _ln_spec(l + 1)]
        args += [mods, cw1, _vec3(cb1), _vec3(wdw), _vec3(bdw), _vec3(cg), _vec3(cb), cw2, _vec3(cb2),
                 ln_g, ln_b]
        scratch += [pltpu.VMEM((D, 2 * D), BF16), pltpu.VMEM((D, D), BF16),
                    pltpu.VMEM((CONF_ROWS, CONF_LB), F32),
                    pltpu.VMEM((SUBLANE - 1, CONF_ROWS, CONF_LB), F32)]
        assert len(in_specs) == 7 + N_CONF_IN
    return pl.pallas_call(
        functools.partial(_mlp_kernel, cfg=cfg),
        grid=(nj + cfg.nt,),
        in_specs=in_specs,
        out_specs=out_specs,
        out_shape=out_shape,
        scratch_shapes=scratch,
        compiler_params=_cparams("arbitrary"),
        name="mlp_conformer" if conformer else "mlp",
    )(*args)


CONF_PAD = 16
CONF_LB = 256
CONF_ROWS = (TM // GRID_W) * (GRID_W + 2 * CONF_PAD)


def _conf_conv(upad_ref, shf_ref, w_taps, u, joined, between=None):
    nseg = TM // GRID_W
    stride = GRID_W + 2 * CONF_PAD
    zpad = jnp.zeros((CONF_PAD, CONF_LB), F32)
    for s in range(nseg):
        base, r0 = s * stride, s * GRID_W
        above = jnp.where(joined, u[r0 - CONF_PAD:r0, :], 0.0) if s > 0 else zpad
        below = jnp.where(joined, u[r0 + GRID_W:r0 + GRID_W + CONF_PAD, :], 0.0) if s < nseg - 1 else zpad
        upad_ref[base:base + CONF_PAD, :] = above
        upad_ref[base + CONF_PAD:base + CONF_PAD + GRID_W, :] = u[r0:r0 + GRID_W, :]
        upad_ref[base + CONF_PAD + GRID_W:base + stride, :] = below
    for b in range(1, SUBLANE):
        shf_ref[b - 1, 0:CONF_ROWS - SUBLANE, :] = upad_ref[b:b + CONF_ROWS - SUBLANE, :]
    out = []
    for s in range(nseg):
        if between is not None:
            between()
        r0 = s * stride + CONF_PAD
        acc = jnp.zeros((GRID_W, CONF_LB), F32)
        for k in range(CONF_W):
            a, b = divmod(k - CONF_W // 2, SUBLANE)
            rows = slice(r0 + SUBLANE * a, r0 + SUBLANE * a + GRID_W)
            acc = acc + w_taps[k] * (upad_ref[rows, :] if b == 0 else shf_ref[b - 1, rows, :])
        out.append(acc)
    return jnp.concatenate(out, axis=0)


def _conformer_tile(x, t, conf_in, conf_scr, between=None):
    mod_ref, _, b1_ref, wdw_ref, bdw_ref, cg_ref, cb_ref, _, b2_ref, g_ref, b_ref = conf_in
    w1s, w2s, upad, shf = conf_scr
    shift, scale, gate = _mod_rows(mod_ref, t * TM, (0, 1, 2))
    h = (x * (1.0 + scale) + shift).astype(BF16)

    def glu(lb):
        cols = slice(lb * CONF_LB, (lb + 1) * CONF_LB)
        gcols = slice(D + lb * CONF_LB, D + (lb + 1) * CONF_LB)
        a = jnp.dot(h, w1s[:, cols], preferred_element_type=F32) + b1_ref[:, cols]
        g = jnp.dot(h, w1s[:, gcols], preferred_element_type=F32) + b1_ref[:, gcols]
        return a * jax.nn.sigmoid(g)

    n_lb = D // CONF_LB
    conv = []
    u = glu(0)
    for lb in range(n_lb):
        u_next = glu(lb + 1) if lb + 1 < n_lb else None
        cols = slice(lb * CONF_LB, (lb + 1) * CONF_LB)
        taps = [wdw_ref[k, :, cols] for k in range(CONF_W)]
        conv.append(_conf_conv(upad, shf, taps, u, t < CTX_TILES, between))
        u = u_next

    hook = between if between is not None else (lambda: None)
    hook()
    uc = jnp.concatenate(conv, axis=1) + bdw_ref[...]
    uc = _silu(_layer_norm(uc, cg_ref[...], cb_ref[...]))
    y = jnp.dot(uc.astype(BF16), w2s[...], preferred_element_type=F32) + b2_ref[...]
    hook()
    return _layer_norm(ALPHA * x + gate * y, g_ref[0:1, :], b_ref[0:1, :])


SC_TM = 2 * TM

def _sconv_kernel(x_ref, xp_ref, xn_ref, mod_ref, win_ref, wc_ref, wout_ref, g_ref, b_ref, o_ref,
                  wins, wouts, cpad):
    i = pl.program_id(0)
    shift, scale, gate = _mod_rows(mod_ref, i * SC_TM, (0, 1, 2))

    @pl.when(i == 0)
    def _():
        _cast_rows(win_ref, wins, 128)
        _cast_rows(wout_ref, wouts, 128)

    scale = 1.0 + scale
    w0, w1, w2 = wc_ref[0], wc_ref[1], wc_ref[2]

    def finish(x, h, y):
        bg = jnp.dot(h, wins[:, :D], preferred_element_type=F32)
        out = jnp.dot((bg * y).astype(BF16), wouts[...], preferred_element_type=F32)
        o_ref[...] = _layer_norm(ALPHA * x + gate * out, g_ref[0:1, :], b_ref[0:1, :])

    @pl.when(i < NCTX // SC_TM)
    def _():
        x = x_ref[...]
        h = (x * scale + shift).astype(BF16)
        cu = jnp.dot(h, wins[:, D:], preferred_element_type=F32)
        cu = cu[:, :D] * cu[:, D:]
        zrow = jnp.zeros((SUBLANE, D), F32)
        cpad[0:SUBLANE, :] = zrow
        cpad[SUBLANE:SUBLANE + SC_TM, :] = cu
        cpad[SUBLANE + SC_TM:2 * SUBLANE + SC_TM, :] = zrow
        pos = lax.broadcasted_iota(jnp.int32, (SC_TM, D), 0) % SEQ
        before = jnp.where(pos == 0, 0.0, cpad[SUBLANE - 1:SUBLANE - 1 + SC_TM, :])
        after = jnp.where(pos == SEQ - 1, 0.0, cpad[SUBLANE + 1:SUBLANE + 1 + SC_TM, :])
        finish(x, h, w0 * before + w1 * cu + w2 * after)

    @pl.when(i >= NCTX // SC_TM)
    def _():
        blocks_per_seq = DEC_SEQ // SC_TM
        r = (i - NCTX // SC_TM) % blocks_per_seq
        x = x_ref[...]
        h = (x * scale + shift).astype(BF16)
        hp = (xp_ref[...] * scale + shift).astype(BF16)
        hn = (xn_ref[...] * scale + shift).astype(BF16)
        hcat = jnp.concatenate([hp, h, hn], axis=0)
        cu = jnp.dot(hcat, wins[:, D:], preferred_element_type=F32)
        cu = cu[:, :D] * cu[:, D:]
        halo_up = jnp.where(r > 0, cu[0:GRID_W, :], 0.0)
        halo_dn = jnp.where(r < blocks_per_seq - 1, cu[GRID_W + SC_TM:, :], 0.0)
        up = jnp.concatenate([halo_up, cu[GRID_W:SC_TM, :]], axis=0)
        dn = jnp.concatenate([cu[2 * GRID_W:GRID_W + SC_TM, :], halo_dn], axis=0)
        finish(x, h, w0 * up + w1 * cu[GRID_W:GRID_W + SC_TM, :] + w2 * dn)


def _short_conv(x, mods, l, j, w_in, w_conv, w_out, ln_g, ln_b):
    halo_per_tile = SC_TM // GRID_W
    n_halo = NTOK // GRID_W
    return pl.pallas_call(
        _sconv_kernel,
        grid=(NTOK // SC_TM,),
        in_specs=[
            pl.BlockSpec((SC_TM, D), lambda i: (i, 0)),
            pl.BlockSpec((GRID_W, D), lambda i: (jnp.maximum(i * halo_per_tile - 1, 0), 0)),
            pl.BlockSpec((GRID_W, D), lambda i: (jnp.minimum((i + 1) * halo_per_tile, n_halo - 1), 0)),
            _mod_spec(l),
            _stacked((D, 3 * D), j, True),
            pl.BlockSpec((3, 1, D), lambda i: (j, 0, 0)),
            _stacked((D, D), j, True),
            _ln_spec(l), _ln_spec(l),
        ],
        out_specs=pl.BlockSpec((SC_TM, D), lambda i: (i, 0)),
        out_shape=jax.ShapeDtypeStruct((NTOK, D), F32),
        scratch_shapes=[pltpu.VMEM((D, 3 * D), BF16), pltpu.VMEM((D, D), BF16),
                        pltpu.VMEM((SC_TM + 2 * SUBLANE, D), F32)],
        compiler_params=_cparams("arbitrary"),
        name="short_conv",
    )(x, x, x, mods, w_in, _vec3(w_conv), w_out, ln_g, ln_b)


N_CH = TM // CHUNK
GLA_FWD_TILES = 2
GLA_BWD_TILES = 2
STATE = (H, DK, DV)


def _state_out_spec(block, j, d, n):
    return pl.BlockSpec((n, None, None) + STATE,
                        lambda i: (jnp.minimum(block(i), CTX_TILES // n - 1), j, d, 0, 0, 0))


def _emit_states(st_ref, finals, t0, create=None):
    @pl.when(t0 < CTX_TILES)
    def _():
        if create is not None:
            st_ref[...] = jnp.zeros(st_ref.shape, F32)
        for sub, final in enumerate(finals):
            for hd in range(H):
                if create is None:
                    st_ref[sub, hd] = final[hd]
                else:
                    st_ref[sub, create, 0, hd] = final[hd]


def _lat_seq(t):
    return jnp.clip((t - CTX_TILES) // TILES_PER_LAT, 0, DEC_BATCH - 1)


def _chunk_tri(rev):
    row = lax.broadcasted_iota(jnp.int32, (TM, TM), 0)
    col = lax.broadcasted_iota(jnp.int32, (TM, TM), 1)
    same = (row // CHUNK) == (col // CHUNK)
    return same & ((col >= row) if rev else (col <= row))


def _scan_states_in(s_scr, s0_ref, tiles, rev):
    first_r = TILES_PER_LAT - 1 if rev else 0

    def enter(prev, t, hd):
        fresh = jnp.logical_and(t >= CTX_TILES, (t - CTX_TILES) % TILES_PER_LAT == first_r)
        return jnp.where(t < CTX_TILES, 0.0, jnp.where(fresh, s0_ref[hd], prev))

    s_in = [enter(s_scr[hd], tiles[0], hd) for hd in range(H)]
    for p, t in enumerate(tiles[1:], start=1):
        s_in += [functools.partial(lambda done, p, t, hd: enter(done[(p - 1) * H + hd], t, hd),
                                   p=p, t=t, hd=hd) for hd in range(H)]
    return s_in


def _decay_operands(q, k, bcum, rev):
    qd, kd, ke, dec = [], [], [], []
    for c in range(N_CH):
        rows = slice(c * CHUNK, (c + 1) * CHUNK)
        b = bcum[rows, :]
        last = b[0:1, :] if rev else b[CHUNK - 1:CHUNK, :]
        qd.append((q[rows, :] * jnp.exp(b)).astype(BF16))
        kd.append((k[rows, :] * jnp.exp(-b)).astype(BF16))
        ke.append(k[rows, :] * jnp.exp(last - b))
        dec.append(jnp.exp(last))
    return (jnp.concatenate(qd, axis=0), jnp.concatenate(kd, axis=0),
            jnp.concatenate(ke, axis=0).T.astype(BF16), dec)


def _scan_tile(heads, s_in, rev):
    mask = _chunk_tri(rev)
    kcol = lax.broadcasted_iota(jnp.int32, (DK, TM), 1) // CHUNK
    order = range(N_CH - 1, -1, -1) if rev else range(N_CH)
    sc, kv = [], []
    for qh, kh, keth, vh, _ in heads:
        sc.append(lax.dot_general(qh, kh, (((1,), (1,)), ((), ())), preferred_element_type=F32))
        kst = jnp.concatenate([jnp.where(kcol == c, keth, jnp.zeros_like(keth)) for c in range(N_CH)],
                              axis=0)
        kv.append(jnp.dot(kst, vh, preferred_element_type=F32))
    o_intra = [jnp.dot(jnp.where(mask, s, 0.0).astype(BF16), hd[3], preferred_element_type=F32)
               for s, hd in zip(sc, heads)]
    outs, s_out = [], []
    for hd, (qh, _, _, _, dec_rows) in enumerate(heads):
        s, o = (s_in[hd](s_out) if callable(s_in[hd]) else s_in[hd]), [None] * N_CH
        for c in order:
            rows = slice(c * CHUNK, (c + 1) * CHUNK)
            o[c] = o_intra[hd][rows, :] + jnp.dot(qh[rows, :], s.astype(BF16),
                                                  preferred_element_type=F32)
            dec_col = jnp.broadcast_to(dec_rows[c], (DK, DK)).T
            s = s * jnp.concatenate([dec_col, dec_col], axis=1) + kv[hd][c * DK:(c + 1) * DK, :]
        outs.append(o)
        s_out.append(s)
    return outs, s_out


def _gla_fwd_kernel(*refs, n_x, create):
    x_refs = refs[:n_x]
    mod_ref, win_ref, wga_ref, wgb_ref, bg_ref, s0_ref = refs[n_x:n_x + 6]
    rest = refs[n_x + (6 if create is not None else 7):]
    ovr_ref, qk_ref, ket_ref, dec_ref, st_ref = rest[:5]
    xall_ref = rest[5] if n_x == 2 else None
    wins, tri_scr, s_scr = rest[5 + (n_x == 2):]
    i = pl.program_id(0)

    @pl.when(i == 0)
    def _():
        _cast_rows(win_ref, wins, 128)
        tri_scr[0] = jnp.where(_chunk_tri(False), 1.0, 0.0).astype(BF16)
        tri_scr[1] = jnp.where(_chunk_tri(True), 1.0, 0.0).astype(BF16)
        s_scr[...] = jnp.zeros_like(s_scr)

    subs = range(GLA_FWD_TILES)
    tiles = [i * GLA_FWD_TILES + sub for sub in subs]
    rows = [slice(sub * TM, (sub + 1) * TM) for sub in subs]

    carry, finals = s_scr, []
    for sub in subs:
        heads = []
        shift, scale = _mod_rows(mod_ref, tiles[sub] * TM, (0, 1))
        xt = _read_x(x_refs, tiles[sub], rows[sub])
        if xall_ref is not None:
            xall_ref[rows[sub], :] = xt
        h = (xt * (1.0 + scale) + shift).astype(BF16)

        g_parts = []
        for d in range(2):
            za = lax.dot_general(h, wga_ref[d].astype(BF16), (((1,), (1,)), ((), ())),
                                 preferred_element_type=F32)
            z = (jnp.dot(za.astype(BF16), wgb_ref[d].astype(BF16), preferred_element_type=F32)
                 + bg_ref[d:d + 1, :])
            g = (jnp.minimum(z, 0.0) - jnp.log1p(jnp.exp(-jnp.abs(z)))) * (1.0 / GATE_NORM)
            g_hi = g.astype(BF16)
            g_parts.append((g_hi, (g - g_hi.astype(F32)).astype(BF16)))

        proj = jnp.dot(h, wins[...], preferred_element_type=F32)
        q = proj[:, :KW] * (DK ** -0.5)
        k = proj[:, KW:2 * KW]
        v = proj[:, 2 * KW:2 * KW + D].astype(BF16)
        ovr_ref[rows[sub], D:2 * D] = v
        ovr_ref[rows[sub], 2 * D:] = proj[:, 2 * KW + D:].astype(BF16)

        bcum = [jnp.dot(tri_scr[d], g_hi, preferred_element_type=F32)
                + jnp.dot(tri_scr[d], g_lo, preferred_element_type=F32)
                for d, (g_hi, g_lo) in enumerate(g_parts)]

        qd_b, kd_b, ket_b, dec_b = _decay_operands(q, k, bcum[1], True)
        qk_ref[rows[sub], :KW] = qd_b
        qk_ref[rows[sub], KW:] = kd_b
        ket_ref[:, rows[sub]] = ket_b
        dec_ref[sub] = jnp.concatenate(dec_b + [jnp.zeros((SUBLANE - N_CH, KW), F32)], axis=0)
        qd, kd, ket, dec = _decay_operands(q, k, bcum[0], False)
        for hd in range(H):
            kc, vc = slice(hd * DK, (hd + 1) * DK), slice(hd * DV, (hd + 1) * DV)
            heads.append((qd[:, kc], kd[:, kc], ket[kc, :], v[:, vc], [e[:, kc] for e in dec]))

        outs, carry = _scan_tile(heads, _scan_states_in(carry, s0_ref, [tiles[sub]], False), False)
        finals.append(carry)
        for hd in range(H):
            for c in range(N_CH):
                r0 = sub * TM + c * CHUNK
                ovr_ref[r0:r0 + CHUNK, hd * DV:(hd + 1) * DV] = outs[hd][c].astype(BF16)
    for hd in range(H):
        s_scr[hd] = carry[hd]
    _emit_states(st_ref, finals, tiles[0], create)


def _gla_fwd(x, mods, l, j, w_in, w_ga, w_gb, b_g, state_gla, new_states):
    xs = _x_args(x)
    n, rows = GLA_FWD_TILES, GLA_FWD_TILES * TM
    tile = pl.BlockSpec((rows, D), lambda i: (i, 0))
    keys = pl.BlockSpec((rows, KW), lambda i: (i, 0))
    n_in = len(xs) + 7
    create = new_states is None
    n_gla = state_gla.shape[1]
    if create:
        st_spec = pl.BlockSpec((n, n_gla, 2) + STATE,
                               lambda i: (jnp.minimum(i, CTX_TILES // n - 1), 0, 0, 0, 0, 0))
        st_in, st_args, alias = [], [], {}
    else:
        st_spec = _state_out_spec(lambda i: i, j, 0, n)
        st_in, st_args, alias = [pl.BlockSpec(memory_space=pl.ANY)], [new_states], {n_in - 1: 4}
    return pl.pallas_call(
        functools.partial(_gla_fwd_kernel, n_x=len(xs), create=j if create else None),
        grid=(NTILE // n,),
        in_specs=_x_specs(len(xs) == 2, rows=rows) + [
            _mod_spec(l),
            _stacked((D, 2 * KW + 2 * D), j, True),
            _stacked((2, RANK, D), j),
            _stacked((2, RANK, KW), j),
            _stacked((2, KW), j),
            pl.BlockSpec((None, None, None) + STATE, lambda i: (_lat_seq(i * n), j, 0, 0, 0, 0)),
        ] + st_in,
        out_specs=[
            pl.BlockSpec((rows, 3 * D), lambda i: (i, 0)),
            pl.BlockSpec((rows, 2 * KW), lambda i: (i, 0)),
            pl.BlockSpec((KW, rows), lambda i: (0, i)),
            pl.BlockSpec((n, SUBLANE, KW), lambda i: (i, 0, 0)),
            st_spec,
        ] + [tile] * (len(xs) == 2),
        out_shape=[
            jax.ShapeDtypeStruct((NTOK, 3 * D), BF16),
            jax.ShapeDtypeStruct((NTOK, 2 * KW), BF16),
            jax.ShapeDtypeStruct((KW, NTOK), BF16),
            jax.ShapeDtypeStruct((NTILE, SUBLANE, KW), F32),
            jax.ShapeDtypeStruct((BATCH, n_gla, 2) + STATE, F32),
        ] + [jax.ShapeDtypeStruct((NTOK, D), F32)] * (len(xs) == 2),
        input_output_aliases=alias,
        scratch_shapes=[pltpu.VMEM((D, 2 * KW + 2 * D), BF16), pltpu.VMEM((2, TM, TM), BF16),
                        pltpu.VMEM(STATE, F32)],
        compiler_params=_cparams("arbitrary"),
        name="gla_fwd",
    )(*xs, mods, w_in, jnp.swapaxes(w_ga, 2, 3), w_gb, b_g, state_gla, *st_args)


def _gla_bwd_kernel(*refs, n_x, j):
    x_refs = refs[:n_x]
    (qk_ref, ket_ref, dec_ref, s0_ref, ovr_ref, mod_ref, gn_ref, wo_ref,
     g_ref, b_ref, _, o_ref, st_ref, wos, s_scr) = refs[n_x:]
    i = pl.program_id(0)
    n = GLA_BWD_TILES
    blk = NTILE // n - 1 - i

    @pl.when(i == 0)
    def _():
        _cast_rows(wo_ref, wos, 128)
        s_scr[...] = jnp.zeros_like(s_scr)

    order = list(range(n - 1, -1, -1))
    tiles = [blk * n + sub for sub in order]
    heads = []
    for sub in order:
        rows = slice(sub * TM, (sub + 1) * TM)
        for hd in range(H):
            kc = slice(hd * DK, (hd + 1) * DK)
            heads.append((qk_ref[rows, kc], qk_ref[rows, KW + hd * DK:KW + (hd + 1) * DK],
                          ket_ref[kc, rows], ovr_ref[rows, D + hd * DV:D + (hd + 1) * DV],
                          [dec_ref[sub, c:c + 1, kc] for c in range(N_CH)]))
    outs, s_out = _scan_tile(heads, _scan_states_in(s_scr, s0_ref, tiles, True), True)

    finals = [None] * n
    for p, sub in enumerate(order):
        rows = slice(sub * TM, (sub + 1) * TM)
        finals[sub] = s_out[p * H:(p + 1) * H]
        (gate,) = _mod_rows(mod_ref, tiles[p] * TM, (2,))
        parts = []
        for hd in range(H):
            oh = (ovr_ref[rows, hd * DV:(hd + 1) * DV].astype(F32)
                  + jnp.concatenate(outs[p * H + hd], axis=0))
            ms = jnp.mean(oh * oh, axis=-1, keepdims=True)
            parts.append(oh * lax.rsqrt(ms + RMS_EPS))
        on = jnp.concatenate(parts, axis=1) * gn_ref[j:j + 1, :]
        y = jnp.dot((on * _silu(ovr_ref[rows, 2 * D:].astype(F32))).astype(BF16), wos[...],
                    preferred_element_type=F32)
        o_ref[rows, :] = _layer_norm(ALPHA * _read_x(x_refs, tiles[p], rows) + gate * y,
                                     g_ref[0:1, :], b_ref[0:1, :])
    for hd in range(H):
        s_scr[hd] = s_out[(n - 1) * H + hd]
    _emit_states(st_ref, finals, blk * n)


def _gla_bwd(x, ovr, qk, ket, dec, state_gla, new_states, mods, l, j, gn_g, w_o, ln_g, ln_b):
    xs = _x_args(x)
    n, rows = GLA_BWD_TILES, GLA_BWD_TILES * TM
    rblk = lambda i: NTILE // n - 1 - i
    tile = pl.BlockSpec((rows, D), lambda i: (rblk(i), 0))
    n_in = len(xs) + 11
    return pl.pallas_call(
        functools.partial(_gla_bwd_kernel, n_x=len(xs), j=j),
        grid=(NTILE // n,),
        in_specs=_x_specs(len(xs) == 2, rblk, rows) + [
            pl.BlockSpec((rows, 2 * KW), lambda i: (rblk(i), 0)),
            pl.BlockSpec((KW, rows), lambda i: (0, rblk(i))),
            pl.BlockSpec((n, SUBLANE, KW), lambda i: (rblk(i), 0, 0)),
            pl.BlockSpec((None, None, None) + STATE,
                         lambda i: (_lat_seq(rblk(i) * n), j, 1, 0, 0, 0)),
            pl.BlockSpec((rows, 3 * D), lambda i: (rblk(i), 0)), _mod_spec(l),
            pl.BlockSpec(gn_g.shape, lambda i: (0, 0)), _stacked((D, D), j, True),
            _ln_spec(l), _ln_spec(l),
            pl.BlockSpec(memory_space=pl.ANY)],
        out_specs=[tile, _state_out_spec(rblk, j, 1, n)],
        out_shape=[jax.ShapeDtypeStruct((NTOK, D), F32), jax.ShapeDtypeStruct(new_states.shape, F32)],
        input_output_aliases={n_in - 1: 1},
        scratch_shapes=[pltpu.VMEM((D, D), BF16), pltpu.VMEM(STATE, F32)],
        compiler_params=_cparams("arbitrary"),
        name="gla_bwd",
    )(*xs, qk, ket, dec, state_gla, ovr, mods, gn_g, w_o, ln_g, ln_b, new_states)


def kernel(x_prompt, x_sample, c, state_gla, c_ctx, mod_w, mod_b, ln_g, ln_b, ff_w1, ff_w2, gla_w_in, gla_w_ga, gla_w_gb, gla_b_g, gla_gn_g, gla_w_o, conf_w_pw1, conf_b_pw1, conf_w_dw, conf_b_dw, conf_ln_g, conf_ln_b, conf_w_pw2, conf_b_pw2, sc_w_in, sc_w_conv, sc_w_out):
    assert x_prompt.shape == (BATCH, SEQ, D) and x_sample.shape == (DEC_BATCH, DEC_SEQ, D)
    x = (x_prompt.reshape(NCTX, D), x_sample.reshape(NLAT, D))

    mods = _adaln(c_ctx, c, mod_w, mod_b)

    states = None
    for l in range(DEPTH):
        kind, j = l % 3, l // 3
        if kind == 0:
            ovr, qk, ket, dec, states, *x_all = _gla_fwd(x, mods, l, j, gla_w_in, gla_w_ga, gla_w_gb,
                                                         gla_b_g, state_gla, states)
            x = x_all[0] if x_all else x
            x, states = _gla_bwd(x, ovr, qk, ket, dec, state_gla, states, mods, l, j,
                                 gla_gn_g, gla_w_o, ln_g, ln_b)
        elif kind == 2:
            x = _short_conv(x, mods, l, j, sc_w_in, sc_w_conv, sc_w_out, ln_g, ln_b)
        conformer = None
        if l + 1 < DEPTH and (l + 1) % 3 == 1:
            conformer = ((l + 1) // 3, conf_w_pw1, conf_b_pw1, conf_w_dw, conf_b_dw, conf_ln_g, conf_ln_b,
                         conf_w_pw2, conf_b_pw2)
        x = _mlp(x, mods, l, ff_w1, ff_w2, ln_g, ln_b, split_out=(l == DEPTH - 1), conformer=conformer)

    y_prompt, y_sample = x
    return (y_prompt.reshape(BATCH, SEQ, D), y_sample.reshape(DEC_BATCH, DEC_SEQ, D), states)
```

```python
import dataclasses
import functools

import jax
import jax.numpy as jnp
from jax import lax
from jax.experimental import pallas as pl
from jax.experimental.pallas import tpu as pltpu

F32 = jnp.float32
BF16 = jnp.bfloat16

D = 1024
DEPTH = 4
BATCH, SEQ = 16, 256
DEC_BATCH, DEC_SEQ = 2, 2048
GRID_W = 64
N_MOD = 6
N_CVEC = 1 + DEC_BATCH
NCTX = BATCH * SEQ
NLAT = DEC_BATCH * DEC_SEQ
NTOK = NCTX + NLAT
H, DK, DV = 4, 128, 256
KW = H * DK
RANK = 16
CHUNK = 64
GATE_NORM = 16.0
CONF_W = 31
D_FF = 4 * D
LN_EPS = 1e-5
RMS_EPS = 1e-6
ALPHA = (2 * DEPTH) ** 0.25

TM = 256
NTILE = NTOK // TM
CTX_TILES = NCTX // TM
TILES_PER_LAT = DEC_SEQ // TM
TM_MLP = 512
TF_MLP = 1024
TC_MLP = 512
TN_MOD = 1024
SUBLANE = 8
VMEM_LIMIT = 58 * 1024 * 1024


def _cparams(*sem):
    return pltpu.CompilerParams(dimension_semantics=sem, vmem_limit_bytes=VMEM_LIMIT)


def _mod_rows(mod_ref, row0, ks):
    m = jnp.where(row0 < NCTX, 0, 1 + (row0 - NCTX) // DEC_SEQ)
    return [mod_ref[pl.ds(m, 1), k * D:(k + 1) * D] for k in ks]


def _mod_spec(l):
    return pl.BlockSpec((None, SUBLANE, N_MOD * D), lambda *_: (l, 0, 0))


def _stacked(block, j, single_buffer=False):
    nd = len(block)
    mode = dict(pipeline_mode=pl.Buffered(1)) if single_buffer else {}
    return pl.BlockSpec((None,) + block, lambda *_: (j,) + (0,) * nd, **mode)


def _x_specs(split, block=lambda i: i, rows=TM):
    if not split:
        return [pl.BlockSpec((rows, D), lambda i: (block(i), 0))]
    nc, nl = NCTX // rows, NLAT // rows
    return [pl.BlockSpec((rows, D), lambda i: (jnp.clip(block(i), 0, nc - 1), 0)),
            pl.BlockSpec((rows, D), lambda i: (jnp.clip(block(i) - nc, 0, nl - 1), 0))]


def _x_args(x):
    return list(x) if isinstance(x, tuple) else [x]


def _read_x(x_refs, t, rows=slice(None)):
    if len(x_refs) == 1:
        return x_refs[0][rows, :]
    return jnp.where(t < CTX_TILES, x_refs[0][rows, :], x_refs[1][rows, :])


def _layer_norm(y, g, b):
    mu = jnp.mean(y, axis=-1, keepdims=True)
    yc = y - mu
    var = jnp.mean(yc * yc, axis=-1, keepdims=True)
    return yc * lax.rsqrt(var + LN_EPS) * g + b


def _silu(x):
    return x * jax.nn.sigmoid(x)


def _cast_rows(src_ref, dst_ref, step):
    n = src_ref.shape[0] // step

    def body(r, c):
        rows = pl.ds(pl.multiple_of(r * step, step), step)
        dst_ref[rows, :] = src_ref[rows, :].astype(BF16)
        return c

    lax.fori_loop(0, n, body, 0)


def _vec3(a):
    return a.reshape(-1, 1, a.shape[-1])


def _ln_spec(l):
    return _stacked((2, D), l)


def _adaln_kernel(cctx_ref, c_ref, w_ref, b_ref, o_ref, c8_scr):
    c8_scr[0:1, :] = cctx_ref[...]
    c8_scr[1:N_CVEC, :] = c_ref[...]
    c8_scr[N_CVEC:, :] = jnp.zeros((SUBLANE - N_CVEC, D), F32)
    s = _silu(c8_scr[...]).astype(BF16)
    bias = b_ref[pl.ds(pl.program_id(0), 1), :]
    o_ref[...] = jnp.dot(s, w_ref[...].astype(BF16), preferred_element_type=F32) + bias


def _adaln(c_ctx, c, mod_w, mod_b):
    return pl.pallas_call(
        _adaln_kernel,
        grid=(DEPTH, N_MOD * D // TN_MOD),
        in_specs=[
            pl.BlockSpec((1, D), lambda l, n: (0, 0)),
            pl.BlockSpec((DEC_BATCH, D), lambda l, n: (0, 0)),
            pl.BlockSpec((None, D, TN_MOD), lambda l, n: (l, 0, n)),
            pl.BlockSpec((DEPTH, TN_MOD), lambda l, n: (0, n)),
        ],
        out_specs=pl.BlockSpec((None, SUBLANE, TN_MOD), lambda l, n: (l, 0, n)),
        out_shape=jax.ShapeDtypeStruct((DEPTH, SUBLANE, N_MOD * D), F32),
        scratch_shapes=[pltpu.VMEM((SUBLANE, D), F32)],
        compiler_params=_cparams("parallel", "parallel"),
        name="adaln",
    )(c_ctx.reshape(1, D), c, mod_w, mod_b)


@dataclasses.dataclass(frozen=True)
class _MlpCfg:
    tm: int
    tf: int
    split_out: bool
    conformer: bool

    @property
    def nj(self):
        return D_FF // self.tf

    @property
    def nt(self):
        return NTOK // self.tm

    def tile(self, s):
        return jnp.clip(s - (self.nj - 1), 0, self.nt - 1)

    def done(self, s):
        return jnp.clip(s - self.nj, 0, self.nt - 1)


N_CONF_IN, N_CONF_SCRATCH = 11, 4


def _sqrelu(a):
    return jnp.square(jnp.maximum(a, 0.0)).astype(BF16)


def _mlp_kernel(*refs, cfg):
    it = iter(refs)
    take = lambda n: [next(it) for _ in range(n)]
    x_ref, xd_ref, mod_ref, w1_ref, w2_ref, g_ref, b_ref = take(7)
    conf_in = take(N_CONF_IN) if cfg.conformer else None
    o_refs = take(2 if cfg.split_out else 1)
    w1s, w2s, h_scr, acc_scr = take(4)
    y_scr = take(1)[0] if cfg.split_out else None
    conf_scr = take(N_CONF_SCRATCH) if cfg.conformer else None
    nj, nt, tm = cfg.nj, cfg.nt, cfg.tm

    s = pl.program_id(0)
    shift, scale = _mod_rows(mod_ref, cfg.tile(s) * tm, (3, 4))
    done = cfg.done(s)

    def finish(emit_matmuls=None):
        (gate,) = _mod_rows(mod_ref, done * tm, (5,))
        y = _layer_norm(ALPHA * xd_ref[...] + gate * acc_scr[...], g_ref[1:2, :], b_ref[1:2, :])
        if cfg.conformer:
            y = _conformer_tile(y, done, conf_in, conf_scr, emit_matmuls)
        if cfg.split_out:
            y_scr[...] = y
        else:
            o_refs[0][...] = y

    def route():
        if cfg.split_out:
            @pl.when(done < NCTX // tm)
            def _():
                o_refs[0][...] = y_scr[...]

            @pl.when(done >= NCTX // tm)
            def _():
                o_refs[1][...] = y_scr[...]

    @pl.when(s < nj)
    def _():
        w1s[s] = w1_ref[...].astype(BF16)
        w2s[s] = w2_ref[...].astype(BF16)
        if cfg.conformer:
            for src, dst in ((conf_in[1], conf_scr[0]), (conf_in[7], conf_scr[1])):
                rows = D // nj
                dst[pl.ds(pl.multiple_of(s * rows, rows), rows), :] = src[...].astype(BF16)

        @pl.when(s == 0)
        def _():
            h_scr[...] = (x_ref[...] * (1.0 + scale) + shift).astype(BF16)
            acc_scr[...] = jnp.zeros_like(acc_scr)

        a = _sqrelu(jnp.dot(h_scr[...], w1s[s], preferred_element_type=F32))
        acc_scr[...] += jnp.dot(a, w2s[s], preferred_element_type=F32)

    @pl.when(jnp.logical_and(s >= nj, s < nj + nt - 1))
    def _():
        h_scr[...] = (x_ref[...] * (1.0 + scale) + shift).astype(BF16)
        chunks = [(j, slice(c * TC_MLP, (c + 1) * TC_MLP))
                  for j in range(nj) for c in range(cfg.tf // TC_MLP)]
        queue = [("up", 0)]
        for k in range(len(chunks)):
            queue += ([("up", k + 1)] if k + 1 < len(chunks) else []) + [("down", k)]
        act, st = {}, dict(pos=0, acc=None)

        def emit_matmuls(n):
            for kind, k in queue[st["pos"]:st["pos"] + n]:
                j, cols = chunks[k]
                if kind == "up":
                    act[k] = _sqrelu(jnp.dot(h_scr[...], w1s[j, :, cols], preferred_element_type=F32))
                else:
                    part = jnp.dot(act.pop(k), w2s[j, cols, :], preferred_element_type=F32)
                    st["acc"] = part if st["acc"] is None else st["acc"] + part
            st["pos"] = min(st["pos"] + n, len(queue))

        emit_matmuls(2)
        finish(lambda: emit_matmuls(1))
        emit_matmuls(len(queue))
        acc_scr[...] = st["acc"]
        route()

    @pl.when(s == nj + nt - 1)
    def _():
        finish()
        route()


def _mlp(x, mods, l, w1, w2, ln_g, ln_b, split_out, conformer=None):
    cfg = _MlpCfg(tm=TM if conformer else TM_MLP, tf=TC_MLP if conformer else TF_MLP,
                  split_out=split_out, conformer=conformer is not None)
    tm, tf, nj = cfg.tm, cfg.tf, cfg.nj
    nc = NCTX // tm
    chunk = lambda s: jnp.minimum(s, nj - 1)
    in_specs = [
        pl.BlockSpec((tm, D), lambda s: (cfg.tile(s), 0)),
        pl.BlockSpec((tm, D), lambda s: (cfg.done(s), 0)),
        _mod_spec(l),
        pl.BlockSpec((None, D, tf), lambda s: (l, 0, chunk(s))),
        pl.BlockSpec((None, tf, D), lambda s: (l, chunk(s), 0)),
        _ln_spec(l), _ln_spec(l),
    ]
    args = [x, x, mods, w1, w2, ln_g, ln_b]
    scratch = [pltpu.VMEM((nj, D, tf), BF16), pltpu.VMEM((nj, tf, D), BF16),
               pltpu.VMEM((tm, D), BF16), pltpu.VMEM((tm, D), F32)]
    if split_out:
        out_specs = [pl.BlockSpec((tm, D), lambda s: (jnp.minimum(cfg.done(s), nc - 1), 0)),
                     pl.BlockSpec((tm, D), lambda s: (jnp.maximum(cfg.done(s) - nc, 0), 0))]
        out_shape = [jax.ShapeDtypeStruct((NCTX, D), F32), jax.ShapeDtypeStruct((NLAT, D), F32)]
        scratch.append(pltpu.VMEM((tm, D), F32))
    else:
        out_specs = pl.BlockSpec((tm, D), lambda s: (cfg.done(s), 0))
        out_shape = jax.ShapeDtypeStruct((NTOK, D), F32)
    if conformer:
        j, cw1, cb1, wdw, bdw, cg, cb, cw2, cb2 = conformer
        slab = lambda cols: pl.BlockSpec((None, D // nj, cols), lambda s: (j, chunk(s), 0))
        in_specs += [_mod_spec(l + 1), slab(2 * D), _stacked((1, 2 * D), j),
                     pl.BlockSpec((CONF_W, 1, D), lambda s: (j, 0, 0)),
                     _stacked((1, D), j), _stacked((1, D), j), _stacked((1, D), j),
                     slab(D), _stacked((1, D), j),
                     _ln_spec(l + 1), _ln_spec(l + 1)]
        args += [mods, cw1, _vec3(cb1), _vec3(wdw), _vec3(bdw), _vec3(cg), _vec3(cb), cw2, _vec3(cb2),
                 ln_g, ln_b]
        scratch += [pltpu.VMEM((D, 2 * D), BF16), pltpu.VMEM((D, D), BF16),
                    pltpu.VMEM((CONF_ROWS, CONF_LB), F32),
                    pltpu.VMEM((SUBLANE - 1, CONF_ROWS, CONF_LB), F32)]
        assert len(in_specs) == 7 + N_CONF_IN
    return pl.pallas_call(
        functools.partial(_mlp_kernel, cfg=cfg),
        grid=(nj + cfg.nt,),
        in_specs=in_specs,
        out_specs=out_specs,
        out_shape=out_shape,
        scratch_shapes=scratch,
        compiler_params=_cparams("arbitrary"),
        name="mlp_conformer" if conformer else "mlp",
    )(*args)


CONF_PAD = 16
CONF_LB = 256
CONF_ROWS = (TM // GRID_W) * (GRID_W + 2 * CONF_PAD)


def _conf_conv(upad_ref, shf_ref, w_taps, u, joined, between=None):
    nseg = TM // GRID_W
    stride = GRID_W + 2 * CONF_PAD
    zpad = jnp.zeros((CONF_PAD, CONF_LB), F32)
    for s in range(nseg):
        base, r0 = s * stride, s * GRID_W
        above = jnp.where(joined, u[r0 - CONF_PAD:r0, :], 0.0) if s > 0 else zpad
        below = jnp.where(joined, u[r0 + GRID_W:r0 + GRID_W + CONF_PAD, :], 0.0) if s < nseg - 1 else zpad
        upad_ref[base:base + CONF_PAD, :] = above
        upad_ref[base + CONF_PAD:base + CONF_PAD + GRID_W, :] = u[r0:r0 + GRID_W, :]
        upad_ref[base + CONF_PAD + GRID_W:base + stride, :] = below
    for b in range(1, SUBLANE):
        shf_ref[b - 1, 0:CONF_ROWS - SUBLANE, :] = upad_ref[b:b + CONF_ROWS - SUBLANE, :]
    out = []
    for s in range(nseg):
        if between is not None:
            between()
        r0 = s * stride + CONF_PAD
        acc = jnp.zeros((GRID_W, CONF_LB), F32)
        for k in range(CONF_W):
            a, b = divmod(k - CONF_W // 2, SUBLANE)
            rows = slice(r0 + SUBLANE * a, r0 + SUBLANE * a + GRID_W)
            acc = acc + w_taps[k] * (upad_ref[rows, :] if b == 0 else shf_ref[b - 1, rows, :])
        out.append(acc)
    return jnp.concatenate(out, axis=0)


def _conformer_tile(x, t, conf_in, conf_scr, between=None):
    mod_ref, _, b1_ref, wdw_ref, bdw_ref, cg_ref, cb_ref, _, b2_ref, g_ref, b_ref = conf_in
    w1s, w2s, upad, shf = conf_scr
    shift, scale, gate = _mod_rows(mod_ref, t * TM, (0, 1, 2))
    h = (x * (1.0 + scale) + shift).astype(BF16)

    def glu(lb):
        cols = slice(lb * CONF_LB, (lb + 1) * CONF_LB)
        gcols = slice(D + lb * CONF_LB, D + (lb + 1) * CONF_LB)
        a = jnp.dot(h, w1s[:, cols], preferred_element_type=F32) + b1_ref[:, cols]
        g = jnp.dot(h, w1s[:, gcols], preferred_element_type=F32) + b1_ref[:, gcols]
        return a * jax.nn.sigmoid(g)

    n_lb = D // CONF_LB
    conv = []
    u = glu(0)
    for lb in range(n_lb):
        u_next = glu(lb + 1) if lb + 1 < n_lb else None
        cols = slice(lb * CONF_LB, (lb + 1) * CONF_LB)
        taps = [wdw_ref[k, :, cols] for k in range(CONF_W)]
        conv.append(_conf_conv(upad, shf, taps, u, t < CTX_TILES, between))
        u = u_next

    hook = between if between is not None else (lambda: None)
    hook()
    uc = jnp.concatenate(conv, axis=1) + bdw_ref[...]
    uc = _silu(_layer_norm(uc, cg_ref[...], cb_ref[...]))
    y = jnp.dot(uc.astype(BF16), w2s[...], preferred_element_type=F32) + b2_ref[...]
    hook()
    return _layer_norm(ALPHA * x + gate * y, g_ref[0:1, :], b_ref[0:1, :])


SC_TM = 2 * TM

def _sconv_kernel(x_ref, xp_ref, xn_ref, mod_ref, win_ref, wc_ref, wout_ref, g_ref, b_ref, o_ref,
                  wins, wouts, cpad):
    i = pl.program_id(0)
    shift, scale, gate = _mod_rows(mod_ref, i * SC_TM, (0, 1, 2))

    @pl.when(i == 0)
    def _():
        _cast_rows(win_ref, wins, 128)
        _cast_rows(wout_ref, wouts, 128)

    scale = 1.0 + scale
    w0, w1, w2 = wc_ref[0], wc_ref[1], wc_ref[2]

    def finish(x, h, y):
        bg = jnp.dot(h, wins[:, :D], preferred_element_type=F32)
        out = jnp.dot((bg * y).astype(BF16), wouts[...], preferred_element_type=F32)
        o_ref[...] = _layer_norm(ALPHA * x + gate * out, g_ref[0:1, :], b_ref[0:1, :])

    @pl.when(i < NCTX // SC_TM)
    def _():
        x = x_ref[...]
        h = (x * scale + shift).astype(BF16)
        cu = jnp.dot(h, wins[:, D:], preferred_element_type=F32)
        cu = cu[:, :D] * cu[:, D:]
        zrow = jnp.zeros((SUBLANE, D), F32)
        cpad[0:SUBLANE, :] = zrow
        cpad[SUBLANE:SUBLANE + SC_TM, :] = cu
        cpad[SUBLANE + SC_TM:2 * SUBLANE + SC_TM, :] = zrow
        pos = lax.broadcasted_iota(jnp.int32, (SC_TM, D), 0) % SEQ
        before = jnp.where(pos == 0, 0.0, cpad[SUBLANE - 1:SUBLANE - 1 + SC_TM, :])
        after = jnp.where(pos == SEQ - 1, 0.0, cpad[SUBLANE + 1:SUBLANE + 1 + SC_TM, :])
        finish(x, h, w0 * before + w1 * cu + w2 * after)

    @pl.when(i >= NCTX // SC_TM)
    def _():
        blocks_per_seq = DEC_SEQ // SC_TM
        r = (i - NCTX // SC_TM) % blocks_per_seq
        x = x_ref[...]
        h = (x * scale + shift).astype(BF16)
        hp = (xp_ref[...] * scale + shift).astype(BF16)
        hn = (xn_ref[...] * scale + shift).astype(BF16)
        hcat = jnp.concatenate([hp, h, hn], axis=0)
        cu = jnp.dot(hcat, wins[:, D:], preferred_element_type=F32)
        cu = cu[:, :D] * cu[:, D:]
        halo_up = jnp.where(r > 0, cu[0:GRID_W, :], 0.0)
        halo_dn = jnp.where(r < blocks_per_seq - 1, cu[GRID_W + SC_TM:, :], 0.0)
        up = jnp.concatenate([halo_up, cu[GRID_W:SC_TM, :]], axis=0)
        dn = jnp.concatenate([cu[2 * GRID_W:GRID_W + SC_TM, :], halo_dn], axis=0)
        finish(x, h, w0 * up + w1 * cu[GRID_W:GRID_W + SC_TM, :] + w2 * dn)


def _short_conv(x, mods, l, j, w_in, w_conv, w_out, ln_g, ln_b):
    halo_per_tile = SC_TM // GRID_W
    n_halo = NTOK // GRID_W
    return pl.pallas_call(
        _sconv_kernel,
        grid=(NTOK // SC_TM,),
        in_specs=[
            pl.BlockSpec((SC_TM, D), lambda i: (i, 0)),
            pl.BlockSpec((GRID_W, D), lambda i: (jnp.maximum(i * halo_per_tile - 1, 0), 0)),
            pl.BlockSpec((GRID_W, D), lambda i: (jnp.minimum((i + 1) * halo_per_tile, n_halo - 1), 0)),
            _mod_spec(l),
            _stacked((D, 3 * D), j, True),
            pl.BlockSpec((3, 1, D), lambda i: (j, 0, 0)),
            _stacked((D, D), j, True),
            _ln_spec(l), _ln_spec(l),
        ],
        out_specs=pl.BlockSpec((SC_TM, D), lambda i: (i, 0)),
        out_shape=jax.ShapeDtypeStruct((NTOK, D), F32),
        scratch_shapes=[pltpu.VMEM((D, 3 * D), BF16), pltpu.VMEM((D, D), BF16),
                        pltpu.VMEM((SC_TM + 2 * SUBLANE, D), F32)],
        compiler_params=_cparams("arbitrary"),
        name="short_conv",
    )(x, x, x, mods, w_in, _vec3(w_conv), w_out, ln_g, ln_b)


N_CH = TM // CHUNK
GLA_FWD_TILES = 2
GLA_BWD_TILES = 2
STATE = (H, DK, DV)


def _state_out_spec(block, j, d, n):
    return pl.BlockSpec((n, None, None) + STATE,
                        lambda i: (jnp.minimum(block(i), CTX_TILES // n - 1), j, d, 0, 0, 0))


def _emit_states(st_ref, finals, t0, create=None):
    @pl.when(t0 < CTX_TILES)
    def _():
        if create is not None:
            st_ref[...] = jnp.zeros(st_ref.shape, F32)
        for sub, final in enumerate(finals):
            for hd in range(H):
                if create is None:
                    st_ref[sub, hd] = final[hd]
                else:
                    st_ref[sub, create, 0, hd] = final[hd]


def _lat_seq(t):
    return jnp.clip((t - CTX_TILES) // TILES_PER_LAT, 0, DEC_BATCH - 1)


def _chunk_tri(rev):
    row = lax.broadcasted_iota(jnp.int32, (TM, TM), 0)
    col = lax.broadcasted_iota(jnp.int32, (TM, TM), 1)
    same = (row // CHUNK) == (col // CHUNK)
    return same & ((col >= row) if rev else (col <= row))


def _scan_states_in(s_scr, s0_ref, tiles, rev):
    first_r = TILES_PER_LAT - 1 if rev else 0

    def enter(prev, t, hd):
        fresh = jnp.logical_and(t >= CTX_TILES, (t - CTX_TILES) % TILES_PER_LAT == first_r)
        return jnp.where(t < CTX_TILES, 0.0, jnp.where(fresh, s0_ref[hd], prev))

    s_in = [enter(s_scr[hd], tiles[0], hd) for hd in range(H)]
    for p, t in enumerate(tiles[1:], start=1):
        s_in += [functools.partial(lambda done, p, t, hd: enter(done[(p - 1) * H + hd], t, hd),
                                   p=p, t=t, hd=hd) for hd in range(H)]
    return s_in


def _decay_operands(q, k, bcum, rev):
    qd, kd, ke, dec = [], [], [], []
    for c in range(N_CH):
        rows = slice(c * CHUNK, (c + 1) * CHUNK)
        b = bcum[rows, :]
        last = b[0:1, :] if rev else b[CHUNK - 1:CHUNK, :]
        qd.append((q[rows, :] * jnp.exp(b)).astype(BF16))
        kd.append((k[rows, :] * jnp.exp(-b)).astype(BF16))
        ke.append(k[rows, :] * jnp.exp(last - b))
        dec.append(jnp.exp(last))
    return (jnp.concatenate(qd, axis=0), jnp.concatenate(kd, axis=0),
            jnp.concatenate(ke, axis=0).T.astype(BF16), dec)


def _scan_tile(heads, s_in, rev):
    mask = _chunk_tri(rev)
    kcol = lax.broadcasted_iota(jnp.int32, (DK, TM), 1) // CHUNK
    order = range(N_CH - 1, -1, -1) if rev else range(N_CH)
    sc, kv = [], []
    for qh, kh, keth, vh, _ in heads:
        sc.append(lax.dot_general(qh, kh, (((1,), (1,)), ((), ())), preferred_element_type=F32))
        kst = jnp.concatenate([jnp.where(kcol == c, keth, jnp.zeros_like(keth)) for c in range(N_CH)],
                              axis=0)
        kv.append(jnp.dot(kst, vh, preferred_element_type=F32))
    o_intra = [jnp.dot(jnp.where(mask, s, 0.0).astype(BF16), hd[3], preferred_element_type=F32)
               for s, hd in zip(sc, heads)]
    outs, s_out = [], []
    for hd, (qh, _, _, _, dec_rows) in enumerate(heads):
        s, o = (s_in[hd](s_out) if callable(s_in[hd]) else s_in[hd]), [None] * N_CH
        for c in order:
            rows = slice(c * CHUNK, (c + 1) * CHUNK)
            o[c] = o_intra[hd][rows, :] + jnp.dot(qh[rows, :], s.astype(BF16),
                                                  preferred_element_type=F32)
            dec_col = jnp.broadcast_to(dec_rows[c], (DK, DK)).T
            s = s * jnp.concatenate([dec_col, dec_col], axis=1) + kv[hd][c * DK:(c + 1) * DK, :]
        outs.append(o)
        s_out.append(s)
    return outs, s_out


def _gla_fwd_kernel(*refs, n_x, create):
    x_refs = refs[:n_x]
    mod_ref, win_ref, wga_ref, wgb_ref, bg_ref, s0_ref = refs[n_x:n_x + 6]
    rest = refs[n_x + (6 if create is not None else 7):]
    of_ref, qd_ref, kd_ref, ket_ref, v_ref, r_ref, dec_ref, st_ref = rest[:8]
    xall_ref = rest[8] if n_x == 2 else None
    wins, tri_scr, s_scr = rest[8 + (n_x == 2):]
    i = pl.program_id(0)

    @pl.when(i == 0)
    def _():
        _cast_rows(win_ref, wins, 128)
        tri_scr[0] = jnp.where(_chunk_tri(False), 1.0, 0.0).astype(BF16)
        tri_scr[1] = jnp.where(_chunk_tri(True), 1.0, 0.0).astype(BF16)
        s_scr[...] = jnp.zeros_like(s_scr)

    subs = range(GLA_FWD_TILES)
    tiles = [i * GLA_FWD_TILES + sub for sub in subs]
    rows = [slice(sub * TM, (sub + 1) * TM) for sub in subs]

    carry, finals = s_scr, []
    for sub in subs:
        heads = []
        shift, scale = _mod_rows(mod_ref, tiles[sub] * TM, (0, 1))
        xt = _read_x(x_refs, tiles[sub], rows[sub])
        if xall_ref is not None:
            xall_ref[rows[sub], :] = xt
        h = (xt * (1.0 + scale) + shift).astype(BF16)

        g_parts = []
        for d in range(2):
            za = lax.dot_general(h, wga_ref[d].astype(BF16), (((1,), (1,)), ((), ())),
                                 preferred_element_type=F32)
            z = (jnp.dot(za.astype(BF16), wgb_ref[d].astype(BF16), preferred_element_type=F32)
                 + bg_ref[d:d + 1, :])
            g = (jnp.minimum(z, 0.0) - jnp.log1p(jnp.exp(-jnp.abs(z)))) * (1.0 / GATE_NORM)
            g_hi = g.astype(BF16)
            g_parts.append((g_hi, (g - g_hi.astype(F32)).astype(BF16)))

        proj = jnp.dot(h, wins[...], preferred_element_type=F32)
        q = proj[:, :KW] * (DK ** -0.5)
        k = proj[:, KW:2 * KW]
        v = proj[:, 2 * KW:2 * KW + D].astype(BF16)
        v_ref[rows[sub], :] = v
        r_ref[rows[sub], :] = proj[:, 2 * KW + D:].astype(BF16)

        bcum = [jnp.dot(tri_scr[d], g_hi, preferred_element_type=F32)
                + jnp.dot(tri_scr[d], g_lo, preferred_element_type=F32)
                for d, (g_hi, g_lo) in enumerate(g_parts)]

        qd_b, kd_b, ket_b, dec_b = _decay_operands(q, k, bcum[1], True)
        qd_ref[rows[sub], :] = qd_b
        kd_ref[rows[sub], :] = kd_b
        ket_ref[:, rows[sub]] = ket_b
        dec_ref[sub] = jnp.concatenate(dec_b + [jnp.zeros((SUBLANE - N_CH, KW), F32)], axis=0)
        qd, kd, ket, dec = _decay_operands(q, k, bcum[0], False)
        for hd in range(H):
            kc, vc = slice(hd * DK, (hd + 1) * DK), slice(hd * DV, (hd + 1) * DV)
            heads.append((qd[:, kc], kd[:, kc], ket[kc, :], v[:, vc], [e[:, kc] for e in dec]))

        outs, carry = _scan_tile(heads, _scan_states_in(carry, s0_ref, [tiles[sub]], False), False)
        finals.append(carry)
        for hd in range(H):
            for c in range(N_CH):
                r0 = sub * TM + c * CHUNK
                of_ref[r0:r0 + CHUNK, hd * DV:(hd + 1) * DV] = outs[hd][c].astype(BF16)
    for hd in range(H):
        s_scr[hd] = carry[hd]
    _emit_states(st_ref, finals, tiles[0], create)


def _gla_fwd(x, mods, l, j, w_in, w_ga, w_gb, b_g, state_gla, new_states):
    xs = _x_args(x)
    n, rows = GLA_FWD_TILES, GLA_FWD_TILES * TM
    tile = pl.BlockSpec((rows, D), lambda i: (i, 0))
    keys = pl.BlockSpec((rows, KW), lambda i: (i, 0))
    n_in = len(xs) + 7
    create = new_states is None
    n_gla = state_gla.shape[1]
    if create:
        st_spec = pl.BlockSpec((n, n_gla, 2) + STATE,
                               lambda i: (jnp.minimum(i, CTX_TILES // n - 1), 0, 0, 0, 0, 0))
        st_in, st_args, alias = [], [], {}
    else:
        st_spec = _state_out_spec(lambda i: i, j, 0, n)
        st_in, st_args, alias = [pl.BlockSpec(memory_space=pl.ANY)], [new_states], {n_in - 1: 7}
    return pl.pallas_call(
        functools.partial(_gla_fwd_kernel, n_x=len(xs), create=j if create else None),
        grid=(NTILE // n,),
        in_specs=_x_specs(len(xs) == 2, rows=rows) + [
            _mod_spec(l),
            _stacked((D, 2 * KW + 2 * D), j, True),
            _stacked((2, RANK, D), j),
            _stacked((2, RANK, KW), j),
            _stacked((2, KW), j),
            pl.BlockSpec((None, None, None) + STATE, lambda i: (_lat_seq(i * n), j, 0, 0, 0, 0)),
        ] + st_in,
        out_specs=[
            tile, keys, keys,
            pl.BlockSpec((KW, rows), lambda i: (0, i)),
            tile, tile,
            pl.BlockSpec((n, SUBLANE, KW), lambda i: (i, 0, 0)),
            st_spec,
        ] + [pl.BlockSpec((rows, D), lambda i: (i, 0))] * (len(xs) == 2),
        out_shape=[
            jax.ShapeDtypeStruct((NTOK, D), BF16),
            jax.ShapeDtypeStruct((NTOK, KW), BF16),
            jax.ShapeDtypeStruct((NTOK, KW), BF16),
            jax.ShapeDtypeStruct((KW, NTOK), BF16),
            jax.ShapeDtypeStruct((NTOK, D), BF16),
            jax.ShapeDtypeStruct((NTOK, D), BF16),
            jax.ShapeDtypeStruct((NTILE, SUBLANE, KW), F32),
            jax.ShapeDtypeStruct((BATCH, n_gla, 2) + STATE, F32),
        ] + [jax.ShapeDtypeStruct((NTOK, D), F32)] * (len(xs) == 2),
        input_output_aliases=alias,
        scratch_shapes=[pltpu.VMEM((D, 2 * KW + 2 * D), BF16), pltpu.VMEM((2, TM, TM), BF16),
                        pltpu.VMEM(STATE, F32)],
        compiler_params=_cparams("arbitrary"),
        name="gla_fwd",
    )(*xs, mods, w_in, jnp.swapaxes(w_ga, 2, 3), w_gb, b_g, state_gla, *st_args)


def _gla_bwd_kernel(*refs, n_x, j):
    x_refs = refs[:n_x]
    (qd_ref, kd_ref, ket_ref, v_ref, dec_ref, s0_ref, of_ref, r_ref, mod_ref, gn_ref, wo_ref,
     g_ref, b_ref, _, o_ref, st_ref, wos, s_scr) = refs[n_x:]
    i = pl.program_id(0)
    n = GLA_BWD_TILES
    blk = NTILE // n - 1 - i

    @pl.when(i == 0)
    def _():
        _cast_rows(wo_ref, wos, 128)
        s_scr[...] = jnp.zeros_like(s_scr)

    order = list(range(n - 1, -1, -1))
    tiles = [blk * n + sub for sub in order]
    heads = []
    for sub in order:
        rows = slice(sub * TM, (sub + 1) * TM)
        for hd in range(H):
            kc, vc = slice(hd * DK, (hd + 1) * DK), slice(hd * DV, (hd + 1) * DV)
            heads.append((qd_ref[rows, kc], kd_ref[rows, kc], ket_ref[kc, rows], v_ref[rows, vc],
                          [dec_ref[sub, c:c + 1, kc] for c in range(N_CH)]))
    outs, s_out = _scan_tile(heads, _scan_states_in(s_scr, s0_ref, tiles, True), True)

    finals = [None] * n
    for p, sub in enumerate(order):
        rows = slice(sub * TM, (sub + 1) * TM)
        finals[sub] = s_out[p * H:(p + 1) * H]
        (gate,) = _mod_rows(mod_ref, tiles[p] * TM, (2,))
        parts = []
        for hd in range(H):
            oh = (of_ref[rows, hd * DV:(hd + 1) * DV].astype(F32)
                  + jnp.concatenate(outs[p * H + hd], axis=0))
            ms = jnp.mean(oh * oh, axis=-1, keepdims=True)
            parts.append(oh * lax.rsqrt(ms + RMS_EPS))
        on = jnp.concatenate(parts, axis=1) * gn_ref[j:j + 1, :]
        y = jnp.dot((on * _silu(r_ref[rows, :].astype(F32))).astype(BF16), wos[...],
                    preferred_element_type=F32)
        o_ref[rows, :] = _layer_norm(ALPHA * _read_x(x_refs, tiles[p], rows) + gate * y,
                                     g_ref[0:1, :], b_ref[0:1, :])
    for hd in range(H):
        s_scr[hd] = s_out[(n - 1) * H + hd]
    _emit_states(st_ref, finals, blk * n)


def _gla_bwd(x, o_f, qd, kd, ket, v, r, dec, state_gla, new_states, mods, l, j, gn_g, w_o, ln_g, ln_b):
    xs = _x_args(x)
    n, rows = GLA_BWD_TILES, GLA_BWD_TILES * TM
    rblk = lambda i: NTILE // n - 1 - i
    tile = pl.BlockSpec((rows, D), lambda i: (rblk(i), 0))
    keys = pl.BlockSpec((rows, KW), lambda i: (rblk(i), 0))
    n_in = len(xs) + 14
    return pl.pallas_call(
        functools.partial(_gla_bwd_kernel, n_x=len(xs), j=j),
        grid=(NTILE // n,),
        in_specs=_x_specs(len(xs) == 2, rblk, rows) + [
            keys, keys,
            pl.BlockSpec((KW, rows), lambda i: (0, rblk(i))),
            tile,
            pl.BlockSpec((n, SUBLANE, KW), lambda i: (rblk(i), 0, 0)),
            pl.BlockSpec((None, None, None) + STATE,
                         lambda i: (_lat_seq(rblk(i) * n), j, 1, 0, 0, 0)),
            tile, tile, _mod_spec(l),
            pl.BlockSpec(gn_g.shape, lambda i: (0, 0)), _stacked((D, D), j, True),
            _ln_spec(l), _ln_spec(l),
            pl.BlockSpec(memory_space=pl.ANY)],
        out_specs=[tile, _state_out_spec(rblk, j, 1, n)],
        out_shape=[jax.ShapeDtypeStruct((NTOK, D), F32), jax.ShapeDtypeStruct(new_states.shape, F32)],
        input_output_aliases={n_in - 1: 1},
        scratch_shapes=[pltpu.VMEM((D, D), BF16), pltpu.VMEM(STATE, F32)],
        compiler_params=_cparams("arbitrary"),
        name="gla_bwd",
    )(*xs, qd, kd, ket, v, dec, state_gla, o_f, r, mods, gn_g, w_o, ln_g, ln_b,
      new_states)


def kernel(x_prompt, x_sample, c, state_gla, c_ctx, mod_w, mod_b, ln_g, ln_b, ff_w1, ff_w2, gla_w_in, gla_w_ga, gla_w_gb, gla_b_g, gla_gn_g, gla_w_o, conf_w_pw1, conf_b_pw1, conf_w_dw, conf_b_dw, conf_ln_g, conf_ln_b, conf_w_pw2, conf_b_pw2, sc_w_in, sc_w_conv, sc_w_out):
    assert x_prompt.shape == (BATCH, SEQ, D) and x_sample.shape == (DEC_BATCH, DEC_SEQ, D)
    x = (x_prompt.reshape(NCTX, D), x_sample.reshape(NLAT, D))

    mods = _adaln(c_ctx, c, mod_w, mod_b)

    states = None
    for l in range(DEPTH):
        kind, j = l % 3, l // 3
        if kind == 0:
            o_f, qd, kd, ket, v, r, dec, states, *x_all = _gla_fwd(x, mods, l, j, gla_w_in, gla_w_ga,
                                                                   gla_w_gb, gla_b_g, state_gla, states)
            x = x_all[0] if x_all else x
            x, states = _gla_bwd(x, o_f, qd, kd, ket, v, r, dec, state_gla, states, mods, l, j,
                                 gla_gn_g, gla_w_o, ln_g, ln_b)
        elif kind == 2:
            x = _short_conv(x, mods, l, j, sc_w_in, sc_w_conv, sc_w_out, ln_g, ln_b)
        conformer = None
        if l + 1 < DEPTH and (l + 1) % 3 == 1:
            conformer = ((l + 1) // 3, conf_w_pw1, conf_b_pw1, conf_w_dw, conf_b_dw, conf_ln_g, conf_ln_b,
                         conf_w_pw2, conf_b_pw2)
        x = _mlp(x, mods, l, ff_w1, ff_w2, ln_g, ln_b, split_out=(l == DEPTH - 1), conformer=conformer)

    y_prompt, y_sample = x
    return (y_prompt.reshape(BATCH, SEQ, D), y_sample.reshape(DEC_BATCH, DEC_SEQ, D), states)
```
